```python
import math
import jax, jax.numpy as jnp
from jax import lax
import numpy as np

D_MODEL = 1024
BATCH = 8
SEQ = 2048
DEPTH = 2
DEC_BATCH = 32
DEC_SEQ = 8
PAST_LEN = 8192
PAGE_SIZE = 128

N_EVEN = (DEPTH + 1) // 2
N_ODD = DEPTH // 2
RMS_EPS = 1e-6
D_A = D_MODEL // 2
POOL_WINDOWS = (2, 4, 8, 16)
POOL_GROUP = D_A // len(POOL_WINDOWS)
POOL_HIST = max(POOL_WINDOWS) - 1
H_B = 4
HD_B = 64
D_B = H_B * 2 * HD_B
Q_BLOCK = 128
D_C = D_MODEL // 2
CONV_W = 3
D_WINDOWS = ((128, 1), (512, 4), (2048, 16))
H_DG = 8
HD_D = 64
D_DG = H_DG * HD_D
D_FF = 4 * D_MODEL
D_IN_AB = D_A + 3 * D_B
D_IN_CD = 3 * D_C + 3 * len(D_WINDOWS) * D_DG
D_MIX_AB = D_A + D_B
D_MIX_CD = D_C + D_DG

kernel_name = "hybrid_pool_diffattn_shortconv_dilated_step"


def rmsnorm(x, g):
    xf = x.astype(jnp.float32)
    y = xf * lax.rsqrt(jnp.mean(xf * xf, axis=-1, keepdims=True) + RMS_EPS)
    return (y * g.astype(jnp.float32)).astype(x.dtype)


def mlp(h, w_up, w_down):
    return jnp.square(jax.nn.relu(h @ w_up)) @ w_down


def pool_mix(u, hist, pos0, a_mix, a_scale):
    B, T, _ = u.shape
    ext = jnp.concatenate([hist.astype(u.dtype), u], axis=1)
    cs = jnp.pad(jnp.cumsum(ext.astype(jnp.float32), axis=1), ((0, 0), (1, 0), (0, 0)))
    end = cs[:, POOL_HIST + 1:POOL_HIST + 1 + T]
    pos = pos0 + jnp.arange(T)
    means = []
    for gi, w in enumerate(POOL_WINDOWS):
        sl = slice(gi * POOL_GROUP, (gi + 1) * POOL_GROUP)
        start = cs[:, POOL_HIST + 1 - w:POOL_HIST + 1 - w + T, sl]
        cnt = jnp.minimum(pos + 1, w).astype(jnp.float32)[None, :, None]
        means.append((end[..., sl] - start) / cnt)
    p = (jnp.concatenate(means, axis=-1) - u.astype(jnp.float32)).reshape(B, T, len(POOL_WINDOWS), POOL_GROUP)
    y = jnp.einsum('btgc,gcd->btgd', p, a_mix.astype(jnp.float32)).reshape(B, T, D_A) * a_scale.astype(jnp.float32)
    return y.astype(u.dtype), ext[:, -POOL_HIST:]


def short_conv(z, hist, w):
    T = z.shape[1]
    ext = jnp.concatenate([hist.astype(z.dtype), z], axis=1)
    y = ext[:, 0:T] * w[0]
    for j in range(1, CONV_W):
        y = y + ext[:, j:j + T] * w[j]
    return y, ext[:, -(CONV_W - 1):]


def diff_attn_core(q, ks, vs, masks, lam, lam_init, subln_g):
    B, Tq = q.shape[:2]
    logits = jnp.concatenate([jnp.einsum('bqhmd,bkhmd->bhmqk', q, k) for k in ks], axis=-1).astype(jnp.float32) * (HD_B ** -0.5)
    logits = jnp.where(jnp.concatenate(masks, axis=-1), logits, -jnp.inf)
    p = jax.nn.softmax(logits, axis=-1)
    a = p[:, :, 0] - lam * p[:, :, 1]
    o = None
    off = 0
    for v in vs:
        n = v.shape[1]
        part = jnp.einsum('bhqk,bkhe->bqhe', a[..., off:off + n].astype(v.dtype), v)
        o = part if o is None else o + part
        off += n
    o = rmsnorm(o, subln_g) * (1.0 - lam_init)
    return o.reshape(B, Tq, D_B)


def diff_attn_prompt(q, k, v, lam, lam_init, subln_g):
    B, S = q.shape[:2]
    nb = S // Q_BLOCK
    qb = jnp.moveaxis(q.reshape(B, nb, Q_BLOCK, H_B, 2, HD_B), 1, 0)
    qpos = jnp.arange(S).reshape(nb, Q_BLOCK)
    kpos = jnp.arange(S)

    def blk(args):
        qi, pi = args
        return diff_attn_core(qi, [k], [v], [kpos[None, :] <= pi[:, None]], lam, lam_init, subln_g)

    o = lax.map(blk, (qb, qpos))
    return jnp.moveaxis(o, 0, 1).reshape(B, S, D_B)


def dilated_prompt(q, k, v, dil, nback):
    B, S, H, E = q.shape
    L = S // dil
    nb = -(-L // nback)
    Lp = nb * nback

    def streams(a):
        a = a.reshape(B, L, dil, H, E).transpose(0, 2, 1, 3, 4)
        return jnp.pad(a, ((0, 0), (0, 0), (0, Lp - L), (0, 0), (0, 0))).reshape(B, dil, nb, nback, H, E)

    def with_prev(a):
        prev = jnp.pad(a, ((0, 0), (0, 0), (1, 0), (0, 0), (0, 0), (0, 0)))[:, :, :-1]
        return jnp.concatenate([prev, a], axis=3)

    qb = streams(q)
    kc = with_prev(streams(k))
    vc = with_prev(streams(v))
    logits = jnp.einsum('brnqhe,brnkhe->brnhqk', qb, kc).astype(jnp.float32) * (HD_D ** -0.5)
    qi = jnp.arange(nback)[:, None]
    kk = jnp.arange(2 * nback)[None, :]
    dist = qi - kk + nback
    band = (dist >= 0) & (dist <= nback)
    valid = (jnp.arange(nb)[:, None, None] * nback + kk[None] - nback) >= 0
    mask = band[None] & valid
    logits = jnp.where(mask[:, None], logits, -jnp.inf)
    lse = jax.nn.logsumexp(logits, axis=-1)
    p = jnp.exp(logits - lse[..., None])
    o = jnp.einsum('brnhqk,brnkhe->brnqhe', p.astype(v.dtype), vc)
    o = o.reshape(B, dil, Lp, H, E)[:, :, :L].transpose(0, 2, 1, 3, 4).reshape(B, S, H, E)
    lse = jnp.moveaxis(lse, -1, 3).reshape(B, dil, Lp, H)[:, :, :L].transpose(0, 2, 1, 3).reshape(B, S, H)
    return o, lse


def dilated_sample(q, k_ext, v_ext, dil, nback):
    B, T = q.shape[:2]
    Lb = k_ext.shape[1] - T
    idx = Lb + jnp.arange(T)[:, None] - dil * jnp.arange(nback + 1)[None, :]
    valid = idx >= 0
    idx = jnp.maximum(idx, 0)
    kg = k_ext[:, idx]
    vg = v_ext[:, idx]
    logits = jnp.einsum('bqhe,bqkhe->bhqk', q, kg).astype(jnp.float32) * (HD_D ** -0.5)
    logits = jnp.where(valid, logits, -jnp.inf)
    lse = jax.nn.logsumexp(logits, axis=-1)
    p = jnp.exp(logits - lse[..., None])
    o = jnp.einsum('bhqk,bqkhe->bqhe', p.astype(v_ext.dtype), vg)
    return o, jnp.moveaxis(lse, 1, 2)


def even_mixer(h, w_in, w_out, a_mix, a_scale, b_lam, b_subln_g, lam_init, pool_hist, pos0, kv_past):
    B, T, _ = h.shape
    proj = h @ w_in
    u = proj[..., :D_A]
    q = proj[..., D_A:D_A + D_B].reshape(B, T, H_B, 2, HD_B)
    k_flat = proj[..., D_A + D_B:D_A + 2 * D_B].reshape(B, T, H_B, 2 * HD_B)
    v = proj[..., D_A + 2 * D_B:].reshape(B, T, H_B, 2 * HD_B)
    k = k_flat.reshape(B, T, H_B, 2, HD_B)
    a_out, new_hist = pool_mix(u, pool_hist, pos0, a_mix, a_scale)
    lf = b_lam.astype(jnp.float32)
    lam = jnp.exp(jnp.sum(lf[0] * lf[1])) - jnp.exp(jnp.sum(lf[2] * lf[3])) + lam_init
    if kv_past is None:
        o_b = diff_attn_prompt(q, k, v, lam, lam_init, b_subln_g)
    else:
        P = kv_past.shape[1]
        k_past = kv_past[:, :, 0].reshape(B, P, H_B, 2, HD_B).astype(k.dtype)
        v_past = kv_past[:, :, 1].astype(v.dtype)
        masks = [jnp.ones((T, P), bool), jnp.tril(jnp.ones((T, T), bool))]
        o_b = diff_attn_core(q, [k_past, k], [v_past, v], masks, lam, lam_init, b_subln_g)
    y = jnp.concatenate([a_out, o_b], axis=-1) @ w_out
    return y, new_hist, jnp.stack([k_flat, v], axis=2)


def odd_mixer(h, w_in, w_out, conv_w, conv_hist, d_bufs):
    B, T, _ = h.shape
    proj = h @ w_in
    b_gate = proj[..., :D_C]
    c_gate = proj[..., D_C:2 * D_C]
    hc = proj[..., 2 * D_C:3 * D_C]
    cv, new_conv = short_conv(c_gate * hc, conv_hist, conv_w)
    c_out = b_gate * cv
    outs, lses, new_bufs = [], [], []
    for g, (win, dil) in enumerate(D_WINDOWS):
        off = 3 * D_C + 3 * g * D_DG
        q = proj[..., off:off + D_DG].reshape(B, T, H_DG, HD_D)
        kv = proj[..., off + D_DG:off + 3 * D_DG].reshape(B, T, 2, H_DG, HD_D)
        nback = win // dil
        if d_bufs is None:
            o, lse = dilated_prompt(q, kv[:, :, 0], kv[:, :, 1], dil, nback)
            new_bufs.append(kv[:, T - min(win, T):])
        else:
            ext = jnp.concatenate([d_bufs[g].astype(kv.dtype), kv], axis=1)
            o, lse = dilated_sample(q, ext[:, :, 0], ext[:, :, 1], dil, nback)
            new_bufs.append(ext[:, T:])
        outs.append(o)
        lses.append(lse)
    wg = jax.nn.softmax(jnp.stack(lses, axis=0), axis=0)
    o_d = jnp.einsum('gbth,gbthe->bthe', wg.astype(h.dtype), jnp.stack(outs, axis=0)).reshape(B, T, D_DG)
    y = jnp.concatenate([c_out, o_d], axis=-1) @ w_out
    return y, new_conv, new_bufs


def setup_inputs(seed: int = 0) -> dict:
    key = jax.random.key(seed)
    ks = jax.random.split(key, 24)
    f32 = jnp.float32
    n_pages = PAST_LEN // PAGE_SIZE
    n_pool = (DEC_BATCH * n_pages * 5) // 4

    def nrm(k, shape, scale=1.0):
        return jax.random.normal(k, shape, f32) * scale

    page_table = jax.random.permutation(ks[0], n_pool)[:DEC_BATCH * n_pages].reshape(DEC_BATCH, n_pages).astype(jnp.int32)
    d_bufs = [nrm(ks[1 + g], (N_ODD, DEC_BATCH, min(w, PAST_LEN), 2, H_DG, HD_D)) for g, (w, _) in enumerate(D_WINDOWS)]
    return {
        'x_prompt': nrm(ks[4], (BATCH, SEQ, D_MODEL)),
        'x_sample': nrm(ks[5], (DEC_BATCH, DEC_SEQ, D_MODEL)),
        'state_a_pool': nrm(ks[6], (N_EVEN, DEC_BATCH, POOL_HIST, D_A)),
        'cache_b_kv': nrm(ks[7], (N_EVEN, n_pool, PAGE_SIZE, 2, H_B, 2 * HD_B)),
        'state_c_conv': nrm(ks[8], (N_ODD, DEC_BATCH, CONV_W - 1, D_C)),
        'cache_d0_kv': d_bufs[0],
        'cache_d1_kv': d_bufs[1],
        'cache_d2_kv': d_bufs[2],
        'page_table': page_table,
        'norm_mix_g': 1.0 + nrm(ks[9], (DEPTH, D_MODEL), 0.05),
        'norm_mlp_g': 1.0 + nrm(ks[10], (DEPTH, D_MODEL), 0.05),
        'norm_out_g': 1.0 + nrm(ks[11], (D_MODEL,), 0.05),
        'w_in_ab': nrm(ks[12], (N_EVEN, D_MODEL, D_IN_AB), D_MODEL ** -0.5),
        'w_out_ab': nrm(ks[13], (N_EVEN, D_MIX_AB, D_MODEL), D_MIX_AB ** -0.5),
        'a_mix': nrm(ks[14], (N_EVEN, len(POOL_WINDOWS), POOL_GROUP, POOL_GROUP), POOL_GROUP ** -0.5),
        'a_scale': 1.0 + nrm(ks[15], (N_EVEN, D_A), 0.1),
        'b_lam': nrm(ks[16], (N_EVEN, 4, HD_B), 0.1),
        'b_subln_g': 1.0 + nrm(ks[17], (N_EVEN, 2 * HD_B), 0.05),
        'w_in_cd': nrm(ks[18], (N_ODD, D_MODEL, D_IN_CD), D_MODEL ** -0.5),
        'w_out_cd': nrm(ks[19], (N_ODD, D_MIX_CD, D_MODEL), D_MIX_CD ** -0.5),
        'c_conv_w': nrm(ks[20], (N_ODD, CONV_W, D_C), CONV_W ** -0.5),
        'w_up': nrm(ks[21], (DEPTH, D_MODEL, D_FF), D_MODEL ** -0.5),
        'w_down': nrm(ks[22], (DEPTH, D_FF, D_MODEL), D_FF ** -0.5),
    }


def reference(x_prompt, x_sample, state_a_pool, cache_b_kv, state_c_conv, cache_d0_kv, cache_d1_kv, cache_d2_kv, page_table,
              norm_mix_g, norm_mlp_g, norm_out_g, w_in_ab, w_out_ab, a_mix, a_scale, b_lam, b_subln_g,
              w_in_cd, w_out_cd, c_conv_w, w_up, w_down):
    Bp = x_prompt.shape[0]
    Bs = x_sample.shape[0]
    past_len = page_table.shape[1] * cache_b_kv.shape[2]
    d_caches = (cache_d0_kv, cache_d1_kv, cache_d2_kv)
    xp, xs = x_prompt, x_sample
    pool_p, pool_s, bkv_p, bkv_s, conv_p, conv_s = [], [], [], [], [], []
    dbuf_p = [[] for _ in D_WINDOWS]
    dbuf_s = [[] for _ in D_WINDOWS]
    for layer in range(DEPTH):
        i = layer // 2
        hp = rmsnorm(xp, norm_mix_g[layer])
        hs = rmsnorm(xs, norm_mix_g[layer])
        if layer % 2 == 0:
            lam_init = 0.8 - 0.6 * math.exp(-0.3 * layer)
            kv_past = cache_b_kv[i][page_table].reshape(Bs, past_len, 2, H_B, 2 * HD_B)
            zero_hist = jnp.zeros((Bp, POOL_HIST, D_A), xp.dtype)
            yp, hist_p, kvp = even_mixer(hp, w_in_ab[i], w_out_ab[i], a_mix[i], a_scale[i], b_lam[i], b_subln_g[i],
                                         lam_init, zero_hist, 0, None)
            ys, hist_s, kvs = even_mixer(hs, w_in_ab[i], w_out_ab[i], a_mix[i], a_scale[i], b_lam[i], b_subln_g[i],
                                         lam_init, state_a_pool[i], past_len, kv_past)
            pool_p.append(hist_p)
            pool_s.append(hist_s)
            bkv_p.append(kvp)
            bkv_s.append(kvs)
        else:
            zero_conv = jnp.zeros((Bp, CONV_W - 1, D_C), xp.dtype)
            yp, cvp, bufs_p = odd_mixer(hp, w_in_cd[i], w_out_cd[i], c_conv_w[i], zero_conv, None)
            ys, cvs, bufs_s = odd_mixer(hs, w_in_cd[i], w_out_cd[i], c_conv_w[i], state_c_conv[i],
                                        [c[i] for c in d_caches])
            conv_p.append(cvp)
            conv_s.append(cvs)
            for g in range(len(D_WINDOWS)):
                dbuf_p[g].append(bufs_p[g])
                dbuf_s[g].append(bufs_s[g])
        xp = xp + yp
        xs = xs + ys
        xp = xp + mlp(rmsnorm(xp, norm_mlp_g[layer]), w_up[layer], w_down[layer])
        xs = xs + mlp(rmsnorm(xs, norm_mlp_g[layer]), w_up[layer], w_down[layer])
    y_prompt = rmsnorm(xp, norm_out_g)
    y_sample = rmsnorm(xs, norm_out_g)
    return (y_prompt, y_sample,
            jnp.stack(pool_p), jnp.stack(pool_s),
            jnp.stack(bkv_p), jnp.stack(bkv_s),
            jnp.stack(conv_p), jnp.stack(conv_s),
            jnp.stack(dbuf_p[0]), jnp.stack(dbuf_s[0]),
            jnp.stack(dbuf_p[1]), jnp.stack(dbuf_s[1]),
            jnp.stack(dbuf_p[2]), jnp.stack(dbuf_s[2]))
```

```python
import functools
import math

import jax
import jax.numpy as jnp
from jax import lax
from jax.experimental import pallas as pl
from jax.experimental.pallas import tpu as pltpu

F32 = jnp.float32
BF16 = jnp.bfloat16

D_MODEL = 1024
RMS_EPS = 1e-6
D_A = 512
POOL_WINDOWS = (2, 4, 8, 16)
POOL_GROUP = 128
POOL_HIST = 15
H_B = 4
HD_B = 64
D_B = 512
D_C = 512
CONV_W = 3
D_WINDOWS = ((128, 1), (512, 4), (2048, 16))
N_BACK = 128
H_DG = 8
HD_D = 64
D_DG = 512
D_FF = 4096
PAGE_SIZE = 128

LANES = 128
SUBLANES = 8
VMEM_LIMIT_BYTES = 56 * 1024 * 1024
NEG_BIG = -1e30

QK_SCALE_B = HD_B ** -0.5
QK_SCALE_D = HD_D ** -0.5


def _params(*sem):
    return pltpu.CompilerParams(dimension_semantics=sem, vmem_limit_bytes=VMEM_LIMIT_BYTES)


def _resident(shape):
    nd = len(shape)
    return pl.BlockSpec(shape, lambda *_: (0,) * nd, pipeline_mode=pl.Buffered(1))


def _rmsnorm(x, g):
    return x * lax.rsqrt(jnp.mean(x * x, axis=-1, keepdims=True) + RMS_EPS) * g


def _dot(a, b):
    return jnp.dot(a, b, preferred_element_type=F32)


def _dot_nt(a, b):
    return lax.dot_general(a, b, (((1,), (1,)), ((), ())), preferred_element_type=F32)


def _split_heads_stack(q, half):
    lane = lax.broadcasted_iota(jnp.int32, q.shape, 1)
    zero = jnp.zeros_like(q)
    return jnp.concatenate([jnp.where(lane < half, q, zero), jnp.where(lane >= half, q, zero)], axis=0)


def _diff_lambda(blam, lam_init):
    a = jnp.sum(blam[0:1] * blam[1:2], axis=-1, keepdims=True)
    b = jnp.sum(blam[2:3] * blam[3:4], axis=-1, keepdims=True)
    return jnp.exp(a) - jnp.exp(b) + lam_init


def _proj_ab_kernel(x_ref, g_ref, w_ref, u_ref, q_ref, kv_ref, k_ref, v_ref):
    h = _rmsnorm(x_ref[...], g_ref[...]).astype(BF16)
    u_ref[...] = _dot(h, w_ref[:, 0:D_A])
    q_ref[...] = (_dot(h, w_ref[:, D_A:D_A + D_B]) * QK_SCALE_B).astype(q_ref.dtype)
    kv = _dot(h, w_ref[:, D_A + D_B:D_A + 3 * D_B])
    kv_ref[...] = kv
    k_ref[...] = kv[:, :D_B].astype(BF16)
    v_ref[...] = kv[:, D_B:].astype(BF16)


def _proj_ab(x, g, w, tm, q_dtype):
    m = x.shape[0]
    n_in = w.shape[1]
    row = lambda n: pl.BlockSpec((tm, n), lambda i: (i, 0))
    return pl.pallas_call(
        _proj_ab_kernel,
        grid=(m // tm,),
        in_specs=[row(D_MODEL), _resident((1, D_MODEL)), _resident((D_MODEL, n_in))],
        out_specs=[row(D_A), row(D_B), row(2 * D_B), row(D_B), row(D_B)],
        out_shape=[jax.ShapeDtypeStruct((m, D_A), F32), jax.ShapeDtypeStruct((m, D_B), q_dtype),
                   jax.ShapeDtypeStruct((m, 2 * D_B), F32), jax.ShapeDtypeStruct((m, D_B), BF16),
                   jax.ShapeDtypeStruct((m, D_B), BF16)],
        compiler_params=_params("arbitrary"),
    )(x, g, w)


def _pool_kernel(pos0, tchunk, u_ref, hist_ref, amix_ref, ascale_ref, o_ref, ext_ref):
    t_len = u_ref.shape[0]
    hpad = hist_ref.shape[0]
    ext_ref[0:hpad, :] = hist_ref[...]
    ext_ref[hpad:hpad + t_len, :] = u_ref[...]
    for t0 in range(0, t_len, tchunk):
        pos = pos0 + t0 + lax.broadcasted_iota(jnp.int32, (tchunk, 1), 0)
        for gi, w in enumerate(POOL_WINDOWS):
            sl = slice(gi * POOL_GROUP, (gi + 1) * POOL_GROUP)
            u = u_ref[t0:t0 + tchunk, sl]
            acc = u
            for j in range(1, w):
                acc = acc + ext_ref[hpad + t0 - j:hpad + t0 - j + tchunk, sl]
            cnt = jnp.minimum(pos + 1, w).astype(F32)
            p = (acc / cnt - u).astype(BF16)
            y = _dot(p, amix_ref[gi]) * ascale_ref[:, sl]
            o_ref[t0:t0 + tchunk, sl] = y.astype(o_ref.dtype)


def _pool_mix(u, hist16, amix, ascale, pos0, n_seq, t_len):
    hpad = hist16.shape[1]
    tchunk = min(t_len, 256)
    out = pl.pallas_call(
        functools.partial(_pool_kernel, pos0, tchunk),
        grid=(n_seq,),
        in_specs=[pl.BlockSpec((None, t_len, D_A), lambda s: (s, 0, 0)),
                  pl.BlockSpec((None, hpad, D_A), lambda s: (s, 0, 0)),
                  _resident(amix.shape), _resident((1, D_A))],
        out_specs=pl.BlockSpec((None, t_len, D_A), lambda s: (s, 0, 0)),
        out_shape=jax.ShapeDtypeStruct((n_seq, t_len, D_A), BF16 if t_len % 16 == 0 else F32),
        scratch_shapes=[pltpu.VMEM((hpad + t_len, D_A), F32)],
        compiler_params=_params("arbitrary"),
    )(u.reshape(n_seq, t_len, D_A), hist16, amix, ascale)
    return out.reshape(n_seq * t_len, D_A)


def _diff_finalize(acc, l, lam, sg, lam_init, t):
    o = acc[0:t] / l[0:t] - lam * (acc[t:2 * t] / l[t:2 * t])
    o = o * lax.rsqrt(jnp.mean(o * o, axis=-1, keepdims=True) + RMS_EPS) * sg
    return o * (1.0 - lam_init)


def _diffattn_prompt_kernel(lam_init, q_ref, k_ref, v_ref, blam_ref, sg_ref, o_ref, m_scr, l_scr, acc_scr):
    tq = q_ref.shape[0]
    i = pl.program_id(2)
    qq = _split_heads_stack(q_ref[...], HD_B)

    def kv_block(j):
        off = pl.multiple_of(j * tq, tq)
        return k_ref[pl.ds(off, tq), :], v_ref[pl.ds(off, tq), :]

    kd, vd = kv_block(i)
    s = _dot_nt(qq, kd)
    row = lax.broadcasted_iota(jnp.int32, s.shape, 0)
    col = lax.broadcasted_iota(jnp.int32, s.shape, 1)
    qidx = jnp.where(row >= tq, row - tq, row)
    s = jnp.where(col <= qidx, s, -jnp.inf)
    m = jnp.max(s, axis=-1, keepdims=True)
    p = jnp.exp(s - m)
    m_scr[...] = m
    l_scr[...] = jnp.sum(p, axis=-1, keepdims=True)
    acc_scr[...] = _dot(p.astype(BF16), vd)

    def body(j, carry):
        kj, vj = kv_block(j)
        s = _dot_nt(qq, kj)
        m_prev = m_scr[...]
        m_new = jnp.maximum(m_prev, jnp.max(s, axis=-1, keepdims=True))
        alpha = jnp.exp(m_prev - m_new)
        p = jnp.exp(s - m_new)
        l_scr[...] = alpha * l_scr[...] + jnp.sum(p, axis=-1, keepdims=True)
        acc_scr[...] = alpha * acc_scr[...] + _dot(p.astype(BF16), vj)
        m_scr[...] = m_new
        return carry

    lax.fori_loop(0, i, body, 0)
    lam = _diff_lambda(blam_ref[...], lam_init)
    o = _diff_finalize(acc_scr[...], l_scr[...], lam, sg_ref[...], lam_init, tq)
    o_ref[...] = o.astype(o_ref.dtype)


def _diffattn_prompt(q, k, v, blam, sg, lam_init, n_batch, seq, tq):
    nq = seq // tq
    hw = 2 * HD_B
    return pl.pallas_call(
        functools.partial(_diffattn_prompt_kernel, lam_init),
        grid=(n_batch, H_B, nq),
        in_specs=[pl.BlockSpec((tq, hw), lambda b, h, i: (b * nq + i, h)),
                  pl.BlockSpec((seq, hw), lambda b, h, i: (b, h)),
                  pl.BlockSpec((seq, hw), lambda b, h, i: (b, h)),
                  _resident(blam.shape), _resident((1, hw))],
        out_specs=pl.BlockSpec((tq, hw), lambda b, h, i: (b * nq + i, h)),
        out_shape=jax.ShapeDtypeStruct((n_batch * seq, D_B), BF16),
        scratch_shapes=[pltpu.VMEM((2 * tq, 1), F32), pltpu.VMEM((2 * tq, 1), F32),
                        pltpu.VMEM((2 * tq, hw), F32)],
        compiler_params=_params("arbitrary", "arbitrary", "arbitrary"),
    )(q, k, v, blam, sg)


def _online_update(rows, s, v, valid, m_scr, l_scr, acc_scr):
    m_prev = m_scr[rows]
    m_new = jnp.maximum(m_prev, jnp.max(s, axis=-1, keepdims=True))
    alpha = jnp.exp(m_prev - m_new)
    p = jnp.exp(s - m_new)
    if valid is not None:
        p = jnp.where(valid, p, 0.0)
    l_scr[rows] = alpha * l_scr[rows] + jnp.sum(p, axis=-1, keepdims=True)
    acc_scr[rows] = alpha * acc_scr[rows] + _dot(p.astype(BF16), v)
    m_scr[rows] = m_new


def _init_online(m_scr, l_scr, acc_scr):
    m_scr[...] = jnp.full(m_scr.shape, NEG_BIG, F32)
    l_scr[...] = jnp.zeros(l_scr.shape, F32)
    acc_scr[...] = jnp.zeros(acc_scr.shape, F32)


def _diffattn_decode_kernel(pages_per_step, lam_init, pt_ref, q_ref, kvnew_ref, blam_ref, sg_ref, *rest):
    page_refs = rest[:pages_per_step]
    o_ref, m_scr, l_scr, acc_scr, pad_scr = rest[pages_per_step:]
    j = pl.program_id(1)
    t_new = q_ref.shape[0]
    hw = 2 * HD_B
    q = q_ref[...].astype(BF16)
    qs = [_split_heads_stack(q[:, h * hw:(h + 1) * hw], HD_B) for h in range(H_B)]

    @pl.when(j == 0)
    def _():
        _init_online(m_scr, l_scr, acc_scr)

    def attend(tile_ref, valid):
        for h in range(H_B):
            kh = tile_ref[:, h * hw:(h + 1) * hw].astype(BF16)
            vh = tile_ref[:, D_B + h * hw:D_B + (h + 1) * hw].astype(BF16)
            s = _dot_nt(qs[h], kh)
            if valid is not None:
                s = jnp.where(valid, s, NEG_BIG)
            _online_update(slice(2 * t_new * h, 2 * t_new * (h + 1)), s, vh, valid, m_scr, l_scr, acc_scr)

    for r in page_refs:
        attend(r, None)

    @pl.when(j == pl.num_programs(1) - 1)
    def _():
        pad_scr[...] = jnp.zeros(pad_scr.shape, F32)
        pad_scr[0:t_new, :] = kvnew_ref[...]
        shape = (2 * t_new, pad_scr.shape[0])
        row = lax.broadcasted_iota(jnp.int32, shape, 0)
        col = lax.broadcasted_iota(jnp.int32, shape, 1)
        qidx = jnp.where(row >= t_new, row - t_new, row)
        attend(pad_scr, col <= qidx)
        lam = _diff_lambda(blam_ref[...], lam_init)
        for h in range(H_B):
            rows = slice(2 * t_new * h, 2 * t_new * (h + 1))
            o = _diff_finalize(acc_scr[rows], l_scr[rows], lam, sg_ref[...], lam_init, t_new)
            o_ref[:, h * hw:(h + 1) * hw] = o.astype(o_ref.dtype)


def _diffattn_decode(page_table, cache, q, kvnew, blam, sg, lam_init, n_batch, t_new, pages_per_step):
    n_pages = page_table.shape[1]
    hw = 2 * HD_B
    page_specs = [
        pl.BlockSpec((None, PAGE_SIZE, 2 * D_B), functools.partial(
            lambda b, j, pt, t: (pt[b, j * pages_per_step + t], 0, 0), t=t))
        for t in range(pages_per_step)]
    grid_spec = pltpu.PrefetchScalarGridSpec(
        num_scalar_prefetch=1,
        grid=(n_batch, n_pages // pages_per_step),
        in_specs=[pl.BlockSpec((None, t_new, D_B), lambda b, j, pt: (b, 0, 0)),
                  pl.BlockSpec((None, t_new, 2 * D_B), lambda b, j, pt: (b, 0, 0)),
                  pl.BlockSpec(blam.shape, lambda b, j, pt: (0, 0)),
                  pl.BlockSpec((1, hw), lambda b, j, pt: (0, 0))] + page_specs,
        out_specs=pl.BlockSpec((None, t_new, D_B), lambda b, j, pt: (b, 0, 0)),
        scratch_shapes=[pltpu.VMEM((2 * t_new * H_B, 1), F32), pltpu.VMEM((2 * t_new * H_B, 1), F32),
                        pltpu.VMEM((2 * t_new * H_B, hw), F32), pltpu.VMEM((PAGE_SIZE, 2 * D_B), F32)])
    out = pl.pallas_call(
        functools.partial(_diffattn_decode_kernel, pages_per_step, lam_init),
        grid_spec=grid_spec,
        out_shape=jax.ShapeDtypeStruct((n_batch, t_new, D_B), F32),
        compiler_params=_params("arbitrary", "arbitrary"),
    )(page_table, q.reshape(n_batch, t_new, D_B), kvnew.reshape(n_batch, t_new, 2 * D_B), blam, sg,
      *([cache] * pages_per_step))
    return out.reshape(n_batch * t_new, D_B)


def _post_kernel(n_parts, n_lse, ff_chunk, has_final, x_ref, first_ref, *rest):
    o_refs = rest[:n_parts]
    lse_refs = rest[n_parts:n_parts + n_lse]
    rest = rest[n_parts + n_lse:]
    if n_lse == 0:
        second = o_refs[0][...].astype(BF16)
    else:
        lses = [r[...] for r in lse_refs]
        mx = functools.reduce(jnp.maximum, lses)
        es = [jnp.exp(l - mx) for l in lses]
        num = functools.reduce(lambda a, b: a + b, [e * r[...].astype(F32) for e, r in zip(es, o_refs)])
        second = (num / functools.reduce(lambda a, b: a + b, es)).astype(BF16)
    wout_ref, g_ref, wup_ref, wdn_ref = rest[:4]
    rest = rest[4:]
    gf_ref = rest[0] if has_final else None
    o_ref = rest[-1]
    mix = jnp.concatenate([first_ref[...].astype(BF16), second], axis=-1)
    x1 = x_ref[...] + _dot(mix, wout_ref[...])
    h = _rmsnorm(x1, g_ref[...]).astype(BF16)
    acc = x1
    for c in range(0, D_FF, ff_chunk):
        up = _dot(h, wup_ref[:, c:c + ff_chunk])
        act = jnp.square(jnp.maximum(up, 0.0)).astype(BF16)
        acc = acc + _dot(act, wdn_ref[c:c + ff_chunk, :])
    if has_final:
        acc = _rmsnorm(acc, gf_ref[...])
    o_ref[...] = acc


def _post_mixer(x, first, o_parts, lse_parts, wout, g, wup, wdn, gf, tm):
    m = x.shape[0]
    row = lambda n: pl.BlockSpec((tm, n), lambda i: (i, 0))
    has_final = gf is not None
    args = [x, first, *o_parts, *lse_parts, wout, g, wup, wdn]
    specs = [row(D_MODEL), row(first.shape[1])] + [row(a.shape[1]) for a in (*o_parts, *lse_parts)]
    specs += [_resident(wout.shape), _resident((1, D_MODEL)), _resident(wup.shape), _resident(wdn.shape)]
    if has_final:
        args.append(gf)
        specs.append(_resident((1, D_MODEL)))
    return pl.pallas_call(
        functools.partial(_post_kernel, len(o_parts), len(lse_parts), 1024, has_final),
        grid=(m // tm,),
        in_specs=specs,
        out_specs=row(D_MODEL),
        out_shape=jax.ShapeDtypeStruct((m, D_MODEL), F32),
        compiler_params=_params("arbitrary"),
    )(*args)


def _proj_cd_kernel(tiles_per_seq, tails, x_ref, g_ref, w_ref, cw_ref, hist_ref, *rest):
    n_g = len(D_WINDOWS)
    c_ref, ztail_ref = rest[0], rest[1]
    group_refs = rest[2:2 + 4 * n_g] if tiles_per_seq else rest[2:2 + 2 * n_g]
    carry_ref = rest[-1]
    tm = x_ref.shape[0]
    i = pl.program_id(0)
    h = _rmsnorm(x_ref[...], g_ref[...]).astype(BF16)
    gates = _dot(h, w_ref[:, 0:3 * D_C])
    b_gate = gates[:, 0:D_C]
    z = gates[:, D_C:2 * D_C] * gates[:, 2 * D_C:3 * D_C]
    row = lax.broadcasted_iota(jnp.int32, z.shape, 0)
    if tiles_per_seq:
        first = (i % tiles_per_seq) == 0
        prev = jnp.where(first, hist_ref[...], carry_ref[...])
        z1 = jnp.where(row == 0, prev[7:8], pltpu.roll(z, 1, 0))
        z2 = jnp.where(row == 0, prev[6:7], jnp.where(row == 1, prev[7:8], pltpu.roll(z, 2, 0)))
        carry_ref[...] = z[tm - SUBLANES:tm]
        ztail_ref[...] = z[tm - SUBLANES:tm]
    else:
        e = hist_ref[...]
        t = row & (SUBLANES - 1)
        z1 = jnp.where(t == 0, pltpu.roll(e, tm - 7, 0), pltpu.roll(z, 1, 0))
        z2 = jnp.where(t < 2, pltpu.roll(e, tm - 6, 0), pltpu.roll(z, 2, 0))
        ztail_ref[...] = z
    cw = cw_ref[...]
    cv = z2 * cw[0:1] + z1 * cw[1:2] + z * cw[2:3]
    c_ref[...] = (b_gate * cv).astype(c_ref.dtype)
    for gi in range(n_g):
        off = 3 * D_C + 3 * gi * D_DG
        res = _dot(h, w_ref[:, off:off + 3 * D_DG])
        q = res[:, 0:D_DG] * QK_SCALE_D
        kv = res[:, D_DG:3 * D_DG]
        if tiles_per_seq:
            q_ref, k_ref, v_ref, kvt_ref = group_refs[4 * gi:4 * gi + 4]
            q_ref[...] = q.astype(BF16)
            k_ref[...] = kv[:, 0:D_DG].astype(BF16)
            v_ref[...] = kv[:, D_DG:].astype(BF16)
            kvt_ref[...] = kv[tm - tails[gi]:tm]
        else:
            q_ref, kv_ref = group_refs[2 * gi:2 * gi + 2]
            q_ref[...] = q
            kv_ref[...] = kv


def _proj_cd_prompt(x, g, w, cw, n_batch, seq, tm):
    m = x.shape[0]
    tps = seq // tm
    row = lambda n, dt=None: pl.BlockSpec((tm, n), lambda i: (i, 0))
    hist = jnp.zeros((n_batch * SUBLANES, D_C), F32)
    out_shape = [jax.ShapeDtypeStruct((m, D_C), BF16), jax.ShapeDtypeStruct((m // tm * SUBLANES, D_C), F32)]
    out_specs = [row(D_C), pl.BlockSpec((SUBLANES, D_C), lambda i: (i, 0))]
    tails = []
    for win, _ in D_WINDOWS:
        keep = min(win, seq)
        tail = min(keep, tm)
        tails.append(tail)
        first_kept = tps - keep // tail
        out_shape += [jax.ShapeDtypeStruct((m, D_DG), BF16)] * 3
        out_shape += [jax.ShapeDtypeStruct((n_batch, keep, 2 * D_DG), F32)]
        out_specs += [row(D_DG)] * 3
        out_specs += [pl.BlockSpec((None, tail, 2 * D_DG), functools.partial(
            lambda i, fk: (i // tps, jnp.maximum(i % tps - fk, 0), 0), fk=first_kept))]
    return pl.pallas_call(
        functools.partial(_proj_cd_kernel, tps, tuple(tails)),
        grid=(m // tm,),
        in_specs=[row(D_MODEL), _resident((1, D_MODEL)), _resident(w.shape), _resident(cw.shape),
                  pl.BlockSpec((SUBLANES, D_C), lambda i: (i // tps, 0))],
        out_specs=out_specs,
        out_shape=out_shape,
        scratch_shapes=[pltpu.VMEM((SUBLANES, D_C), F32)],
        compiler_params=_params("arbitrary"),
    )(x, g, w, cw, hist)


def _proj_cd_sample(x, g, w, cw, hist_rows):
    m = x.shape[0]
    full = lambda n: pl.BlockSpec((m, n), lambda i: (0, 0))
    out_shape = [jax.ShapeDtypeStruct((m, D_C), F32), jax.ShapeDtypeStruct((m, D_C), F32)]
    out_specs = [full(D_C), full(D_C)]
    for _ in D_WINDOWS:
        out_shape += [jax.ShapeDtypeStruct((m, D_DG), F32), jax.ShapeDtypeStruct((m, 2 * D_DG), F32)]
        out_specs += [full(D_DG), full(2 * D_DG)]
    return pl.pallas_call(
        functools.partial(_proj_cd_kernel, 0, ()),
        grid=(1,),
        in_specs=[full(D_MODEL), _resident((1, D_MODEL)), _resident(w.shape), _resident(cw.shape), full(D_C)],
        out_specs=out_specs,
        out_shape=out_shape,
        scratch_shapes=[pltpu.VMEM((SUBLANES, D_C), F32)],
        compiler_params=_params("arbitrary"),
    )(x, g, w, cw, hist_rows)


def _dilated_prompt_kernel(q_ref, kc_ref, kp_ref, vc_ref, vp_ref, o_ref, lse_ref):
    jb = pl.program_id(2)
    bl = q_ref.shape[0]
    shape = (bl, 2 * bl)
    row = lax.broadcasted_iota(jnp.int32, shape, 0)
    col = lax.broadcasted_iota(jnp.int32, shape, 1)
    dist = row - col + bl
    valid = (dist >= 0) & (dist <= N_BACK) & (jb * bl + col - bl >= 0)
    hw = 2 * HD_D
    lane = lax.broadcasted_iota(jnp.int32, (bl, hw), 1)
    for p in range(H_DG // 2):
        sl = slice(p * hw, (p + 1) * hw)
        qq = _split_heads_stack(q_ref[:, sl], HD_D)
        kcat = jnp.concatenate([kp_ref[:, sl], kc_ref[:, sl]], axis=0)
        vcat = jnp.concatenate([vp_ref[:, sl], vc_ref[:, sl]], axis=0)
        outs, lses = [], []
        for e in range(2):
            s = _dot_nt(qq[e * bl:(e + 1) * bl], kcat)
            s = jnp.where(valid, s, -jnp.inf)
            m = jnp.max(s, axis=-1, keepdims=True)
            pr = jnp.exp(s - m)
            l = jnp.sum(pr, axis=-1, keepdims=True)
            outs.append(_dot(pr.astype(BF16), vcat) / l)
            lses.append(m + jnp.log(l))
        o_ref[:, sl] = jnp.where(lane < HD_D, outs[0], outs[1]).astype(o_ref.dtype)
        lse_ref[:, sl] = jnp.where(lane < HD_D, lses[0], lses[1])


def _dilated_prompt(q, k, v, dil, n_batch, seq):
    length = seq // dil
    bl = N_BACK
    nblk = length // bl
    view = lambda a: a.reshape(n_batch, length, dil * D_DG)
    cur = pl.BlockSpec((None, bl, D_DG), lambda b, r, j: (b, j, r))
    prev = pl.BlockSpec((None, bl, D_DG), lambda b, r, j: (b, jnp.maximum(j - 1, 0), r))
    o, lse = pl.pallas_call(
        _dilated_prompt_kernel,
        grid=(n_batch, dil, nblk),
        in_specs=[cur, cur, prev, cur, prev],
        out_specs=[cur, cur],
        out_shape=[jax.ShapeDtypeStruct((n_batch, length, dil * D_DG), BF16),
                   jax.ShapeDtypeStruct((n_batch, length, dil * D_DG), F32)],
        compiler_params=_params("arbitrary", "arbitrary", "arbitrary"),
    )(view(q), view(k), view(k), view(v), view(v))
    return o.reshape(n_batch * seq, D_DG), lse.reshape(n_batch * seq, D_DG)


def _dilated_sample_kernel(dil, buf_len, q_ref, cache_ref, next_ref, kvnew_ref, newc_ref, o_ref, lse_ref,
                           m_scr, l_scr, acc_scr, pad_scr):
    j = pl.program_id(1)
    last = pl.num_programs(1) - 1
    tk = cache_ref.shape[0]
    t_new = q_ref.shape[0]
    hw = 2 * HD_D
    n_pairs = H_DG // 2

    newc_ref[0:tk - t_new, :] = cache_ref[t_new:tk, :]
    newc_ref[tk - t_new:tk, :] = jnp.where(j == last, kvnew_ref[...], next_ref[...])

    @pl.when(j == 0)
    def _():
        _init_online(m_scr, l_scr, acc_scr)

    q = q_ref[...].astype(BF16)
    qs = [_split_heads_stack(q[:, p * hw:(p + 1) * hw], HD_D) for p in range(n_pairs)]

    def attend(tile_ref, valid):
        for p in range(n_pairs):
            kp = tile_ref[:, p * hw:(p + 1) * hw].astype(BF16)
            vp = tile_ref[:, D_DG + p * hw:D_DG + (p + 1) * hw].astype(BF16)
            s = jnp.where(valid, _dot_nt(qs[p], kp), NEG_BIG)
            _online_update(slice(2 * t_new * p, 2 * t_new * (p + 1)), s, vp, valid, m_scr, l_scr, acc_scr)

    def iotas(n_keys):
        shape = (2 * t_new, n_keys)
        row = lax.broadcasted_iota(jnp.int32, shape, 0)
        return row & (t_new - 1), lax.broadcasted_iota(jnp.int32, shape, 1)

    qi, col = iotas(tk)
    d = buf_len + qi - (j * tk + col)
    attend(cache_ref, ((d & (dil - 1)) == 0) & (d <= N_BACK * dil))

    @pl.when(j == last)
    def _():
        pad_scr[...] = jnp.zeros(pad_scr.shape, F32)
        pad_scr[0:t_new, :] = kvnew_ref[...]
        qi, col = iotas(pad_scr.shape[0])
        d = qi - col
        attend(pad_scr, (col < t_new) & (d >= 0) & ((d & (dil - 1)) == 0))
        lane = lax.broadcasted_iota(jnp.int32, (t_new, hw), 1)
        for p in range(n_pairs):
            rows = slice(2 * t_new * p, 2 * t_new * (p + 1))
            acc, l, m = acc_scr[rows], l_scr[rows], m_scr[rows]
            o = acc / l
            lse = m + jnp.log(l)
            o_ref[:, p * hw:(p + 1) * hw] = jnp.where(lane < HD_D, o[0:t_new], o[t_new:2 * t_new])
            lse_ref[:, p * hw:(p + 1) * hw] = jnp.where(lane < HD_D, lse[0:t_new], lse[t_new:2 * t_new])


def _dilated_sample(q, cache, kvnew, dil, n_batch, t_new, tk):
    buf_len = cache.shape[1]
    nkt = buf_len // tk
    per_tile = tk // t_new
    small = lambda n: pl.BlockSpec((None, t_new, n), lambda b, j: (b, 0, 0))
    newc, o, lse = pl.pallas_call(
        functools.partial(_dilated_sample_kernel, dil, buf_len),
        grid=(n_batch, nkt),
        in_specs=[small(D_DG),
                  pl.BlockSpec((None, tk, 2 * D_DG), lambda b, j: (b, j, 0)),
                  pl.BlockSpec((None, t_new, 2 * D_DG),
                               lambda b, j: (b, jnp.minimum((j + 1) * per_tile, nkt * per_tile - 1), 0)),
                  small(2 * D_DG)],
        out_specs=[pl.BlockSpec((None, tk, 2 * D_DG), lambda b, j: (b, j, 0)), small(D_DG), small(D_DG)],
        out_shape=[jax.ShapeDtypeStruct(cache.shape, F32),
                   jax.ShapeDtypeStruct((n_batch, t_new, D_DG), F32),
                   jax.ShapeDtypeStruct((n_batch, t_new, D_DG), F32)],
        scratch_shapes=[pltpu.VMEM((2 * t_new * (H_DG // 2), 1), F32), pltpu.VMEM((2 * t_new * (H_DG // 2), 1), F32),
                        pltpu.VMEM((2 * t_new * (H_DG // 2), 2 * HD_D), F32),
                        pltpu.VMEM((LANES, 2 * D_DG), F32)],
        compiler_params=_params("arbitrary", "arbitrary"),
    )(q.reshape(n_batch, t_new, D_DG), cache, cache, kvnew.reshape(n_batch, t_new, 2 * D_DG))
    return newc, o.reshape(n_batch * t_new, D_DG), lse.reshape(n_batch * t_new, D_DG)


def kernel(x_prompt, x_sample, state_a_pool, cache_b_kv, state_c_conv, cache_d0_kv, cache_d1_kv, cache_d2_kv, page_table, norm_mix_g, norm_mlp_g, norm_out_g, w_in_ab, w_out_ab, a_mix, a_scale, b_lam, b_subln_g, w_in_cd, w_out_cd, c_conv_w, w_up, w_down):
    bp, seq, _ = x_prompt.shape
    bs, t_new, _ = x_sample.shape
    n_pages = page_table.shape[1]
    past_len = n_pages * cache_b_kv.shape[2]
    mp, ms = bp * seq, bs * t_new
    tm_p = 512
    xp = x_prompt.reshape(mp, D_MODEL)
    xs = x_sample.reshape(ms, D_MODEL)
    g2 = lambda v: v.reshape(1, -1)
    d_caches = (cache_d0_kv, cache_d1_kv, cache_d2_kv)

    lam_init = 0.8 - 0.6 * math.exp(-0.3 * 0)
    w_in = w_in_ab[0].astype(BF16)
    w_out = w_out_ab[0].astype(BF16)
    wu, wd = w_up[0].astype(BF16), w_down[0].astype(BF16)
    amix = a_mix[0].astype(BF16)
    gm, gl = g2(norm_mix_g[0]), g2(norm_mlp_g[0])
    ascale, sg = g2(a_scale[0]), g2(b_subln_g[0])

    u_p, q_p, kv_p, k_p, v_p = _proj_ab(xp, gm, w_in, tm_p, BF16)
    hist_p = jnp.zeros((bp, POOL_HIST + 1, D_A), F32)
    a_p = _pool_mix(u_p, hist_p, amix, ascale, 0, bp, seq)
    ob_p = _diffattn_prompt(q_p, k_p, v_p, b_lam[0], sg, lam_init, bp, seq, 256)
    xp = _post_mixer(xp, a_p, (ob_p,), (), w_out, gl, wu, wd, None, tm_p)

    u_s, q_s, kv_s, _, _ = _proj_ab(xs, gm, w_in, ms, F32)
    hist_s = jnp.pad(state_a_pool[0], ((0, 0), (1, 0), (0, 0)))
    a_s = _pool_mix(u_s, hist_s, amix, ascale, past_len, bs, t_new)
    pool_pages = cache_b_kv[0].reshape(cache_b_kv.shape[1], PAGE_SIZE, 2 * D_B)
    ob_s = _diffattn_decode(page_table, pool_pages, q_s, kv_s, b_lam[0], sg, lam_init, bs, t_new, 8)
    xs = _post_mixer(xs, a_s, (ob_s,), (), w_out, gl, wu, wd, None, ms)

    u_p3 = u_p.reshape(bp, seq, D_A)
    new_pool_p = u_p3[:, seq - POOL_HIST:][None]
    new_pool_s = jnp.concatenate([state_a_pool[0], u_s.reshape(bs, t_new, D_A)], axis=1)[:, -POOL_HIST:][None]
    new_bkv_p = kv_p.reshape(1, bp, seq, 2, H_B, 2 * HD_B)
    new_bkv_s = kv_s.reshape(1, bs, t_new, 2, H_B, 2 * HD_B)

    w_in = w_in_cd[0].astype(BF16)
    w_out = w_out_cd[0].astype(BF16)
    wu, wd = w_up[1].astype(BF16), w_down[1].astype(BF16)
    gm, gl, gf = g2(norm_mix_g[1]), g2(norm_mlp_g[1]), g2(norm_out_g)
    cw = c_conv_w[0]

    outs = _proj_cd_prompt(xp, gm, w_in, cw, bp, seq, tm_p)
    c_p, ztail_p = outs[0], outs[1]
    o_parts, lse_parts, new_d_p = [], [], []
    for gi, (win, dil) in enumerate(D_WINDOWS):
        q_g, k_g, v_g, kvt_g = outs[2 + 4 * gi:6 + 4 * gi]
        o_g, lse_g = _dilated_prompt(q_g, k_g, v_g, dil, bp, seq)
        o_parts.append(o_g)
        lse_parts.append(lse_g)
        new_d_p.append(kvt_g.reshape(1, bp, min(win, seq), 2, H_DG, HD_D))
    y_p = _post_mixer(xp, c_p, o_parts, lse_parts, w_out, gl, wu, wd, gf, tm_p)
    tiles_per_seq = seq // tm_p
    new_conv_p = ztail_p.reshape(bp, tiles_per_seq, SUBLANES, D_C)[:, -1, SUBLANES - (CONV_W - 1):][None]

    hist_rows = jnp.pad(state_c_conv[0], ((0, 0), (t_new - (CONV_W - 1), 0), (0, 0))).reshape(ms, D_C)
    outs = _proj_cd_sample(xs, gm, w_in, cw, hist_rows)
    c_s, z_s = outs[0], outs[1]
    o_parts, lse_parts, new_d_s = [], [], []
    for gi, (win, dil) in enumerate(D_WINDOWS):
        q_g, kv_g = outs[2 + 2 * gi:4 + 2 * gi]
        cache = d_caches[gi][0]
        buf_len = cache.shape[1]
        newc, o_g, lse_g = _dilated_sample(q_g, cache.reshape(bs, buf_len, 2 * D_DG), kv_g, dil, bs, t_new,
                                           min(buf_len, 512))
        o_parts.append(o_g)
        lse_parts.append(lse_g)
        new_d_s.append(newc.reshape(1, bs, buf_len, 2, H_DG, HD_D))
    y_s = _post_mixer(xs, c_s, o_parts, lse_parts, w_out, gl, wu, wd, gf, ms)
    new_conv_s = z_s.reshape(bs, t_new, D_C)[:, t_new - (CONV_W - 1):][None]

    return (y_p.reshape(bp, seq, D_MODEL), y_s.reshape(bs, t_new, D_MODEL),
            new_pool_p, new_pool_s, new_bkv_p, new_bkv_s, new_conv_p, new_conv_s,
            new_d_p[0], new_d_s[0], new_d_p[1], new_d_s[1], new_d_p[2], new_d_s[2])
```

```python
import functools
import math

import jax
import jax.numpy as jnp
from jax import lax
from jax.experimental import pallas as pl
from jax.experimental.pallas import tpu as pltpu

F32 = jnp.float32
BF16 = jnp.bfloat16

D_MODEL = 1024
RMS_EPS = 1e-6
D_A = 512
POOL_WINDOWS = (2, 4, 8, 16)
POOL_GROUP = 128
POOL_HIST = 15
H_B = 4
HD_B = 64
D_B = 512
D_C = 512
CONV_W = 3
D_WINDOWS = ((128, 1), (512, 4), (2048, 16))
N_BACK = 128
H_DG = 8
HD_D = 64
D_DG = 512
D_FF = 4096
PAGE_SIZE = 128

LANES = 128
SUBLANES = 8
VMEM_LIMIT_BYTES = 56 * 1024 * 1024
NEG_BIG = -1e30

QK_SCALE_B = HD_B ** -0.5
QK_SCALE_D = HD_D ** -0.5


def _params(*sem):
    return pltpu.CompilerParams(dimension_semantics=sem, vmem_limit_bytes=VMEM_LIMIT_BYTES)


def _resident(shape):
    nd = len(shape)
    return pl.BlockSpec(shape, lambda *_: (0,) * nd, pipeline_mode=pl.Buffered(1))


def _rmsnorm(x, g):
    return x * lax.rsqrt(jnp.mean(x * x, axis=-1, keepdims=True) + RMS_EPS) * g


def _dot(a, b):
    return jnp.dot(a, b, preferred_element_type=F32)


def _dot_nt(a, b):
    return lax.dot_general(a, b, (((1,), (1,)), ((), ())), preferred_element_type=F32)


def _split_heads_stack(q, half):
    lane = lax.broadcasted_iota(jnp.int32, q.shape, 1)
    zero = jnp.zeros_like(q)
    return jnp.concatenate([jnp.where(lane < half, q, zero), jnp.where(lane >= half, q, zero)], axis=0)


def _diff_lambda(blam, lam_init):
    a = jnp.sum(blam[0:1] * blam[1:2], axis=-1, keepdims=True)
    b = jnp.sum(blam[2:3] * blam[3:4], axis=-1, keepdims=True)
    return jnp.exp(a) - jnp.exp(b) + lam_init


def _proj_ab_kernel(x_ref, g_ref, w_ref, u_ref, q_ref, kv_ref, k_ref, v_ref):
    h = _rmsnorm(x_ref[...], g_ref[...]).astype(BF16)
    u_ref[...] = _dot(h, w_ref[:, 0:D_A])
    q_ref[...] = (_dot(h, w_ref[:, D_A:D_A + D_B]) * QK_SCALE_B).astype(q_ref.dtype)
    kv = _dot(h, w_ref[:, D_A + D_B:D_A + 3 * D_B])
    kv_ref[...] = kv
    k_ref[...] = kv[:, :D_B].astype(BF16)
    v_ref[...] = kv[:, D_B:].astype(BF16)


def _proj_ab(x, g, w, tm, q_dtype):
    m = x.shape[0]
    n_in = w.shape[1]
    row = lambda n: pl.BlockSpec((tm, n), lambda i: (i, 0))
    return pl.pallas_call(
        _proj_ab_kernel,
        grid=(m // tm,),
        in_specs=[row(D_MODEL), _resident((1, D_MODEL)), _resident((D_MODEL, n_in))],
        out_specs=[row(D_A), row(D_B), row(2 * D_B), row(D_B), row(D_B)],
        out_shape=[jax.ShapeDtypeStruct((m, D_A), F32), jax.ShapeDtypeStruct((m, D_B), q_dtype),
                   jax.ShapeDtypeStruct((m, 2 * D_B), F32), jax.ShapeDtypeStruct((m, D_B), BF16),
                   jax.ShapeDtypeStruct((m, D_B), BF16)],
        compiler_params=_params("arbitrary"),
    )(x, g, w)


def _pool_kernel(pos0, tchunk, u_ref, hist_ref, amix_ref, ascale_ref, o_ref, ext_ref):
    t_len = u_ref.shape[0]
    hpad = hist_ref.shape[0]
    ext_ref[0:hpad, :] = hist_ref[...]
    ext_ref[hpad:hpad + t_len, :] = u_ref[...]
    for t0 in range(0, t_len, tchunk):
        pos = pos0 + t0 + lax.broadcasted_iota(jnp.int32, (tchunk, 1), 0)
        for gi, w in enumerate(POOL_WINDOWS):
            sl = slice(gi * POOL_GROUP, (gi + 1) * POOL_GROUP)
            u = u_ref[t0:t0 + tchunk, sl]
            acc = u
            for j in range(1, w):
                acc = acc + ext_ref[hpad + t0 - j:hpad + t0 - j + tchunk, sl]
            cnt = jnp.minimum(pos + 1, w).astype(F32)
            p = (acc / cnt - u).astype(BF16)
            y = _dot(p, amix_ref[gi]) * ascale_ref[:, sl]
            o_ref[t0:t0 + tchunk, sl] = y.astype(o_ref.dtype)


def _pool_mix(u, hist16, amix, ascale, pos0, n_seq, t_len):
    hpad = hist16.shape[1]
    tchunk = min(t_len, 256)
    out = pl.pallas_call(
        functools.partial(_pool_kernel, pos0, tchunk),
        grid=(n_seq,),
        in_specs=[pl.BlockSpec((None, t_len, D_A), lambda s: (s, 0, 0)),
                  pl.BlockSpec((None, hpad, D_A), lambda s: (s, 0, 0)),
                  _resident(amix.shape), _resident((1, D_A))],
        out_specs=pl.BlockSpec((None, t_len, D_A), lambda s: (s, 0, 0)),
        out_shape=jax.ShapeDtypeStruct((n_seq, t_len, D_A), BF16 if t_len % 16 == 0 else F32),
        scratch_shapes=[pltpu.VMEM((hpad + t_len, D_A), F32)],
        compiler_params=_params("arbitrary"),
    )(u.reshape(n_seq, t_len, D_A), hist16, amix, ascale)
    return out.reshape(n_seq * t_len, D_A)


def _diff_finalize(acc, l, lam, sg, lam_init, t):
    o = acc[0:t] / l[0:t] - lam * (acc[t:2 * t] / l[t:2 * t])
    o = o * lax.rsqrt(jnp.mean(o * o, axis=-1, keepdims=True) + RMS_EPS) * sg
    return o * (1.0 - lam_init)


def _diffattn_prompt_kernel(lam_init, q_ref, k_ref, v_ref, blam_ref, sg_ref, o_ref, m_scr, l_scr, acc_scr):
    tq = q_ref.shape[0]
    hw = 2 * HD_B
    i = pl.program_id(1)
    heads = [slice(h * hw, (h + 1) * hw) for h in range(H_B)]
    qqs = [_split_heads_stack(q_ref[:, sl], HD_B) for sl in heads]

    def kv_block(j, sl):
        off = pl.multiple_of(j * tq, tq)
        return k_ref[pl.ds(off, tq), sl], v_ref[pl.ds(off, tq), sl]

    shape = (2 * tq, tq)
    row = lax.broadcasted_iota(jnp.int32, shape, 0)
    col = lax.broadcasted_iota(jnp.int32, shape, 1)
    causal = col <= jnp.where(row >= tq, row - tq, row)
    for h, sl in enumerate(heads):
        kd, vd = kv_block(i, sl)
        s = jnp.where(causal, _dot_nt(qqs[h], kd), -jnp.inf)
        m = jnp.max(s, axis=-1, keepdims=True)
        p = jnp.exp(s - m)
        m_scr[h] = m
        l_scr[h] = jnp.sum(p, axis=-1, keepdims=True)
        acc_scr[h] = _dot(p.astype(BF16), vd)

    def body(j, carry):
        for h, sl in enumerate(heads):
            kj, vj = kv_block(j, sl)
            s = _dot_nt(qqs[h], kj)
            m_prev = m_scr[h]
            m_new = jnp.maximum(m_prev, jnp.max(s, axis=-1, keepdims=True))
            alpha = jnp.exp(m_prev - m_new)
            p = jnp.exp(s - m_new)
            l_scr[h] = alpha * l_scr[h] + jnp.sum(p, axis=-1, keepdims=True)
            acc_scr[h] = alpha * acc_scr[h] + _dot(p.astype(BF16), vj)
            m_scr[h] = m_new
        return carry

    lax.fori_loop(0, i, body, 0)
    lam = _diff_lambda(blam_ref[...], lam_init)
    for h, sl in enumerate(heads):
        o = _diff_finalize(acc_scr[h], l_scr[h], lam, sg_ref[...], lam_init, tq)
        o_ref[:, sl] = o.astype(o_ref.dtype)


def _diffattn_prompt(q, k, v, blam, sg, lam_init, n_batch, seq, tq):
    nq = seq // tq
    hw = 2 * HD_B
    return pl.pallas_call(
        functools.partial(_diffattn_prompt_kernel, lam_init),
        grid=(n_batch, nq),
        in_specs=[pl.BlockSpec((tq, D_B), lambda b, i: (b * nq + i, 0)),
                  pl.BlockSpec((seq, D_B), lambda b, i: (b, 0)),
                  pl.BlockSpec((seq, D_B), lambda b, i: (b, 0)),
                  _resident(blam.shape), _resident((1, hw))],
        out_specs=pl.BlockSpec((tq, D_B), lambda b, i: (b * nq + i, 0)),
        out_shape=jax.ShapeDtypeStruct((n_batch * seq, D_B), BF16),
        scratch_shapes=[pltpu.VMEM((H_B, 2 * tq, 1), F32), pltpu.VMEM((H_B, 2 * tq, 1), F32),
                        pltpu.VMEM((H_B, 2 * tq, hw), F32)],
        compiler_params=_params("arbitrary", "arbitrary"),
    )(q, k, v, blam, sg)


def _online_update(rows, s, v, valid, m_scr, l_scr, acc_scr, v_transposed=False):
    m_prev = m_scr[rows]
    m_new = jnp.maximum(m_prev, jnp.max(s, axis=-1, keepdims=True))
    alpha = jnp.exp(m_prev - m_new)
    p = jnp.exp(s - m_new)
    if valid is not None:
        p = jnp.where(valid, p, 0.0)
    pv = _dot_nt(p.astype(BF16), v) if v_transposed else _dot(p.astype(BF16), v)
    l_scr[rows] = alpha * l_scr[rows] + jnp.sum(p, axis=-1, keepdims=True)
    acc_scr[rows] = alpha * acc_scr[rows] + pv
    m_scr[rows] = m_new


def _init_online(m_scr, l_scr, acc_scr):
    m_scr[...] = jnp.full(m_scr.shape, NEG_BIG, F32)
    l_scr[...] = jnp.zeros(l_scr.shape, F32)
    acc_scr[...] = jnp.zeros(acc_scr.shape, F32)


def _diffattn_decode_kernel(pages_per_step, lam_init, pt_ref, q_ref, kvnew_ref, blam_ref, sg_ref, *rest):
    page_refs = rest[:pages_per_step]
    o_ref, m_scr, l_scr, acc_scr, pad_scr = rest[pages_per_step:]
    j = pl.program_id(1)
    t_new = q_ref.shape[0]
    hw = 2 * HD_B
    q = q_ref[...].astype(BF16)
    qs = [_split_heads_stack(q[:, h * hw:(h + 1) * hw], HD_B) for h in range(H_B)]

    @pl.when(j == 0)
    def _():
        _init_online(m_scr, l_scr, acc_scr)

    rows_per_pos = 2 * H_B
    for h in range(H_B):
        kh = jnp.concatenate([r[pl.ds(h, PAGE_SIZE, stride=rows_per_pos), :] for r in page_refs], axis=0)
        vh = jnp.concatenate([r[pl.ds(H_B + h, PAGE_SIZE, stride=rows_per_pos), :] for r in page_refs], axis=0)
        s = _dot_nt(qs[h], kh.astype(BF16))
        _online_update(slice(2 * t_new * h, 2 * t_new * (h + 1)), s, vh.astype(BF16), None, m_scr, l_scr, acc_scr)

    @pl.when(j == pl.num_programs(1) - 1)
    def _():
        pad_scr[...] = jnp.zeros(pad_scr.shape, F32)
        pad_scr[0:t_new, :] = kvnew_ref[...]
        shape = (2 * t_new, pad_scr.shape[0])
        row = lax.broadcasted_iota(jnp.int32, shape, 0)
        col = lax.broadcasted_iota(jnp.int32, shape, 1)
        valid = col <= jnp.where(row >= t_new, row - t_new, row)
        for h in range(H_B):
            kh = pad_scr[:, h * hw:(h + 1) * hw].astype(BF16)
            vh = pad_scr[:, D_B + h * hw:D_B + (h + 1) * hw].astype(BF16)
            s = jnp.where(valid, _dot_nt(qs[h], kh), NEG_BIG)
            _online_update(slice(2 * t_new * h, 2 * t_new * (h + 1)), s, vh, valid, m_scr, l_scr, acc_scr)
        lam = _diff_lambda(blam_ref[...], lam_init)
        for h in range(H_B):
            rows = slice(2 * t_new * h, 2 * t_new * (h + 1))
            o = _diff_finalize(acc_scr[rows], l_scr[rows], lam, sg_ref[...], lam_init, t_new)
            o_ref[:, h * hw:(h + 1) * hw] = o.astype(o_ref.dtype)


def _diffattn_decode(page_table, cache, q, kvnew, blam, sg, lam_init, n_batch, t_new, pages_per_step):
    n_pages = page_table.shape[1]
    hw = 2 * HD_B
    page_specs = [
        pl.BlockSpec((None, PAGE_SIZE * 2 * H_B, hw), functools.partial(
            lambda b, j, pt, t: (pt[b, j * pages_per_step + t], 0, 0), t=t))
        for t in range(pages_per_step)]
    grid_spec = pltpu.PrefetchScalarGridSpec(
        num_scalar_prefetch=1,
        grid=(n_batch, n_pages // pages_per_step),
        in_specs=[pl.BlockSpec((None, t_new, D_B), lambda b, j, pt: (b, 0, 0)),
                  pl.BlockSpec((None, t_new, 2 * D_B), lambda b, j, pt: (b, 0, 0)),
                  pl.BlockSpec(blam.shape, lambda b, j, pt: (0, 0)),
                  pl.BlockSpec((1, hw), lambda b, j, pt: (0, 0))] + page_specs,
        out_specs=pl.BlockSpec((None, t_new, D_B), lambda b, j, pt: (b, 0, 0)),
        scratch_shapes=[pltpu.VMEM((2 * t_new * H_B, 1), F32), pltpu.VMEM((2 * t_new * H_B, 1), F32),
                        pltpu.VMEM((2 * t_new * H_B, hw), F32), pltpu.VMEM((PAGE_SIZE, 2 * D_B), F32)])
    out = pl.pallas_call(
        functools.partial(_diffattn_decode_kernel, pages_per_step, lam_init),
        grid_spec=grid_spec,
        out_shape=jax.ShapeDtypeStruct((n_batch, t_new, D_B), F32),
        compiler_params=_params("arbitrary", "arbitrary"),
    )(page_table, q.reshape(n_batch, t_new, D_B), kvnew.reshape(n_batch, t_new, 2 * D_B), blam, sg,
      *([cache] * pages_per_step))
    return out.reshape(n_batch * t_new, D_B)


def _post_kernel(n_parts, n_lse, ff_chunk, has_final, x_ref, first_ref, *rest):
    o_refs = rest[:n_parts]
    lse_refs = rest[n_parts:n_parts + n_lse]
    rest = rest[n_parts + n_lse:]
    if n_lse == 0:
        second = o_refs[0][...].astype(BF16)
    else:
        lses = [r[...] for r in lse_refs]
        mx = functools.reduce(jnp.maximum, lses)
        es = [jnp.exp(l - mx) for l in lses]
        num = functools.reduce(lambda a, b: a + b, [e * r[...].astype(F32) for e, r in zip(es, o_refs)])
        second = (num / functools.reduce(lambda a, b: a + b, es)).astype(BF16)
    wout_ref, g_ref, wup_ref, wdn_ref = rest[:4]
    rest = rest[4:]
    gf_ref = rest[0] if has_final else None
    o_ref = rest[-1]
    mix = jnp.concatenate([first_ref[...].astype(BF16), second], axis=-1)
    x1 = x_ref[...] + _dot(mix, wout_ref[...])
    h = _rmsnorm(x1, g_ref[...]).astype(BF16)
    acc = x1
    for c in range(0, D_FF, ff_chunk):
        up = _dot(h, wup_ref[:, c:c + ff_chunk])
        act = jnp.square(jnp.maximum(up, 0.0)).astype(BF16)
        acc = acc + _dot(act, wdn_ref[c:c + ff_chunk, :])
    if has_final:
        acc = _rmsnorm(acc, gf_ref[...])
    o_ref[...] = acc


def _post_mixer(x, first, o_parts, lse_parts, wout, g, wup, wdn, gf, tm):
    m = x.shape[0]
    row = lambda n: pl.BlockSpec((tm, n), lambda i: (i, 0))
    has_final = gf is not None
    args = [x, first, *o_parts, *lse_parts, wout, g, wup, wdn]
    specs = [row(D_MODEL), row(first.shape[1])] + [row(a.shape[1]) for a in (*o_parts, *lse_parts)]
    specs += [_resident(wout.shape), _resident((1, D_MODEL)), _resident(wup.shape), _resident(wdn.shape)]
    if has_final:
        args.append(gf)
        specs.append(_resident((1, D_MODEL)))
    return pl.pallas_call(
        functools.partial(_post_kernel, len(o_parts), len(lse_parts), 1024, has_final),
        grid=(m // tm,),
        in_specs=specs,
        out_specs=row(D_MODEL),
        out_shape=jax.ShapeDtypeStruct((m, D_MODEL), F32),
        compiler_params=_params("arbitrary"),
    )(*args)


def _proj_cd_kernel(tiles_per_seq, tails, x_ref, g_ref, w_ref, wt_ref, cw_ref, hist_ref, *rest):
    n_g = len(D_WINDOWS)
    c_ref, ztail_ref = rest[0], rest[1]
    group_refs = rest[2:2 + 4 * n_g] if tiles_per_seq else rest[2:2 + 2 * n_g]
    carry_ref = rest[-1]
    tm = x_ref.shape[0]
    i = pl.program_id(0)
    h = _rmsnorm(x_ref[...], g_ref[...]).astype(BF16)
    gates = _dot(h, w_ref[:, 0:3 * D_C])
    b_gate = gates[:, 0:D_C]
    z = gates[:, D_C:2 * D_C] * gates[:, 2 * D_C:3 * D_C]
    row = lax.broadcasted_iota(jnp.int32, z.shape, 0)
    if tiles_per_seq:
        first = (i % tiles_per_seq) == 0
        prev = jnp.where(first, hist_ref[...], carry_ref[...])
        z1 = jnp.where(row == 0, prev[7:8], pltpu.roll(z, 1, 0))
        z2 = jnp.where(row == 0, prev[6:7], jnp.where(row == 1, prev[7:8], pltpu.roll(z, 2, 0)))
        carry_ref[...] = z[tm - SUBLANES:tm]
        ztail_ref[...] = z[tm - SUBLANES:tm]
    else:
        e = hist_ref[...]
        t = row & (SUBLANES - 1)
        z1 = jnp.where(t == 0, pltpu.roll(e, tm - 7, 0), pltpu.roll(z, 1, 0))
        z2 = jnp.where(t < 2, pltpu.roll(e, tm - 6, 0), pltpu.roll(z, 2, 0))
        ztail_ref[...] = z
    cw = cw_ref[...]
    cv = z2 * cw[0:1] + z1 * cw[1:2] + z * cw[2:3]
    c_ref[...] = (b_gate * cv).astype(c_ref.dtype)
    for gi in range(n_g):
        off = 3 * D_C + 3 * gi * D_DG
        if tiles_per_seq:
            q_ref, k_ref, v_ref, kvt_ref = group_refs[4 * gi:4 * gi + 4]
            res = _dot(h, w_ref[:, off:off + 3 * D_DG])
            q_ref[...] = (res[:, 0:D_DG] * QK_SCALE_D).astype(BF16)
            k_ref[...] = res[:, D_DG:2 * D_DG].astype(BF16)
            v_ref[...] = res[:, 2 * D_DG:3 * D_DG].astype(BF16)
            tail, first_kept = tails[gi]

            @pl.when(i % tiles_per_seq >= first_kept)
            def _(gi=gi, kvt_ref=kvt_ref, tail=tail):
                kvt_ref[...] = _dot_nt(wt_ref[gi], h[tm - tail:tm])
        else:
            q_ref, kvt_ref = group_refs[2 * gi:2 * gi + 2]
            q_ref[...] = _dot(h, w_ref[:, off:off + D_DG]) * QK_SCALE_D
            kvt_ref[...] = _dot_nt(wt_ref[gi], h)


def _proj_cd_prompt(x, g, w, wt, cw, n_batch, seq, tm):
    m = x.shape[0]
    tps = seq // tm
    row = lambda n, dt=None: pl.BlockSpec((tm, n), lambda i: (i, 0))
    hist = jnp.zeros((n_batch * SUBLANES, D_C), F32)
    out_shape = [jax.ShapeDtypeStruct((m, D_C), BF16), jax.ShapeDtypeStruct((m // tm * SUBLANES, D_C), F32)]
    out_specs = [row(D_C), pl.BlockSpec((SUBLANES, D_C), lambda i: (i, 0))]
    tails = []
    for win, _ in D_WINDOWS:
        keep = min(win, seq)
        tail = min(keep, tm)
        first_kept = tps - keep // tail
        tails.append((tail, first_kept))
        out_shape += [jax.ShapeDtypeStruct((m, D_DG), BF16)] * 3
        out_shape += [jax.ShapeDtypeStruct((n_batch, 2 * D_DG, keep), F32)]
        out_specs += [row(D_DG)] * 3
        out_specs += [pl.BlockSpec((None, 2 * D_DG, tail), functools.partial(
            lambda i, fk: (i // tps, 0, jnp.maximum(i % tps - fk, 0)), fk=first_kept))]
    return pl.pallas_call(
        functools.partial(_proj_cd_kernel, tps, tuple(tails)),
        grid=(m // tm,),
        in_specs=[row(D_MODEL), _resident((1, D_MODEL)), _resident(w.shape), _resident(wt.shape),
                  _resident(cw.shape), pl.BlockSpec((SUBLANES, D_C), lambda i: (i // tps, 0))],
        out_specs=out_specs,
        out_shape=out_shape,
        scratch_shapes=[pltpu.VMEM((SUBLANES, D_C), F32)],
        compiler_params=_params("arbitrary"),
    )(x, g, w, wt, cw, hist)


def _proj_cd_sample(x, g, w, wt, cw, hist_rows):
    m = x.shape[0]
    full = lambda r, n: pl.BlockSpec((r, n), lambda i: (0, 0))
    out_shape = [jax.ShapeDtypeStruct((m, D_C), F32), jax.ShapeDtypeStruct((m, D_C), F32)]
    out_specs = [full(m, D_C), full(m, D_C)]
    for _ in D_WINDOWS:
        out_shape += [jax.ShapeDtypeStruct((m, D_DG), F32), jax.ShapeDtypeStruct((2 * D_DG, m), F32)]
        out_specs += [full(m, D_DG), full(2 * D_DG, m)]
    return pl.pallas_call(
        functools.partial(_proj_cd_kernel, 0, ()),
        grid=(1,),
        in_specs=[full(m, D_MODEL), _resident((1, D_MODEL)), _resident(w.shape), _resident(wt.shape),
                  _resident(cw.shape), full(m, D_C)],
        out_specs=out_specs,
        out_shape=out_shape,
        scratch_shapes=[pltpu.VMEM((SUBLANES, D_C), F32)],
        compiler_params=_params("arbitrary"),
    )(x, g, w, wt, cw, hist_rows)


def _dilated_prompt_kernel(q_ref, kc_ref, kp_ref, vc_ref, vp_ref, o_ref, lse_ref):
    jb = pl.program_id(2)
    bl = q_ref.shape[0]
    shape = (bl, 2 * bl)
    row = lax.broadcasted_iota(jnp.int32, shape, 0)
    col = lax.broadcasted_iota(jnp.int32, shape, 1)
    dist = row - col + bl
    valid = (dist >= 0) & (dist <= N_BACK) & (jb * bl + col - bl >= 0)
    hw = 2 * HD_D
    lane = lax.broadcasted_iota(jnp.int32, (bl, hw), 1)
    for p in range(H_DG // 2):
        sl = slice(p * hw, (p + 1) * hw)
        qq = _split_heads_stack(q_ref[:, sl], HD_D)
        kcat = jnp.concatenate([kp_ref[:, sl], kc_ref[:, sl]], axis=0)
        vcat = jnp.concatenate([vp_ref[:, sl], vc_ref[:, sl]], axis=0)
        outs, lses = [], []
        for e in range(2):
            s = _dot_nt(qq[e * bl:(e + 1) * bl], kcat)
            s = jnp.where(valid, s, -jnp.inf)
            m = jnp.max(s, axis=-1, keepdims=True)
            pr = jnp.exp(s - m)
            l = jnp.sum(pr, axis=-1, keepdims=True)
            outs.append(_dot(pr.astype(BF16), vcat) / l)
            lses.append(m + jnp.log(l))
        o_ref[:, sl] = jnp.where(lane < HD_D, outs[0], outs[1]).astype(o_ref.dtype)
        lse_ref[:, sl] = jnp.where(lane < HD_D, lses[0], lses[1])


def _dilated_prompt(q, k, v, dil, n_batch, seq):
    length = seq // dil
    bl = N_BACK
    nblk = length // bl
    view = lambda a: a.reshape(n_batch, length, dil * D_DG)
    cur = pl.BlockSpec((None, bl, D_DG), lambda b, r, j: (b, j, r))
    prev = pl.BlockSpec((None, bl, D_DG), lambda b, r, j: (b, jnp.maximum(j - 1, 0), r))
    o, lse = pl.pallas_call(
        _dilated_prompt_kernel,
        grid=(n_batch, dil, nblk),
        in_specs=[cur, cur, prev, cur, prev],
        out_specs=[cur, cur],
        out_shape=[jax.ShapeDtypeStruct((n_batch, length, dil * D_DG), BF16),
                   jax.ShapeDtypeStruct((n_batch, length, dil * D_DG), F32)],
        compiler_params=_params("arbitrary", "arbitrary", "arbitrary"),
    )(view(q), view(k), view(k), view(v), view(v))
    return o.reshape(n_batch * seq, D_DG), lse.reshape(n_batch * seq, D_DG)


def _dilated_sample_kernel(dil, buf_len, q_ref, cache_ref, next_ref, kvnew_ref, newc_ref, o_ref, lse_ref,
                           m_scr, l_scr, acc_scr):
    b = pl.program_id(0)
    j = pl.program_id(1)
    last = pl.num_programs(1) - 1
    tk = cache_ref.shape[1]
    t_new = q_ref.shape[0]
    hw = 2 * HD_D
    n_pairs = H_DG // 2

    seq_in_block = b % (LANES // t_new)
    new_tile = pltpu.roll(kvnew_ref[...], (LANES - seq_in_block * t_new) % LANES, 1)

    lane = lax.broadcasted_iota(jnp.int32, (cache_ref.shape[0], LANES), 1)
    following = jnp.where(j == last, new_tile, next_ref[...])
    shifted = [pltpu.roll(cache_ref[:, c * LANES:(c + 1) * LANES], LANES - t_new, 1) for c in range(tk // LANES)]
    shifted.append(pltpu.roll(following, LANES - t_new, 1))
    for c in range(tk // LANES):
        newc_ref[:, c * LANES:(c + 1) * LANES] = jnp.where(lane < LANES - t_new, shifted[c], shifted[c + 1])

    @pl.when(j == 0)
    def _():
        _init_online(m_scr, l_scr, acc_scr)

    q = q_ref[...].astype(BF16)
    qs = [_split_heads_stack(q[:, p * hw:(p + 1) * hw], HD_D) for p in range(n_pairs)]

    def attend(kt_of, vt_of, valid):
        for p in range(n_pairs):
            s = jnp.where(valid, _dot(qs[p], kt_of(p).astype(BF16)), NEG_BIG)
            _online_update(slice(2 * t_new * p, 2 * t_new * (p + 1)), s, vt_of(p).astype(BF16), valid,
                           m_scr, l_scr, acc_scr, v_transposed=True)

    def iotas(n_keys):
        shape = (2 * t_new, n_keys)
        row = lax.broadcasted_iota(jnp.int32, shape, 0)
        return row & (t_new - 1), lax.broadcasted_iota(jnp.int32, shape, 1)

    qi, col = iotas(tk)
    d = buf_len + qi - (j * tk + col)
    attend(lambda p: cache_ref[p * hw:(p + 1) * hw, :], lambda p: cache_ref[D_DG + p * hw:D_DG + (p + 1) * hw, :],
           ((d & (dil - 1)) == 0) & (d <= N_BACK * dil))

    @pl.when(j == last)
    def _():
        qi, col = iotas(LANES)
        d = qi - col
        attend(lambda p: new_tile[p * hw:(p + 1) * hw], lambda p: new_tile[D_DG + p * hw:D_DG + (p + 1) * hw],
               (col < t_new) & (d >= 0) & ((d & (dil - 1)) == 0))
        lane = lax.broadcasted_iota(jnp.int32, (t_new, hw), 1)
        for p in range(n_pairs):
            rows = slice(2 * t_new * p, 2 * t_new * (p + 1))
            acc, l, m = acc_scr[rows], l_scr[rows], m_scr[rows]
            o = acc / l
            lse = m + jnp.log(l)
            o_ref[:, p * hw:(p + 1) * hw] = jnp.where(lane < HD_D, o[0:t_new], o[t_new:2 * t_new])
            lse_ref[:, p * hw:(p + 1) * hw] = jnp.where(lane < HD_D, lse[0:t_new], lse[t_new:2 * t_new])


def _dilated_sample(q, cache_t, kvnew_t, dil, n_batch, t_new, tk):
    buf_len = cache_t.shape[2]
    nkt = buf_len // tk
    lane_blocks = tk // LANES
    seqs_per_block = LANES // t_new
    small = lambda n: pl.BlockSpec((None, t_new, n), lambda b, j: (b, 0, 0))
    newc, o, lse = pl.pallas_call(
        functools.partial(_dilated_sample_kernel, dil, buf_len),
        grid=(n_batch, nkt),
        in_specs=[small(D_DG),
                  pl.BlockSpec((None, 2 * D_DG, tk), lambda b, j: (b, 0, j)),
                  pl.BlockSpec((None, 2 * D_DG, LANES),
                               lambda b, j: (b, 0, jnp.minimum((j + 1) * lane_blocks, nkt * lane_blocks - 1))),
                  pl.BlockSpec((2 * D_DG, LANES), lambda b, j: (0, b // seqs_per_block))],
        out_specs=[pl.BlockSpec((None, 2 * D_DG, tk), lambda b, j: (b, 0, j)), small(D_DG), small(D_DG)],
        out_shape=[jax.ShapeDtypeStruct(cache_t.shape, F32),
                   jax.ShapeDtypeStruct((n_batch, t_new, D_DG), F32),
                   jax.ShapeDtypeStruct((n_batch, t_new, D_DG), F32)],
        scratch_shapes=[pltpu.VMEM((2 * t_new * (H_DG // 2), 1), F32), pltpu.VMEM((2 * t_new * (H_DG // 2), 1), F32),
                        pltpu.VMEM((2 * t_new * (H_DG // 2), 2 * HD_D), F32)],
        compiler_params=_params("arbitrary", "arbitrary"),
    )(q.reshape(n_batch, t_new, D_DG), cache_t, cache_t, kvnew_t)
    return newc, o.reshape(n_batch * t_new, D_DG), lse.reshape(n_batch * t_new, D_DG)


def kernel(x_prompt, x_sample, state_a_pool, cache_b_kv, state_c_conv, cache_d0_kv, cache_d1_kv, cache_d2_kv, page_table, norm_mix_g, norm_mlp_g, norm_out_g, w_in_ab, w_out_ab, a_mix, a_scale, b_lam, b_subln_g, w_in_cd, w_out_cd, c_conv_w, w_up, w_down):
    bp, seq, _ = x_prompt.shape
    bs, t_new, _ = x_sample.shape
    n_pages = page_table.shape[1]
    past_len = n_pages * cache_b_kv.shape[2]
    mp, ms = bp * seq, bs * t_new
    tm_p = 512
    xp = x_prompt.reshape(mp, D_MODEL)
    xs = x_sample.reshape(ms, D_MODEL)
    g2 = lambda v: v.reshape(1, -1)
    d_caches = (cache_d0_kv, cache_d1_kv, cache_d2_kv)

    lam_init = 0.8 - 0.6 * math.exp(-0.3 * 0)
    w_in = w_in_ab[0].astype(BF16)
    w_out = w_out_ab[0].astype(BF16)
    wu, wd = w_up[0].astype(BF16), w_down[0].astype(BF16)
    amix = a_mix[0].astype(BF16)
    gm, gl = g2(norm_mix_g[0]), g2(norm_mlp_g[0])
    ascale, sg = g2(a_scale[0]), g2(b_subln_g[0])

    u_p, q_p, kv_p, k_p, v_p = _proj_ab(xp, gm, w_in, tm_p, BF16)
    hist_p = jnp.zeros((bp, POOL_HIST + 1, D_A), F32)
    a_p = _pool_mix(u_p, hist_p, amix, ascale, 0, bp, seq)
    ob_p = _diffattn_prompt(q_p, k_p, v_p, b_lam[0], sg, lam_init, bp, seq, 512)
    xp = _post_mixer(xp, a_p, (ob_p,), (), w_out, gl, wu, wd, None, tm_p)

    u_s, q_s, kv_s, _, _ = _proj_ab(xs, gm, w_in, ms, F32)
    hist_s = jnp.pad(state_a_pool[0], ((0, 0), (1, 0), (0, 0)))
    a_s = _pool_mix(u_s, hist_s, amix, ascale, past_len, bs, t_new)
    pool_pages = cache_b_kv[0].reshape(cache_b_kv.shape[1], PAGE_SIZE * 2 * H_B, 2 * HD_B)
    ob_s = _diffattn_decode(page_table, pool_pages, q_s, kv_s, b_lam[0], sg, lam_init, bs, t_new, 16)
    xs = _post_mixer(xs, a_s, (ob_s,), (), w_out, gl, wu, wd, None, ms)

    u_p3 = u_p.reshape(bp, seq, D_A)
    new_pool_p = u_p3[:, seq - POOL_HIST:][None]
    new_pool_s = jnp.concatenate([state_a_pool[0], u_s.reshape(bs, t_new, D_A)], axis=1)[:, -POOL_HIST:][None]
    new_bkv_p = kv_p.reshape(1, bp, seq, 2, H_B, 2 * HD_B)
    new_bkv_s = kv_s.reshape(1, bs, t_new, 2, H_B, 2 * HD_B)

    w_in = w_in_cd[0].astype(BF16)
    w_out = w_out_cd[0].astype(BF16)
    wu, wd = w_up[1].astype(BF16), w_down[1].astype(BF16)
    gm, gl, gf = g2(norm_mix_g[1]), g2(norm_mlp_g[1]), g2(norm_out_g)
    cw = c_conv_w[0]
    w_kv_t = jnp.stack([w_in[:, 3 * D_C + (3 * gi + 1) * D_DG:3 * D_C + (3 * gi + 3) * D_DG].T
                        for gi in range(len(D_WINDOWS))])

    def from_position_minor(a):
        a = a.reshape(a.shape[0], 2, H_DG, HD_D, a.shape[2])
        return jnp.transpose(a, (0, 4, 1, 2, 3))[None]

    outs = _proj_cd_prompt(xp, gm, w_in, w_kv_t, cw, bp, seq, tm_p)
    c_p, ztail_p = outs[0], outs[1]
    o_parts, lse_parts, new_d_p = [], [], []
    for gi, (win, dil) in enumerate(D_WINDOWS):
        q_g, k_g, v_g, kvt_g = outs[2 + 4 * gi:6 + 4 * gi]
        o_g, lse_g = _dilated_prompt(q_g, k_g, v_g, dil, bp, seq)
        o_parts.append(o_g)
        lse_parts.append(lse_g)
        new_d_p.append(from_position_minor(kvt_g))
    y_p = _post_mixer(xp, c_p, o_parts, lse_parts, w_out, gl, wu, wd, gf, tm_p)
    tiles_per_seq = seq // tm_p
    new_conv_p = ztail_p.reshape(bp, tiles_per_seq, SUBLANES, D_C)[:, -1, SUBLANES - (CONV_W - 1):][None]

    hist_rows = jnp.pad(state_c_conv[0], ((0, 0), (t_new - (CONV_W - 1), 0), (0, 0))).reshape(ms, D_C)
    outs = _proj_cd_sample(xs, gm, w_in, w_kv_t, cw, hist_rows)
    c_s, z_s = outs[0], outs[1]
    o_parts, lse_parts, new_d_s = [], [], []
    for gi, (win, dil) in enumerate(D_WINDOWS):
        q_g, kvt_g = outs[2 + 2 * gi:4 + 2 * gi]
        cache = d_caches[gi][0]
        buf_len = cache.shape[1]
        cache_t = jnp.transpose(cache, (0, 2, 3, 4, 1)).reshape(bs, 2 * D_DG, buf_len)
        newc, o_g, lse_g = _dilated_sample(q_g, cache_t, kvt_g, dil, bs, t_new, min(buf_len, 512))
        o_parts.append(o_g)
        lse_parts.append(lse_g)
        new_d_s.append(from_position_minor(newc))
    y_s = _post_mixer(xs, c_s, o_parts, lse_parts, w_out, gl, wu, wd, gf, ms)
    new_conv_s = z_s.reshape(bs, t_new, D_C)[:, t_new - (CONV_W - 1):][None]

    return (y_p.reshape(bp, seq, D_MODEL), y_s.reshape(bs, t_new, D_MODEL),
            new_pool_p, new_pool_s, new_bkv_p, new_bkv_s, new_conv_p, new_conv_s,
            new_d_p[0], new_d_s[0], new_d_p[1], new_d_s[1], new_d_p[2], new_d_s[2])
```

```python
import functools
import math

import jax
import jax.numpy as jnp
from jax import lax
from jax.experimental import pallas as pl
from jax.experimental.pallas import tpu as pltpu

F32 = jnp.float32
BF16 = jnp.bfloat16

D_MODEL = 1024
RMS_EPS = 1e-6
D_A = 512
POOL_WINDOWS = (2, 4, 8, 16)
POOL_GROUP = 128
POOL_HIST = 15
H_B = 4
HD_B = 64
D_B = 512
D_C = 512
CONV_W = 3
D_WINDOWS = ((128, 1), (512, 4), (2048, 16))
N_BACK = 128
H_DG = 8
HD_D = 64
D_DG = 512
D_FF = 4096
PAGE_SIZE = 128

LANES = 128
SUBLANES = 8
VMEM_LIMIT_BYTES = 56 * 1024 * 1024
NEG_BIG = -1e30

QK_SCALE_B = HD_B ** -0.5
QK_SCALE_D = HD_D ** -0.5


def _params(*sem):
    return pltpu.CompilerParams(dimension_semantics=sem, vmem_limit_bytes=VMEM_LIMIT_BYTES)


def _resident(shape):
    nd = len(shape)
    return pl.BlockSpec(shape, lambda *_: (0,) * nd, pipeline_mode=pl.Buffered(1))


def _rmsnorm(x, g):
    return x * lax.rsqrt(jnp.mean(x * x, axis=-1, keepdims=True) + RMS_EPS) * g


def _dot(a, b):
    return jnp.dot(a, b, preferred_element_type=F32)


def _dot_nt(a, b):
    return lax.dot_general(a, b, (((1,), (1,)), ((), ())), preferred_element_type=F32)


def _split_heads_stack(q, half):
    lane = lax.broadcasted_iota(jnp.int32, q.shape, 1)
    zero = jnp.zeros_like(q)
    return jnp.concatenate([jnp.where(lane < half, q, zero), jnp.where(lane >= half, q, zero)], axis=0)


def _lane_blocks(n):
    return [slice(c * LANES, (c + 1) * LANES) for c in range(n // LANES)]


def _store_streams(ref, val, dil, scr):
    if dil == 1:
        ref[...] = val.astype(ref.dtype)
        return
    width = val.shape[1]
    for c, sl in enumerate(_lane_blocks(width)):
        scr[c] = val[:, sl]
    for r in range(dil):
        for c, sl in enumerate(_lane_blocks(width)):
            ref[:, r * width + sl.start:r * width + sl.stop] = (
                scr[c, pl.ds(r, ref.shape[0], stride=dil), :].astype(ref.dtype))


def _rows_from_streams(ref, dil, scr):
    width = ref.shape[1] // dil
    if dil == 1:
        return [ref[:, sl].astype(F32) for sl in _lane_blocks(width)]
    for r in range(dil):
        for c, sl in enumerate(_lane_blocks(width)):
            scr[c, pl.ds(r, ref.shape[0], stride=dil), :] = ref[:, r * width + sl.start:r * width + sl.stop].astype(F32)
    return [scr[c] for c in range(width // LANES)]


def _diff_lambda(blam, lam_init):
    a = jnp.sum(blam[0:1] * blam[1:2], axis=-1, keepdims=True)
    b = jnp.sum(blam[2:3] * blam[3:4], axis=-1, keepdims=True)
    return jnp.exp(a) - jnp.exp(b) + lam_init


def _proj_ab_kernel(x_ref, g_ref, w_ref, u_ref, q_ref, kv_ref, k_ref, v_ref):
    h = _rmsnorm(x_ref[...], g_ref[...]).astype(BF16)
    u_ref[...] = _dot(h, w_ref[:, 0:D_A])
    q_ref[...] = (_dot(h, w_ref[:, D_A:D_A + D_B]) * QK_SCALE_B).astype(q_ref.dtype)
    kv = _dot(h, w_ref[:, D_A + D_B:D_A + 3 * D_B])
    kv_ref[...] = kv
    k_ref[...] = kv[:, :D_B].astype(BF16)
    v_ref[...] = kv[:, D_B:].astype(BF16)


def _proj_ab(x, g, w, tm, q_dtype):
    m = x.shape[0]
    n_in = w.shape[1]
    row = lambda n: pl.BlockSpec((tm, n), lambda i: (i, 0))
    return pl.pallas_call(
        _proj_ab_kernel,
        grid=(m // tm,),
        in_specs=[row(D_MODEL), _resident((1, D_MODEL)), _resident((D_MODEL, n_in))],
        out_specs=[row(D_A), row(D_B), row(2 * D_B), row(D_B), row(D_B)],
        out_shape=[jax.ShapeDtypeStruct((m, D_A), F32), jax.ShapeDtypeStruct((m, D_B), q_dtype),
                   jax.ShapeDtypeStruct((m, 2 * D_B), F32), jax.ShapeDtypeStruct((m, D_B), BF16),
                   jax.ShapeDtypeStruct((m, D_B), BF16)],
        compiler_params=_params("arbitrary"),
    )(x, g, w)


def _pool_kernel(pos0, tchunk, u_ref, hist_ref, amix_ref, ascale_ref, o_ref, ext_ref):
    t_len = u_ref.shape[0]
    hpad = hist_ref.shape[0]
    ext_ref[0:hpad, :] = hist_ref[...]
    ext_ref[hpad:hpad + t_len, :] = u_ref[...]
    for t0 in range(0, t_len, tchunk):
        pos = pos0 + t0 + lax.broadcasted_iota(jnp.int32, (tchunk, 1), 0)
        for gi, w in enumerate(POOL_WINDOWS):
            sl = slice(gi * POOL_GROUP, (gi + 1) * POOL_GROUP)
            u = u_ref[t0:t0 + tchunk, sl]
            acc = u
            for j in range(1, w):
                acc = acc + ext_ref[hpad + t0 - j:hpad + t0 - j + tchunk, sl]
            cnt = jnp.minimum(pos + 1, w).astype(F32)
            p = (acc / cnt - u).astype(BF16)
            y = _dot(p, amix_ref[gi]) * ascale_ref[:, sl]
            o_ref[t0:t0 + tchunk, sl] = y.astype(o_ref.dtype)


def _pool_mix(u, hist16, amix, ascale, pos0, n_seq, t_len):
    hpad = hist16.shape[1]
    tchunk = min(t_len, 256)
    out = pl.pallas_call(
        functools.partial(_pool_kernel, pos0, tchunk),
        grid=(n_seq,),
        in_specs=[pl.BlockSpec((None, t_len, D_A), lambda s: (s, 0, 0)),
                  pl.BlockSpec((None, hpad, D_A), lambda s: (s, 0, 0)),
                  _resident(amix.shape), _resident((1, D_A))],
        out_specs=pl.BlockSpec((None, t_len, D_A), lambda s: (s, 0, 0)),
        out_shape=jax.ShapeDtypeStruct((n_seq, t_len, D_A), BF16 if t_len % 16 == 0 else F32),
        scratch_shapes=[pltpu.VMEM((hpad + t_len, D_A), F32)],
        compiler_params=_params("arbitrary"),
    )(u.reshape(n_seq, t_len, D_A), hist16, amix, ascale)
    return out.reshape(n_seq * t_len, D_A)


def _diff_finalize(acc, l, lam, sg, lam_init, t):
    o = acc[0:t] / l[0:t] - lam * (acc[t:2 * t] / l[t:2 * t])
    o = o * lax.rsqrt(jnp.mean(o * o, axis=-1, keepdims=True) + RMS_EPS) * sg
    return o * (1.0 - lam_init)


def _diffattn_prompt_kernel(lam_init, q_ref, k_ref, v_ref, blam_ref, sg_ref, o_ref, m_scr, l_scr, acc_scr):
    tq = q_ref.shape[0]
    hw = 2 * HD_B
    i = pl.program_id(1)
    heads = [slice(h * hw, (h + 1) * hw) for h in range(H_B)]
    qqs = [_split_heads_stack(q_ref[:, sl], HD_B) for sl in heads]

    def kv_block(j, sl):
        off = pl.multiple_of(j * tq, tq)
        return k_ref[pl.ds(off, tq), sl], v_ref[pl.ds(off, tq), sl]

    shape = (2 * tq, tq)
    row = lax.broadcasted_iota(jnp.int32, shape, 0)
    col = lax.broadcasted_iota(jnp.int32, shape, 1)
    causal = col <= jnp.where(row >= tq, row - tq, row)
    for h, sl in enumerate(heads):
        kd, vd = kv_block(i, sl)
        s = jnp.where(causal, _dot_nt(qqs[h], kd), -jnp.inf)
        m = jnp.max(s, axis=-1, keepdims=True)
        p = jnp.exp(s - m)
        m_scr[h] = m
        l_scr[h] = jnp.sum(p, axis=-1, keepdims=True)
        acc_scr[h] = _dot(p.astype(BF16), vd)

    def body(j, carry):
        for h, sl in enumerate(heads):
            kj, vj = kv_block(j, sl)
            s = _dot_nt(qqs[h], kj)
            m_prev = m_scr[h]
            m_new = jnp.maximum(m_prev, jnp.max(s, axis=-1, keepdims=True))
            alpha = jnp.exp(m_prev - m_new)
            p = jnp.exp(s - m_new)
            l_scr[h] = alpha * l_scr[h] + jnp.sum(p, axis=-1, keepdims=True)
            acc_scr[h] = alpha * acc_scr[h] + _dot(p.astype(BF16), vj)
            m_scr[h] = m_new
        return carry

    lax.fori_loop(0, i, body, 0)
    lam = _diff_lambda(blam_ref[...], lam_init)
    for h, sl in enumerate(heads):
        o = _diff_finalize(acc_scr[h], l_scr[h], lam, sg_ref[...], lam_init, tq)
        o_ref[:, sl] = o.astype(o_ref.dtype)


def _diffattn_prompt(q, k, v, blam, sg, lam_init, n_batch, seq, tq):
    nq = seq // tq
    hw = 2 * HD_B
    return pl.pallas_call(
        functools.partial(_diffattn_prompt_kernel, lam_init),
        grid=(n_batch, nq),
        in_specs=[pl.BlockSpec((tq, D_B), lambda b, i: (b * nq + i, 0)),
                  pl.BlockSpec((seq, D_B), lambda b, i: (b, 0)),
                  pl.BlockSpec((seq, D_B), lambda b, i: (b, 0)),
                  _resident(blam.shape), _resident((1, hw))],
        out_specs=pl.BlockSpec((tq, D_B), lambda b, i: (b * nq + i, 0)),
        out_shape=jax.ShapeDtypeStruct((n_batch * seq, D_B), BF16),
        scratch_shapes=[pltpu.VMEM((H_B, 2 * tq, 1), F32), pltpu.VMEM((H_B, 2 * tq, 1), F32),
                        pltpu.VMEM((H_B, 2 * tq, hw), F32)],
        compiler_params=_params("arbitrary", "arbitrary"),
    )(q, k, v, blam, sg)


def _online_update(rows, s, v, valid, m_scr, l_scr, acc_scr, v_transposed=False):
    m_prev = m_scr[rows]
    m_new = jnp.maximum(m_prev, jnp.max(s, axis=-1, keepdims=True))
    alpha = jnp.exp(m_prev - m_new)
    p = jnp.exp(s - m_new)
    if valid is not None:
        p = jnp.where(valid, p, 0.0)
    pv = _dot_nt(p.astype(BF16), v) if v_transposed else _dot(p.astype(BF16), v)
    l_scr[rows] = alpha * l_scr[rows] + jnp.sum(p, axis=-1, keepdims=True)
    acc_scr[rows] = alpha * acc_scr[rows] + pv
    m_scr[rows] = m_new


def _init_online(m_scr, l_scr, acc_scr):
    m_scr[...] = jnp.full(m_scr.shape, NEG_BIG, F32)
    l_scr[...] = jnp.zeros(l_scr.shape, F32)
    acc_scr[...] = jnp.zeros(acc_scr.shape, F32)


def _diffattn_decode_kernel(pages_per_step, lam_init, pt_ref, q_ref, kvnew_ref, blam_ref, sg_ref, *rest):
    page_refs = rest[:pages_per_step]
    o_ref, m_scr, l_scr, acc_scr, pad_scr = rest[pages_per_step:]
    j = pl.program_id(1)
    t_new = q_ref.shape[0]
    hw = 2 * HD_B
    q = q_ref[...].astype(BF16)
    qs = [_split_heads_stack(q[:, h * hw:(h + 1) * hw], HD_B) for h in range(H_B)]

    @pl.when(j == 0)
    def _():
        _init_online(m_scr, l_scr, acc_scr)

    rows_per_pos = 2 * H_B
    for h in range(H_B):
        kh = jnp.concatenate([r[pl.ds(h, PAGE_SIZE, stride=rows_per_pos), :] for r in page_refs], axis=0)
        vh = jnp.concatenate([r[pl.ds(H_B + h, PAGE_SIZE, stride=rows_per_pos), :] for r in page_refs], axis=0)
        s = _dot_nt(qs[h], kh.astype(BF16))
        _online_update(slice(2 * t_new * h, 2 * t_new * (h + 1)), s, vh.astype(BF16), None, m_scr, l_scr, acc_scr)

    @pl.when(j == pl.num_programs(1) - 1)
    def _():
        pad_scr[...] = jnp.zeros(pad_scr.shape, F32)
        pad_scr[0:t_new, :] = kvnew_ref[...]
        shape = (2 * t_new, pad_scr.shape[0])
        row = lax.broadcasted_iota(jnp.int32, shape, 0)
        col = lax.broadcasted_iota(jnp.int32, shape, 1)
        valid = col <= jnp.where(row >= t_new, row - t_new, row)
        for h in range(H_B):
            kh = pad_scr[:, h * hw:(h + 1) * hw].astype(BF16)
            vh = pad_scr[:, D_B + h * hw:D_B + (h + 1) * hw].astype(BF16)
            s = jnp.where(valid, _dot_nt(qs[h], kh), NEG_BIG)
            _online_update(slice(2 * t_new * h, 2 * t_new * (h + 1)), s, vh, valid, m_scr, l_scr, acc_scr)
        lam = _diff_lambda(blam_ref[...], lam_init)
        for h in range(H_B):
            rows = slice(2 * t_new * h, 2 * t_new * (h + 1))
            o = _diff_finalize(acc_scr[rows], l_scr[rows], lam, sg_ref[...], lam_init, t_new)
            o_ref[:, h * hw:(h + 1) * hw] = o.astype(o_ref.dtype)


def _diffattn_decode(page_table, cache, q, kvnew, blam, sg, lam_init, n_batch, t_new, pages_per_step):
    n_pages = page_table.shape[1]
    hw = 2 * HD_B
    page_specs = [
        pl.BlockSpec((None, PAGE_SIZE * 2 * H_B, hw), functools.partial(
            lambda b, j, pt, t: (pt[b, j * pages_per_step + t], 0, 0), t=t))
        for t in range(pages_per_step)]
    grid_spec = pltpu.PrefetchScalarGridSpec(
        num_scalar_prefetch=1,
        grid=(n_batch, n_pages // pages_per_step),
        in_specs=[pl.BlockSpec((None, t_new, D_B), lambda b, j, pt: (b, 0, 0)),
                  pl.BlockSpec((None, t_new, 2 * D_B), lambda b, j, pt: (b, 0, 0)),
                  pl.BlockSpec(blam.shape, lambda b, j, pt: (0, 0)),
                  pl.BlockSpec((1, hw), lambda b, j, pt: (0, 0))] + page_specs,
        out_specs=pl.BlockSpec((None, t_new, D_B), lambda b, j, pt: (b, 0, 0)),
        scratch_shapes=[pltpu.VMEM((2 * t_new * H_B, 1), F32), pltpu.VMEM((2 * t_new * H_B, 1), F32),
                        pltpu.VMEM((2 * t_new * H_B, hw), F32), pltpu.VMEM((PAGE_SIZE, 2 * D_B), F32)])
    out = pl.pallas_call(
        functools.partial(_diffattn_decode_kernel, pages_per_step, lam_init),
        grid_spec=grid_spec,
        out_shape=jax.ShapeDtypeStruct((n_batch, t_new, D_B), F32),
        compiler_params=_params("arbitrary", "arbitrary"),
    )(page_table, q.reshape(n_batch, t_new, D_B), kvnew.reshape(n_batch, t_new, 2 * D_B), blam, sg,
      *([cache] * pages_per_step))
    return out.reshape(n_batch * t_new, D_B)


def _post_kernel(dils, ff_chunk, has_final, x_ref, first_ref, *rest):
    n_g = max(len(dils), 1)
    o_refs = rest[:n_g]
    lse_refs = rest[n_g:n_g + len(dils)]
    rest = rest[n_g + len(dils):]
    wout_ref, g_ref, wup_ref, wdn_ref = rest[:4]
    rest = rest[4:]
    gf_ref = rest[0] if has_final else None
    o_ref = rest[1] if has_final else rest[0]
    scratch = list(rest[2 if has_final else 1:])
    if not dils:
        second = o_refs[0][...].astype(BF16)
    else:
        scr_of = lambda d: scratch.pop(0) if d > 1 else None
        os_ = [_rows_from_streams(r, d, scr_of(d)) for r, d in zip(o_refs, dils)]
        ls_ = [_rows_from_streams(r, d, scr_of(d)) for r, d in zip(lse_refs, dils)]
        merged = []
        for c in range(len(os_[0])):
            lses = [l[c] for l in ls_]
            mx = functools.reduce(jnp.maximum, lses)
            es = [jnp.exp(l - mx) for l in lses]
            num = functools.reduce(lambda a, b: a + b, [e * o[c] for e, o in zip(es, os_)])
            merged.append((num / functools.reduce(lambda a, b: a + b, es)).astype(BF16))
        second = jnp.concatenate(merged, axis=-1)
    mix = jnp.concatenate([first_ref[...].astype(BF16), second], axis=-1)
    x1 = x_ref[...] + _dot(mix, wout_ref[...])
    h = _rmsnorm(x1, g_ref[...]).astype(BF16)
    acc = x1
    for c in range(0, D_FF, ff_chunk):
        up = _dot(h, wup_ref[:, c:c + ff_chunk])
        act = jnp.square(jnp.maximum(up, 0.0)).astype(BF16)
        acc = acc + _dot(act, wdn_ref[c:c + ff_chunk, :])
    if has_final:
        acc = _rmsnorm(acc, gf_ref[...])
    o_ref[...] = acc


def _post_mixer(x, first, o_parts, lse_parts, dils, wout, g, wup, wdn, gf, tm):
    m = x.shape[0]
    row = lambda n: pl.BlockSpec((tm, n), lambda i: (i, 0))

    def part_spec(a, dil):
        if dil == 1:
            return row(a.shape[1])
        tiles_per_seq = a.shape[1] * dil // tm
        return pl.BlockSpec((None, tm // dil, a.shape[2]), lambda i: (i // tiles_per_seq, i % tiles_per_seq, 0))

    has_final = gf is not None
    part_dils = tuple(dils) if dils else (1,)
    args = [x, first, *o_parts, *lse_parts, wout, g, wup, wdn]
    specs = [row(D_MODEL), row(first.shape[1])]
    specs += [part_spec(a, d) for a, d in zip(o_parts, part_dils)] + [part_spec(a, d) for a, d in zip(lse_parts, dils)]
    specs += [_resident(wout.shape), _resident((1, D_MODEL)), _resident(wup.shape), _resident(wdn.shape)]
    if has_final:
        args.append(gf)
        specs.append(_resident((1, D_MODEL)))
    n_scratch = 2 * sum(d > 1 for d in dils)
    return pl.pallas_call(
        functools.partial(_post_kernel, tuple(dils), 1024, has_final),
        grid=(m // tm,),
        in_specs=specs,
        out_specs=row(D_MODEL),
        out_shape=jax.ShapeDtypeStruct((m, D_MODEL), F32),
        scratch_shapes=[pltpu.VMEM((D_DG // LANES, tm, LANES), F32)] * n_scratch,
        compiler_params=_params("arbitrary"),
    )(*args)


def _proj_cd_kernel(tiles_per_seq, tails, x_ref, g_ref, w_ref, wt_ref, cw_ref, hist_ref, *rest):
    n_g = len(D_WINDOWS)
    c_ref, ztail_ref = rest[0], rest[1]
    group_refs = rest[2:2 + 4 * n_g] if tiles_per_seq else rest[2:2 + 2 * n_g]
    carry_ref, stream_scr = rest[-2], rest[-1]
    tm = x_ref.shape[0]
    i = pl.program_id(0)
    h = _rmsnorm(x_ref[...], g_ref[...]).astype(BF16)
    gates = _dot(h, w_ref[:, 0:3 * D_C])
    b_gate = gates[:, 0:D_C]
    z = gates[:, D_C:2 * D_C] * gates[:, 2 * D_C:3 * D_C]
    row = lax.broadcasted_iota(jnp.int32, z.shape, 0)
    if tiles_per_seq:
        first = (i % tiles_per_seq) == 0
        prev = jnp.where(first, hist_ref[...], carry_ref[...])
        z1 = jnp.where(row == 0, prev[7:8], pltpu.roll(z, 1, 0))
        z2 = jnp.where(row == 0, prev[6:7], jnp.where(row == 1, prev[7:8], pltpu.roll(z, 2, 0)))
        carry_ref[...] = z[tm - SUBLANES:tm]
        ztail_ref[...] = z[tm - SUBLANES:tm]
    else:
        e = hist_ref[...]
        t = row & (SUBLANES - 1)
        z1 = jnp.where(t == 0, pltpu.roll(e, tm - 7, 0), pltpu.roll(z, 1, 0))
        z2 = jnp.where(t < 2, pltpu.roll(e, tm - 6, 0), pltpu.roll(z, 2, 0))
        ztail_ref[...] = z
    cw = cw_ref[...]
    cv = z2 * cw[0:1] + z1 * cw[1:2] + z * cw[2:3]
    c_ref[...] = (b_gate * cv).astype(c_ref.dtype)
    for gi in range(n_g):
        off = 3 * D_C + 3 * gi * D_DG
        if tiles_per_seq:
            q_ref, k_ref, v_ref, kvt_ref = group_refs[4 * gi:4 * gi + 4]
            res = _dot(h, w_ref[:, off:off + 3 * D_DG])
            dil = D_WINDOWS[gi][1]
            _store_streams(q_ref, res[:, 0:D_DG] * QK_SCALE_D, dil, stream_scr)
            _store_streams(k_ref, res[:, D_DG:2 * D_DG], dil, stream_scr)
            _store_streams(v_ref, res[:, 2 * D_DG:3 * D_DG], dil, stream_scr)
            tail, first_kept = tails[gi]

            @pl.when(i % tiles_per_seq >= first_kept)
            def _(gi=gi, kvt_ref=kvt_ref, tail=tail):
                kvt_ref[...] = _dot_nt(wt_ref[gi], h[tm - tail:tm])
        else:
            q_ref, kvt_ref = group_refs[2 * gi:2 * gi + 2]
            q_ref[...] = _dot(h, w_ref[:, off:off + D_DG]) * QK_SCALE_D
            kvt_ref[...] = _dot_nt(wt_ref[gi], h)


def _proj_cd_prompt(x, g, w, wt, cw, n_batch, seq, tm):
    m = x.shape[0]
    tps = seq // tm
    row = lambda n, dt=None: pl.BlockSpec((tm, n), lambda i: (i, 0))
    hist = jnp.zeros((n_batch * SUBLANES, D_C), F32)
    out_shape = [jax.ShapeDtypeStruct((m, D_C), BF16), jax.ShapeDtypeStruct((m // tm * SUBLANES, D_C), F32)]
    out_specs = [row(D_C), pl.BlockSpec((SUBLANES, D_C), lambda i: (i, 0))]
    tails = []
    for win, dil in D_WINDOWS:
        keep = min(win, seq)
        tail = min(keep, tm)
        first_kept = tps - keep // tail
        tails.append((tail, first_kept))
        out_shape += [jax.ShapeDtypeStruct((n_batch, seq // dil, dil * D_DG), BF16)] * 3
        out_shape += [jax.ShapeDtypeStruct((n_batch, 2 * D_DG, keep), F32)]
        out_specs += [pl.BlockSpec((None, tm // dil, dil * D_DG), lambda i: (i // tps, i % tps, 0))] * 3
        out_specs += [pl.BlockSpec((None, 2 * D_DG, tail), functools.partial(
            lambda i, fk: (i // tps, 0, jnp.maximum(i % tps - fk, 0)), fk=first_kept))]
    return pl.pallas_call(
        functools.partial(_proj_cd_kernel, tps, tuple(tails)),
        grid=(m // tm,),
        in_specs=[row(D_MODEL), _resident((1, D_MODEL)), _resident(w.shape), _resident(wt.shape),
                  _resident(cw.shape), pl.BlockSpec((SUBLANES, D_C), lambda i: (i // tps, 0))],
        out_specs=out_specs,
        out_shape=out_shape,
        scratch_shapes=[pltpu.VMEM((SUBLANES, D_C), F32), pltpu.VMEM((D_DG // LANES, tm, LANES), F32)],
        compiler_params=_params("arbitrary"),
    )(x, g, w, wt, cw, hist)


def _proj_cd_sample(x, g, w, wt, cw, hist_rows):
    m = x.shape[0]
    full = lambda r, n: pl.BlockSpec((r, n), lambda i: (0, 0))
    out_shape = [jax.ShapeDtypeStruct((m, D_C), F32), jax.ShapeDtypeStruct((m, D_C), F32)]
    out_specs = [full(m, D_C), full(m, D_C)]
    for _ in D_WINDOWS:
        out_shape += [jax.ShapeDtypeStruct((m, D_DG), F32), jax.ShapeDtypeStruct((2 * D_DG, m), F32)]
        out_specs += [full(m, D_DG), full(2 * D_DG, m)]
    return pl.pallas_call(
        functools.partial(_proj_cd_kernel, 0, ()),
        grid=(1,),
        in_specs=[full(m, D_MODEL), _resident((1, D_MODEL)), _resident(w.shape), _resident(wt.shape),
                  _resident(cw.shape), full(m, D_C)],
        out_specs=out_specs,
        out_shape=out_shape,
        scratch_shapes=[pltpu.VMEM((SUBLANES, D_C), F32), pltpu.VMEM((D_DG // LANES, SUBLANES, LANES), F32)],
        compiler_params=_params("arbitrary"),
    )(x, g, w, wt, cw, hist_rows)


def _dilated_prompt_kernel(n_streams, q_ref, kc_ref, kp_ref, vc_ref, vp_ref, o_ref, lse_ref):
    bl = N_BACK
    n_qblk = q_ref.shape[0] // bl
    first_blk = pl.program_id(2) * n_qblk
    shape = (bl, 2 * bl)
    row = lax.broadcasted_iota(jnp.int32, shape, 0)
    col = lax.broadcasted_iota(jnp.int32, shape, 1)
    dist = row - col + bl
    band = (dist >= 0) & (dist <= N_BACK)
    hw = 2 * HD_D
    lane = lax.broadcasted_iota(jnp.int32, (bl, hw), 1)
    for r in range(n_streams):
        for qb in range(n_qblk):
            rows = slice(qb * bl, (qb + 1) * bl)
            prev_rows = slice((qb - 1) * bl, qb * bl)
            valid = (band & ((first_blk * bl + col - bl) >= 0)) if qb == 0 else band
            for p in range(H_DG // 2):
                sl = slice(r * D_DG + p * hw, r * D_DG + (p + 1) * hw)
                qq = _split_heads_stack(q_ref[rows, sl], HD_D)
                k_prev = kp_ref[:, sl] if qb == 0 else kc_ref[prev_rows, sl]
                v_prev = vp_ref[:, sl] if qb == 0 else vc_ref[prev_rows, sl]
                kcat = jnp.concatenate([k_prev, kc_ref[rows, sl]], axis=0)
                vcat = jnp.concatenate([v_prev, vc_ref[rows, sl]], axis=0)
                outs, lses = [], []
                for e in range(2):
                    s = _dot_nt(qq[e * bl:(e + 1) * bl], kcat)
                    s = jnp.where(valid, s, -jnp.inf)
                    m = jnp.max(s, axis=-1, keepdims=True)
                    pr = jnp.exp(s - m)
                    l = jnp.sum(pr, axis=-1, keepdims=True)
                    outs.append(_dot(pr.astype(BF16), vcat) / l)
                    lses.append(m + jnp.log(l))
                o_ref[rows, sl] = jnp.where(lane < HD_D, outs[0], outs[1]).astype(o_ref.dtype)
                lse_ref[rows, sl] = jnp.where(lane < HD_D, lses[0], lses[1])


def _dilated_prompt(q, k, v, dil, blocks_per_step):
    n_batch, length, _ = q.shape
    bl = N_BACK
    n_qblk = min(blocks_per_step, length // bl)
    n_streams = min(blocks_per_step // n_qblk, dil)
    rows, width = n_qblk * bl, n_streams * D_DG
    cur = pl.BlockSpec((None, rows, width), lambda b, r, j: (b, j, r))
    prev = pl.BlockSpec((None, bl, width), lambda b, r, j: (b, jnp.maximum(j * n_qblk - 1, 0), r))
    return pl.pallas_call(
        functools.partial(_dilated_prompt_kernel, n_streams),
        grid=(n_batch, dil // n_streams, length // rows),
        in_specs=[cur, cur, prev, cur, prev],
        out_specs=[cur, cur],
        out_shape=[jax.ShapeDtypeStruct(q.shape, BF16), jax.ShapeDtypeStruct(q.shape, F32)],
        compiler_params=_params("arbitrary", "arbitrary", "arbitrary"),
    )(q, k, k, v, v)


def _dilated_sample_kernel(dil, q_ref, cache_ref, kvnew_ref, newc_ref, o_ref, lse_ref):
    buf_len = cache_ref.shape[1]
    t_new = q_ref.shape[0]
    hw = 2 * HD_D
    n_blk = buf_len // LANES

    seq_in_block = pl.program_id(0) % (LANES // t_new)
    new_tile = pltpu.roll(kvnew_ref[...], (LANES - seq_in_block * t_new) % LANES, 1)

    lane = lax.broadcasted_iota(jnp.int32, (cache_ref.shape[0], LANES), 1)
    cur = pltpu.roll(cache_ref[:, 0:LANES], LANES - t_new, 1)
    for c in range(n_blk):
        following = cache_ref[:, (c + 1) * LANES:(c + 2) * LANES] if c + 1 < n_blk else new_tile
        nxt = pltpu.roll(following, LANES - t_new, 1)
        newc_ref[:, c * LANES:(c + 1) * LANES] = jnp.where(lane < LANES - t_new, cur, nxt)
        cur = nxt

    def iotas(n_keys):
        shape = (2 * t_new, n_keys)
        row = lax.broadcasted_iota(jnp.int32, shape, 0)
        return row & (t_new - 1), lax.broadcasted_iota(jnp.int32, shape, 1)

    qi, col = iotas(buf_len)
    d = buf_len + qi - col
    valid_buf = ((d & (dil - 1)) == 0) & (d <= N_BACK * dil)
    qi, col = iotas(LANES)
    d = qi - col
    valid_new = (col < t_new) & (d >= 0) & ((d & (dil - 1)) == 0)

    q = q_ref[...].astype(BF16)
    lane = lax.broadcasted_iota(jnp.int32, (t_new, hw), 1)
    for p in range(H_DG // 2):
        k_rows, v_rows = slice(p * hw, (p + 1) * hw), slice(D_DG + p * hw, D_DG + (p + 1) * hw)
        qq = _split_heads_stack(q[:, k_rows], HD_D)
        s_buf = jnp.where(valid_buf, _dot(qq, cache_ref[k_rows, :].astype(BF16)), NEG_BIG)
        s_new = jnp.where(valid_new, _dot(qq, new_tile[k_rows].astype(BF16)), NEG_BIG)
        m = jnp.maximum(jnp.max(s_buf, axis=-1, keepdims=True), jnp.max(s_new, axis=-1, keepdims=True))
        p_buf = jnp.where(valid_buf, jnp.exp(s_buf - m), 0.0)
        p_new = jnp.where(valid_new, jnp.exp(s_new - m), 0.0)
        l = jnp.sum(p_buf, axis=-1, keepdims=True) + jnp.sum(p_new, axis=-1, keepdims=True)
        acc = (_dot_nt(p_buf.astype(BF16), cache_ref[v_rows, :].astype(BF16))
               + _dot_nt(p_new.astype(BF16), new_tile[v_rows].astype(BF16)))
        o = acc / l
        lse = m + jnp.log(l)
        o_ref[:, k_rows] = jnp.where(lane < HD_D, o[0:t_new], o[t_new:2 * t_new])
        lse_ref[:, k_rows] = jnp.where(lane < HD_D, lse[0:t_new], lse[t_new:2 * t_new])


def _dilated_sample(q, cache_t, kvnew_t, dil, n_batch, t_new):
    buf_len = cache_t.shape[2]
    seqs_per_block = LANES // t_new
    small = lambda n: pl.BlockSpec((None, t_new, n), lambda b: (b, 0, 0))
    whole = pl.BlockSpec((None, 2 * D_DG, buf_len), lambda b: (b, 0, 0))
    newc, o, lse = pl.pallas_call(
        functools.partial(_dilated_sample_kernel, dil),
        grid=(n_batch,),
        in_specs=[small(D_DG), whole, pl.BlockSpec((2 * D_DG, LANES), lambda b: (0, b // seqs_per_block))],
        out_specs=[whole, small(D_DG), small(D_DG)],
        out_shape=[jax.ShapeDtypeStruct(cache_t.shape, F32),
                   jax.ShapeDtypeStruct((n_batch, t_new, D_DG), F32),
                   jax.ShapeDtypeStruct((n_batch, t_new, D_DG), F32)],
        compiler_params=_params("arbitrary"),
    )(q.reshape(n_batch, t_new, D_DG), cache_t, kvnew_t)
    return newc, o.reshape(n_batch * t_new, D_DG), lse.reshape(n_batch * t_new, D_DG)


def kernel(x_prompt, x_sample, state_a_pool, cache_b_kv, state_c_conv, cache_d0_kv, cache_d1_kv, cache_d2_kv, page_table, norm_mix_g, norm_mlp_g, norm_out_g, w_in_ab, w_out_ab, a_mix, a_scale, b_lam, b_subln_g, w_in_cd, w_out_cd, c_conv_w, w_up, w_down):
    bp, seq, _ = x_prompt.shape
    bs, t_new, _ = x_sample.shape
    n_pages = page_table.shape[1]
    past_len = n_pages * cache_b_kv.shape[2]
    mp, ms = bp * seq, bs * t_new
    tm_p = 512
    xp = x_prompt.reshape(mp, D_MODEL)
    xs = x_sample.reshape(ms, D_MODEL)
    g2 = lambda v: v.reshape(1, -1)
    d_caches = (cache_d0_kv, cache_d1_kv, cache_d2_kv)

    lam_init = 0.8 - 0.6 * math.exp(-0.3 * 0)
    w_in = w_in_ab[0].astype(BF16)
    w_out = w_out_ab[0].astype(BF16)
    wu, wd = w_up[0].astype(BF16), w_down[0].astype(BF16)
    amix = a_mix[0].astype(BF16)
    gm, gl = g2(norm_mix_g[0]), g2(norm_mlp_g[0])
    ascale, sg = g2(a_scale[0]), g2(b_subln_g[0])

    u_p, q_p, kv_p, k_p, v_p = _proj_ab(xp, gm, w_in, tm_p, BF16)
    hist_p = jnp.zeros((bp, POOL_HIST + 1, D_A), F32)
    a_p = _pool_mix(u_p, hist_p, amix, ascale, 0, bp, seq)
    ob_p = _diffattn_prompt(q_p, k_p, v_p, b_lam[0], sg, lam_init, bp, seq, 512)
    xp = _post_mixer(xp, a_p, (ob_p,), (), (), w_out, gl, wu, wd, None, tm_p)

    u_s, q_s, kv_s, _, _ = _proj_ab(xs, gm, w_in, ms, F32)
    hist_s = jnp.pad(state_a_pool[0], ((0, 0), (1, 0), (0, 0)))
    a_s = _pool_mix(u_s, hist_s, amix, ascale, past_len, bs, t_new)
    pool_pages = cache_b_kv[0].reshape(cache_b_kv.shape[1], PAGE_SIZE * 2 * H_B, 2 * HD_B)
    ob_s = _diffattn_decode(page_table, pool_pages, q_s, kv_s, b_lam[0], sg, lam_init, bs, t_new,
                            min(32, n_pages))
    xs = _post_mixer(xs, a_s, (ob_s,), (), (), w_out, gl, wu, wd, None, ms)

    u_p3 = u_p.reshape(bp, seq, D_A)
    new_pool_p = u_p3[:, seq - POOL_HIST:][None]
    new_pool_s = jnp.concatenate([state_a_pool[0], u_s.reshape(bs, t_new, D_A)], axis=1)[:, -POOL_HIST:][None]
    new_bkv_p = kv_p.reshape(1, bp, seq, 2, H_B, 2 * HD_B)
    new_bkv_s = kv_s.reshape(1, bs, t_new, 2, H_B, 2 * HD_B)

    w_in = w_in_cd[0].astype(BF16)
    w_out = w_out_cd[0].astype(BF16)
    wu, wd = w_up[1].astype(BF16), w_down[1].astype(BF16)
    gm, gl, gf = g2(norm_mix_g[1]), g2(norm_mlp_g[1]), g2(norm_out_g)
    cw = c_conv_w[0]
    w_kv_t = jnp.stack([w_in[:, 3 * D_C + (3 * gi + 1) * D_DG:3 * D_C + (3 * gi + 3) * D_DG].T
                        for gi in range(len(D_WINDOWS))])

    def from_position_minor(a):
        a = a.reshape(a.shape[0], 2, H_DG, HD_D, a.shape[2])
        return jnp.transpose(a, (0, 4, 1, 2, 3))[None]

    outs = _proj_cd_prompt(xp, gm, w_in, w_kv_t, cw, bp, seq, tm_p)
    c_p, ztail_p = outs[0], outs[1]
    o_parts, lse_parts, new_d_p = [], [], []
    for gi, (win, dil) in enumerate(D_WINDOWS):
        q_g, k_g, v_g, kvt_g = outs[2 + 4 * gi:6 + 4 * gi]
        o_g, lse_g = _dilated_prompt(q_g, k_g, v_g, dil, 4)
        if dil == 1:
            o_g, lse_g = o_g.reshape(mp, D_DG), lse_g.reshape(mp, D_DG)
        o_parts.append(o_g)
        lse_parts.append(lse_g)
        new_d_p.append(from_position_minor(kvt_g))
    dils = tuple(dil for _, dil in D_WINDOWS)
    y_p = _post_mixer(xp, c_p, o_parts, lse_parts, dils, w_out, gl, wu, wd, gf, tm_p)
    tiles_per_seq = seq // tm_p
    new_conv_p = ztail_p.reshape(bp, tiles_per_seq, SUBLANES, D_C)[:, -1, SUBLANES - (CONV_W - 1):][None]

    hist_rows = jnp.pad(state_c_conv[0], ((0, 0), (t_new - (CONV_W - 1), 0), (0, 0))).reshape(ms, D_C)
    outs = _proj_cd_sample(xs, gm, w_in, w_kv_t, cw, hist_rows)
    c_s, z_s = outs[0], outs[1]
    o_parts, lse_parts, new_d_s = [], [], []
    for gi, (win, dil) in enumerate(D_WINDOWS):
        q_g, kvt_g = outs[2 + 2 * gi:4 + 2 * gi]
        cache = d_caches[gi][0]
        buf_len = cache.shape[1]
        cache_t = jnp.transpose(cache, (0, 2, 3, 4, 1)).reshape(bs, 2 * D_DG, buf_len)
        newc, o_g, lse_g = _dilated_sample(q_g, cache_t, kvt_g, dil, bs, t_new)
        o_parts.append(o_g)
        lse_parts.append(lse_g)
        new_d_s.append(from_position_minor(newc))
    y_s = _post_mixer(xs, c_s, o_parts, lse_parts, (1,) * len(D_WINDOWS), w_out, gl, wu, wd, gf, ms)
    new_conv_s = z_s.reshape(bs, t_new, D_C)[:, t_new - (CONV_W - 1):][None]

    return (y_p.reshape(bp, seq, D_MODEL), y_s.reshape(bs, t_new, D_MODEL),
            new_pool_p, new_pool_s, new_bkv_p, new_bkv_s, new_conv_p, new_conv_s,
            new_d_p[0], new_d_s[0], new_d_p[1], new_d_s[1], new_d_p[2], new_d_s[2])
```

```python
import functools
import math

import jax
import jax.numpy as jnp
from jax import lax
from jax.experimental import pallas as pl
from jax.experimental.pallas import tpu as pltpu

F32 = jnp.float32
BF16 = jnp.bfloat16

D_MODEL = 1024
RMS_EPS = 1e-6
D_A = 512
POOL_WINDOWS = (2, 4, 8, 16)
POOL_GROUP = 128
POOL_HIST = 15
H_B = 4
HD_B = 64
D_B = 512
D_C = 512
CONV_W = 3
D_WINDOWS = ((128, 1), (512, 4), (2048, 16))
N_BACK = 128
H_DG = 8
HD_D = 64
D_DG = 512
D_FF = 4096
PAGE_SIZE = 128

LANES = 128
SUBLANES = 8
VMEM_LIMIT_BYTES = 56 * 1024 * 1024
FUSED_VMEM_LIMIT_BYTES = 60 * 1024 * 1024
NEG_BIG = -1e30
MLP_PAGE_CHUNKS = 4

QK_SCALE_B = HD_B ** -0.5
QK_SCALE_D = HD_D ** -0.5


def _params(*sem):
    return pltpu.CompilerParams(dimension_semantics=sem, vmem_limit_bytes=VMEM_LIMIT_BYTES)


def _resident(shape):
    nd = len(shape)
    return pl.BlockSpec(shape, lambda *_: (0,) * nd, pipeline_mode=pl.Buffered(1))


def _rmsnorm(x, g):
    return x * lax.rsqrt(jnp.mean(x * x, axis=-1, keepdims=True) + RMS_EPS) * g


def _dot(a, b):
    return jnp.dot(a, b, preferred_element_type=F32)


def _dot_nt(a, b):
    return lax.dot_general(a, b, (((1,), (1,)), ((), ())), preferred_element_type=F32)


def _split_heads_stack(q, half):
    lane = lax.broadcasted_iota(jnp.int32, q.shape, 1)
    zero = jnp.zeros_like(q)
    return jnp.concatenate([jnp.where(lane < half, q, zero), jnp.where(lane >= half, q, zero)], axis=0)


def _lane_blocks(n):
    return [slice(c * LANES, (c + 1) * LANES) for c in range(n // LANES)]


def _store_streams(ref, val, dil, scr):
    if dil == 1:
        ref[...] = val.astype(ref.dtype)
        return
    width = val.shape[1]
    for c, sl in enumerate(_lane_blocks(width)):
        scr[c] = val[:, sl]
    for r in range(dil):
        for c, sl in enumerate(_lane_blocks(width)):
            ref[:, r * width + sl.start:r * width + sl.stop] = (
                scr[c, pl.ds(r, ref.shape[0], stride=dil), :].astype(ref.dtype))


def _rows_from_streams(ref, dil, scr):
    width = ref.shape[1] // dil
    if dil == 1:
        return [ref[:, sl].astype(F32) for sl in _lane_blocks(width)]
    for r in range(dil):
        for c, sl in enumerate(_lane_blocks(width)):
            scr[c, pl.ds(r, ref.shape[0], stride=dil), :] = ref[:, r * width + sl.start:r * width + sl.stop].astype(F32)
    return [scr[c] for c in range(width // LANES)]


def _diff_lambda(blam, lam_init):
    a = jnp.sum(blam[0:1] * blam[1:2], axis=-1, keepdims=True)
    b = jnp.sum(blam[2:3] * blam[3:4], axis=-1, keepdims=True)
    return jnp.exp(a) - jnp.exp(b) + lam_init


def _proj_ab_kernel(prompt, x_ref, g_ref, w_ref, wvt_ref, u_ref, q_ref, kv_ref, k_ref, v_ref):
    h = _rmsnorm(x_ref[...], g_ref[...]).astype(BF16)
    u_ref[...] = _dot(h, w_ref[:, 0:D_A])
    q_ref[...] = (_dot(h, w_ref[:, D_A:D_A + D_B]) * QK_SCALE_B).astype(q_ref.dtype)
    kv = _dot(h, w_ref[:, D_A + D_B:D_A + 3 * D_B])
    k_ref[...] = kv[:, :D_B].astype(BF16)
    if not prompt:
        kv_ref[...] = kv
        v_ref[...] = kv[:, D_B:].astype(BF16)
        return
    n_blk = kv.shape[1] // LANES
    for c, sl in enumerate(_lane_blocks(kv.shape[1])):
        kv_ref[pl.ds(c, kv.shape[0], stride=n_blk), :] = kv[:, sl]
    vt = _dot_nt(wvt_ref[...], h).astype(BF16)
    tk = v_ref.shape[2]
    for j in range(v_ref.shape[0]):
        v_ref[j] = vt[:, j * tk:(j + 1) * tk]


def _proj_ab(x, g, w, wv_t, tm, prompt, tk):
    m = x.shape[0]
    n_in = w.shape[1]
    row = lambda n: pl.BlockSpec((tm, n), lambda i: (i, 0))
    n_blk = 2 * D_B // LANES
    if prompt:
        kv_shape, kv_spec = (m * n_blk, LANES), pl.BlockSpec((tm * n_blk, LANES), lambda i: (i, 0))
        v_shape, v_spec = (m // tk, D_B, tk), pl.BlockSpec((tm // tk, D_B, tk), lambda i: (i, 0, 0))
    else:
        kv_shape, kv_spec = (m, 2 * D_B), row(2 * D_B)
        v_shape, v_spec = (m, D_B), row(D_B)
    return pl.pallas_call(
        functools.partial(_proj_ab_kernel, prompt),
        grid=(m // tm,),
        in_specs=[row(D_MODEL), _resident((1, D_MODEL)), _resident((D_MODEL, n_in)), _resident(wv_t.shape)],
        out_specs=[row(D_A), row(D_B), kv_spec, row(D_B), v_spec],
        out_shape=[jax.ShapeDtypeStruct((m, D_A), F32), jax.ShapeDtypeStruct((m, D_B), BF16 if prompt else F32),
                   jax.ShapeDtypeStruct(kv_shape, F32), jax.ShapeDtypeStruct((m, D_B), BF16),
                   jax.ShapeDtypeStruct(v_shape, BF16)],
        compiler_params=_params("arbitrary"),
    )(x, g, w, wv_t)


def _pool_kernel(pos0, tchunk, u_ref, hist_ref, amix_ref, ascale_ref, o_ref, ext_ref):
    t_len = u_ref.shape[0]
    hpad = hist_ref.shape[0]
    ext_ref[0:hpad, :] = hist_ref[...]
    ext_ref[hpad:hpad + t_len, :] = u_ref[...]
    for t0 in range(0, t_len, tchunk):
        pos = pos0 + t0 + lax.broadcasted_iota(jnp.int32, (tchunk, 1), 0)
        for gi, w in enumerate(POOL_WINDOWS):
            sl = slice(gi * POOL_GROUP, (gi + 1) * POOL_GROUP)
            u = u_ref[t0:t0 + tchunk, sl]
            acc = u
            for j in range(1, w):
                acc = acc + ext_ref[hpad + t0 - j:hpad + t0 - j + tchunk, sl]
            cnt = jnp.minimum(pos + 1, w).astype(F32)
            p = (acc / cnt - u).astype(BF16)
            y = _dot(p, amix_ref[gi]) * ascale_ref[:, sl]
            o_ref[t0:t0 + tchunk, sl] = y.astype(o_ref.dtype)


def _pool_mix(u, hist16, amix, ascale, pos0, n_seq, t_len):
    hpad = hist16.shape[1]
    tchunk = min(t_len, 256)
    out = pl.pallas_call(
        functools.partial(_pool_kernel, pos0, tchunk),
        grid=(n_seq,),
        in_specs=[pl.BlockSpec((None, t_len, D_A), lambda s: (s, 0, 0)),
                  pl.BlockSpec((None, hpad, D_A), lambda s: (s, 0, 0)),
                  _resident(amix.shape), _resident((1, D_A))],
        out_specs=pl.BlockSpec((None, t_len, D_A), lambda s: (s, 0, 0)),
        out_shape=jax.ShapeDtypeStruct((n_seq, t_len, D_A), BF16 if t_len % 16 == 0 else F32),
        scratch_shapes=[pltpu.VMEM((hpad + t_len, D_A), F32)],
        compiler_params=_params("arbitrary"),
    )(u.reshape(n_seq, t_len, D_A), hist16, amix, ascale)
    return out.reshape(n_seq * t_len, D_A)


def _diff_finalize(acc, l, lam, sg, lam_init, t):
    o = acc[0:t] / l[0:t] - lam * (acc[t:2 * t] / l[t:2 * t])
    o = o * lax.rsqrt(jnp.mean(o * o, axis=-1, keepdims=True) + RMS_EPS) * sg
    return o * (1.0 - lam_init)


def _diffattn_prompt_kernel(lam_init, tc, q_ref, k_ref, vt_ref, blam_ref, sgt_ref, o_ref, m_scr, l_scr, acc_scr):
    seq = q_ref.shape[0]
    tk = vt_ref.shape[2]
    hw = 2 * HD_B
    heads = [slice(h * hw, (h + 1) * hw) for h in range(H_B)]
    lam = _diff_lambda(blam_ref[...], lam_init)
    sgt = sgt_ref[...]
    shape = (tk, 2 * tc)
    key = lax.broadcasted_iota(jnp.int32, shape, 0)
    col = lax.broadcasted_iota(jnp.int32, shape, 1)
    qcol = jnp.where(col >= tc, col - tc, col)

    def scores(j, h, qq):
        off = pl.multiple_of(j * tk, tk)
        return _dot_nt(k_ref[pl.ds(off, tk), heads[h]], qq)

    def chunk(c, carry):
        q0 = pl.multiple_of(c * tc, tc)
        jd = q0 // tk
        qqs = [_split_heads_stack(q_ref[pl.ds(q0, tc), sl], HD_B) for sl in heads]

        causal = (jd * tk + key) <= (q0 + qcol)
        ss = [scores(jd, h, qqs[h]) for h in range(H_B)]
        ps = []
        for h in range(H_B):
            s = jnp.where(causal, ss[h], -jnp.inf)
            m = jnp.max(s, axis=0, keepdims=True)
            p = jnp.exp(s - m)
            m_scr[h] = m
            l_scr[h] = jnp.sum(p, axis=0, keepdims=True)
            ps.append(p.astype(BF16))
        for h, sl in enumerate(heads):
            acc_scr[h] = _dot(vt_ref[jd, sl, :], ps[h])

        def body(j, inner):
            ss = [scores(j, h, qqs[h]) for h in range(H_B)]
            ps = []
            for h in range(H_B):
                m_prev = m_scr[h]
                m_new = jnp.maximum(m_prev, jnp.max(ss[h], axis=0, keepdims=True))
                alpha = jnp.exp(m_prev - m_new)
                p = jnp.exp(ss[h] - m_new)
                l_scr[h] = alpha * l_scr[h] + jnp.sum(p, axis=0, keepdims=True)
                m_scr[h] = m_new
                ps.append((alpha, p.astype(BF16)))
            for h, sl in enumerate(heads):
                alpha, p = ps[h]
                acc_scr[h] = alpha * acc_scr[h] + _dot(vt_ref[j, sl, :], p)
            return inner

        lax.fori_loop(0, jd, body, 0)
        for h, sl in enumerate(heads):
            o = acc_scr[h] / l_scr[h]
            o = o[:, 0:tc] - lam * o[:, tc:2 * tc]
            o = o * lax.rsqrt(jnp.mean(o * o, axis=0, keepdims=True) + RMS_EPS) * sgt * (1.0 - lam_init)
            o_ref[pl.ds(q0, tc), sl] = o.T.astype(o_ref.dtype)
        return carry

    lax.fori_loop(0, seq // tc, chunk, 0)


def _diffattn_prompt(q, k, vt, blam, sg, lam_init, n_batch, seq, tc):
    hw = 2 * HD_B
    tk = vt.shape[2]
    whole = pl.BlockSpec((seq, D_B), lambda b: (b, 0))
    return pl.pallas_call(
        functools.partial(_diffattn_prompt_kernel, lam_init, tc),
        grid=(n_batch,),
        in_specs=[whole, whole, pl.BlockSpec((seq // tk, D_B, tk), lambda b: (b, 0, 0)),
                  _resident(blam.shape), _resident((hw, 1))],
        out_specs=whole,
        out_shape=jax.ShapeDtypeStruct((n_batch * seq, D_B), BF16),
        scratch_shapes=[pltpu.VMEM((H_B, 1, 2 * tc), F32), pltpu.VMEM((H_B, 1, 2 * tc), F32),
                        pltpu.VMEM((H_B, hw, 2 * tc), F32)],
        compiler_params=_params("arbitrary"),
    )(q, k, vt, blam, sg.reshape(hw, 1))


def _online_softmax(rows, scores, valid, m_scr, l_scr):
    steps = []
    for r, s in zip(rows, scores):
        m_prev = m_scr[r]
        m_new = jnp.maximum(m_prev, jnp.max(s, axis=-1, keepdims=True))
        alpha = jnp.exp(m_prev - m_new)
        p = jnp.exp(s - m_new)
        if valid is not None:
            p = jnp.where(valid, p, 0.0)
        l_scr[r] = alpha * l_scr[r] + jnp.sum(p, axis=-1, keepdims=True)
        m_scr[r] = m_new
        steps.append((alpha, p.astype(BF16)))
    return steps


def _online_accumulate(rows, steps, values, acc_scr):
    for r, v, (alpha, p) in zip(rows, values, steps):
        acc_scr[r] = alpha * acc_scr[r] + _dot(p, v)


def _online_update(rows, scores, values, valid, m_scr, l_scr, acc_scr):
    _online_accumulate(rows, _online_softmax(rows, scores, valid, m_scr, l_scr), values, acc_scr)


def _init_online(m_scr, l_scr, acc_scr):
    m_scr[...] = jnp.full(m_scr.shape, NEG_BIG, F32)
    l_scr[...] = jnp.zeros(l_scr.shape, F32)
    acc_scr[...] = jnp.zeros(acc_scr.shape, F32)


def _decode_queries(q_ref):
    t_new = q_ref.shape[0]
    hw = 2 * HD_B
    q = q_ref[...].astype(BF16)
    qs = [_split_heads_stack(q[:, h * hw:(h + 1) * hw], HD_B) for h in range(H_B)]
    return qs, [slice(2 * t_new * h, 2 * t_new * (h + 1)) for h in range(H_B)]


def _decode_page_heads(page_refs, first_row):
    return jnp.concatenate([r[pl.ds(first_row, PAGE_SIZE, stride=2 * H_B), :] for r in page_refs],
                           axis=0).astype(BF16)


def _decode_finish(lam_init, qs, head_rows, kvnew_ref, blam_ref, sg_ref, o_ref, m_scr, l_scr, acc_scr, pad_scr):
    t_new = kvnew_ref.shape[0]
    hw = 2 * HD_B
    pad_scr[...] = jnp.zeros(pad_scr.shape, F32)
    pad_scr[0:t_new, :] = kvnew_ref[...]
    shape = (2 * t_new, pad_scr.shape[0])
    row = lax.broadcasted_iota(jnp.int32, shape, 0)
    col = lax.broadcasted_iota(jnp.int32, shape, 1)
    valid = col <= jnp.where(row >= t_new, row - t_new, row)
    scores = [jnp.where(valid, _dot_nt(qs[h], pad_scr[:, h * hw:(h + 1) * hw].astype(BF16)), NEG_BIG)
              for h in range(H_B)]
    values = [pad_scr[:, D_B + h * hw:D_B + (h + 1) * hw].astype(BF16) for h in range(H_B)]
    _online_update(head_rows, scores, values, valid, m_scr, l_scr, acc_scr)
    lam = _diff_lambda(blam_ref[...], lam_init)
    for h, rows in enumerate(head_rows):
        o = _diff_finalize(acc_scr[rows], l_scr[rows], lam, sg_ref[...], lam_init, t_new)
        o_ref[:, h * hw:(h + 1) * hw] = o.astype(o_ref.dtype)


def _diffattn_decode_kernel(pages_per_step, lam_init, pt_ref, q_ref, kvnew_ref, blam_ref, sg_ref, *rest):
    page_refs = rest[:pages_per_step]
    o_ref, m_scr, l_scr, acc_scr, pad_scr = rest[pages_per_step:]
    j = pl.program_id(1)
    qs, head_rows = _decode_queries(q_ref)

    @pl.when(j == 0)
    def _():
        _init_online(m_scr, l_scr, acc_scr)

    scores = [_dot_nt(qs[h], _decode_page_heads(page_refs, h)) for h in range(H_B)]
    values = [_decode_page_heads(page_refs, H_B + h) for h in range(H_B)]
    _online_update(head_rows, scores, values, None, m_scr, l_scr, acc_scr)

    @pl.when(j == pl.num_programs(1) - 1)
    def _():
        _decode_finish(lam_init, qs, head_rows, kvnew_ref, blam_ref, sg_ref, o_ref, m_scr, l_scr, acc_scr, pad_scr)


def _post_decode_kernel(pages_per_step, lam_init, pt_ref, x_ref, a_ref, b_ref, wout_ref, g_ref, wup_ref, wdn_ref,
                        q_ref, kvnew_ref, blam_ref, sg_ref, *rest):
    page_refs = rest[:pages_per_step]
    o_ref, os_ref, h_scr, mlp_scr, m_scr, l_scr, acc_scr, pad_scr = rest[pages_per_step:]
    j = pl.program_id(1)
    qs, head_rows = _decode_queries(q_ref)

    @pl.when(j == 0)
    def _():
        mix = jnp.concatenate([a_ref[...].astype(BF16), b_ref[...].astype(BF16)], axis=-1)
        x1 = x_ref[...] + _dot(mix, wout_ref[...])
        h_scr[...] = _rmsnorm(x1, g_ref[...]).astype(BF16)
        mlp_scr[...] = x1
        _init_online(m_scr, l_scr, acc_scr)

    up = _dot(h_scr[...], wup_ref[j])
    scores = [_dot_nt(qs[h], _decode_page_heads(page_refs, h)) for h in range(H_B)]
    steps = _online_softmax(head_rows, scores, None, m_scr, l_scr)
    act = jnp.square(jnp.maximum(up, 0.0)).astype(BF16)
    mlp_scr[...] += _dot(act, wdn_ref[j])
    _online_accumulate(head_rows, steps, [_decode_page_heads(page_refs, H_B + h) for h in range(H_B)], acc_scr)

    @pl.when(j == pl.num_programs(1) - 1)
    def _():
        o_ref[...] = mlp_scr[...]
        _decode_finish(lam_init, qs, head_rows, kvnew_ref, blam_ref, sg_ref, os_ref, m_scr, l_scr, acc_scr, pad_scr)


def _post_mixer_decode(x, a, b, wout, g, wup, wdn, page_table, cache, q, kvnew, blam, sg, lam_init, tm, t_new):
    m = x.shape[0]
    n_tiles = m // tm
    n_pages = page_table.shape[1]
    n_chunks = wup.shape[0]
    pages_per_step = n_pages // n_chunks
    assert page_table.shape[0] == n_tiles and n_pages % n_chunks == 0
    hw = 2 * HD_B
    row = lambda n: pl.BlockSpec((tm, n), lambda i, j, pt: (i, 0))
    const = lambda shape: pl.BlockSpec(shape, lambda i, j, pt: (0,) * len(shape), pipeline_mode=pl.Buffered(1))
    small = lambda n: pl.BlockSpec((None, t_new, n), lambda i, j, pt: (i, 0, 0))
    page_specs = [
        pl.BlockSpec((None, PAGE_SIZE * 2 * H_B, hw), functools.partial(
            lambda i, j, pt, t: (pt[i, j * pages_per_step + t], 0, 0), t=t))
        for t in range(pages_per_step)]
    grid_spec = pltpu.PrefetchScalarGridSpec(
        num_scalar_prefetch=1,
        grid=(n_tiles, n_chunks),
        in_specs=[row(D_MODEL), row(a.shape[1]), row(b.shape[1]), const(wout.shape), const((1, D_MODEL)),
                  const(wup.shape), const(wdn.shape), small(D_B), small(2 * D_B), const(blam.shape),
                  const((1, hw))] + page_specs,
        out_specs=[row(D_MODEL), small(D_B)],
        scratch_shapes=[pltpu.VMEM((tm, D_MODEL), BF16), pltpu.VMEM((tm, D_MODEL), F32),
                        pltpu.VMEM((2 * t_new * H_B, 1), F32), pltpu.VMEM((2 * t_new * H_B, 1), F32),
                        pltpu.VMEM((2 * t_new * H_B, hw), F32), pltpu.VMEM((PAGE_SIZE, 2 * D_B), F32)])
    xo, os_ = pl.pallas_call(
        functools.partial(_post_decode_kernel, pages_per_step, lam_init),
        grid_spec=grid_spec,
        out_shape=[jax.ShapeDtypeStruct((m, D_MODEL), F32), jax.ShapeDtypeStruct((n_tiles, t_new, D_B), F32)],
        compiler_params=pltpu.CompilerParams(dimension_semantics=("arbitrary", "arbitrary"),
                                             vmem_limit_bytes=FUSED_VMEM_LIMIT_BYTES),
    )(page_table, x, a, b, wout, g, wup, wdn, q.reshape(n_tiles, t_new, D_B),
      kvnew.reshape(n_tiles, t_new, 2 * D_B), blam, sg, *([cache] * pages_per_step))
    return xo, os_.reshape(n_tiles * t_new, D_B)


def _diffattn_decode(page_table, cache, q, kvnew, blam, sg, lam_init, n_batch, t_new, pages_per_step):
    n_pages = page_table.shape[1]
    hw = 2 * HD_B
    page_specs = [
        pl.BlockSpec((None, PAGE_SIZE * 2 * H_B, hw), functools.partial(
            lambda b, j, pt, t: (pt[b, j * pages_per_step + t], 0, 0), t=t))
        for t in range(pages_per_step)]
    grid_spec = pltpu.PrefetchScalarGridSpec(
        num_scalar_prefetch=1,
        grid=(n_batch, n_pages // pages_per_step),
        in_specs=[pl.BlockSpec((None, t_new, D_B), lambda b, j, pt: (b, 0, 0)),
                  pl.BlockSpec((None, t_new, 2 * D_B), lambda b, j, pt: (b, 0, 0)),
                  pl.BlockSpec(blam.shape, lambda b, j, pt: (0, 0)),
                  pl.BlockSpec((1, hw), lambda b, j, pt: (0, 0))] + page_specs,
        out_specs=pl.BlockSpec((None, t_new, D_B), lambda b, j, pt: (b, 0, 0)),
        scratch_shapes=[pltpu.VMEM((2 * t_new * H_B, 1), F32), pltpu.VMEM((2 * t_new * H_B, 1), F32),
                        pltpu.VMEM((2 * t_new * H_B, hw), F32), pltpu.VMEM((PAGE_SIZE, 2 * D_B), F32)])
    out = pl.pallas_call(
        functools.partial(_diffattn_decode_kernel, pages_per_step, lam_init),
        grid_spec=grid_spec,
        out_shape=jax.ShapeDtypeStruct((n_batch, t_new, D_B), F32),
        compiler_params=_params("arbitrary", "arbitrary"),
    )(page_table, q.reshape(n_batch, t_new, D_B), kvnew.reshape(n_batch, t_new, 2 * D_B), blam, sg,
      *([cache] * pages_per_step))
    return out.reshape(n_batch * t_new, D_B)


def _post_kernel(dils, ff_chunk, has_final, x_ref, first_ref, *rest):
    n_g = max(len(dils), 1)
    o_refs = rest[:n_g]
    lse_refs = rest[n_g:n_g + len(dils)]
    rest = rest[n_g + len(dils):]
    wout_ref, g_ref, wup_ref, wdn_ref = rest[:4]
    rest = rest[4:]
    gf_ref = rest[0] if has_final else None
    o_ref = rest[1] if has_final else rest[0]
    scratch = list(rest[2 if has_final else 1:])
    if not dils:
        second = o_refs[0][...].astype(BF16)
    else:
        scr_of = lambda d: scratch.pop(0) if d > 1 else None
        os_ = [_rows_from_streams(r, d, scr_of(d)) for r, d in zip(o_refs, dils)]
        ls_ = [_rows_from_streams(r, d, scr_of(d)) for r, d in zip(lse_refs, dils)]
        merged = []
        for c in range(len(os_[0])):
            lses = [l[c] for l in ls_]
            mx = functools.reduce(jnp.maximum, lses)
            es = [jnp.exp(l - mx) for l in lses]
            num = functools.reduce(lambda a, b: a + b, [e * o[c] for e, o in zip(es, os_)])
            merged.append((num / functools.reduce(lambda a, b: a + b, es)).astype(BF16))
        second = jnp.concatenate(merged, axis=-1)
    mix = jnp.concatenate([first_ref[...].astype(BF16), second], axis=-1)
    x1 = x_ref[...] + _dot(mix, wout_ref[...])
    h = _rmsnorm(x1, g_ref[...]).astype(BF16)
    acc = x1
    up = _dot(h, wup_ref[:, 0:ff_chunk])
    for c in range(0, D_FF, ff_chunk):
        nxt = _dot(h, wup_ref[:, c + ff_chunk:c + 2 * ff_chunk]) if c + ff_chunk < D_FF else None
        act = jnp.square(jnp.maximum(up, 0.0)).astype(BF16)
        acc = acc + _dot(act, wdn_ref[c:c + ff_chunk, :])
        up = nxt
    if has_final:
        acc = _rmsnorm(acc, gf_ref[...])
    o_ref[...] = acc


def _post_mixer(x, first, o_parts, lse_parts, dils, wout, g, wup, wdn, gf, tm):
    m = x.shape[0]
    row = lambda n: pl.BlockSpec((tm, n), lambda i: (i, 0))

    def part_spec(a, dil):
        if dil == 1:
            return row(a.shape[1])
        tiles_per_seq = a.shape[1] * dil // tm
        return pl.BlockSpec((None, tm // dil, a.shape[2]), lambda i: (i // tiles_per_seq, i % tiles_per_seq, 0))

    has_final = gf is not None
    part_dils = tuple(dils) if dils else (1,)
    args = [x, first, *o_parts, *lse_parts, wout, g, wup, wdn]
    specs = [row(D_MODEL), row(first.shape[1])]
    specs += [part_spec(a, d) for a, d in zip(o_parts, part_dils)] + [part_spec(a, d) for a, d in zip(lse_parts, dils)]
    specs += [_resident(wout.shape), _resident((1, D_MODEL)), _resident(wup.shape), _resident(wdn.shape)]
    if has_final:
        args.append(gf)
        specs.append(_resident((1, D_MODEL)))
    n_scratch = 2 * sum(d > 1 for d in dils)
    return pl.pallas_call(
        functools.partial(_post_kernel, tuple(dils), 1024, has_final),
        grid=(m // tm,),
        in_specs=specs,
        out_specs=row(D_MODEL),
        out_shape=jax.ShapeDtypeStruct((m, D_MODEL), F32),
        scratch_shapes=[pltpu.VMEM((D_DG // LANES, tm, LANES), F32)] * n_scratch,
        compiler_params=_params("arbitrary"),
    )(*args)


def _proj_cd_kernel(tiles_per_seq, tails, x_ref, g_ref, w_ref, wt_ref, cw_ref, hist_ref, *rest):
    n_g = len(D_WINDOWS)
    c_ref, ztail_ref = rest[0], rest[1]
    group_refs = rest[2:2 + 4 * n_g] if tiles_per_seq else rest[2:2 + 2 * n_g]
    carry_ref, stream_scr = rest[-2], rest[-1]
    tm = x_ref.shape[0]
    i = pl.program_id(0)
    h = _rmsnorm(x_ref[...], g_ref[...]).astype(BF16)
    gates = _dot(h, w_ref[:, 0:3 * D_C])
    b_gate = gates[:, 0:D_C]
    z = gates[:, D_C:2 * D_C] * gates[:, 2 * D_C:3 * D_C]
    row = lax.broadcasted_iota(jnp.int32, z.shape, 0)
    if tiles_per_seq:
        first = (i % tiles_per_seq) == 0
        prev = jnp.where(first, hist_ref[...], carry_ref[...])
        z1 = jnp.where(row == 0, prev[7:8], pltpu.roll(z, 1, 0))
        z2 = jnp.where(row == 0, prev[6:7], jnp.where(row == 1, prev[7:8], pltpu.roll(z, 2, 0)))
        carry_ref[...] = z[tm - SUBLANES:tm]
        ztail_ref[...] = z[tm - SUBLANES:tm]
    else:
        e = hist_ref[...]
        t = row & (SUBLANES - 1)
        z1 = jnp.where(t == 0, pltpu.roll(e, tm - 7, 0), pltpu.roll(z, 1, 0))
        z2 = jnp.where(t < 2, pltpu.roll(e, tm - 6, 0), pltpu.roll(z, 2, 0))
        ztail_ref[...] = z
    cw = cw_ref[...]
    cv = z2 * cw[0:1] + z1 * cw[1:2] + z * cw[2:3]
    c_ref[...] = (b_gate * cv).astype(c_ref.dtype)
    for gi in range(n_g):
        off = 3 * D_C + 3 * gi * D_DG
        if tiles_per_seq:
            q_ref, k_ref, v_ref, kvt_ref = group_refs[4 * gi:4 * gi + 4]
            res = _dot(h, w_ref[:, off:off + 3 * D_DG])
            dil = D_WINDOWS[gi][1]
            _store_streams(q_ref, res[:, 0:D_DG] * QK_SCALE_D, dil, stream_scr)
            _store_streams(k_ref, res[:, D_DG:2 * D_DG], dil, stream_scr)
            _store_streams(v_ref, res[:, 2 * D_DG:3 * D_DG], dil, stream_scr)
        else:
            q_ref, kvt_ref = group_refs[2 * gi:2 * gi + 2]
            q_ref[...] = _dot(h, w_ref[:, off:off + D_DG]) * QK_SCALE_D
            kvt_ref[...] = _dot_nt(wt_ref[gi], h)
    if tiles_per_seq:
        for gi in range(n_g):
            kvt_ref = group_refs[4 * gi + 3]
            tail, first_kept = tails[gi]
            if first_kept == 0:
                kvt_ref[...] = _dot_nt(wt_ref[gi], h[tm - tail:tm])
            else:
                @pl.when(i % tiles_per_seq >= first_kept)
                def _(gi=gi, kvt_ref=kvt_ref, tail=tail):
                    kvt_ref[...] = _dot_nt(wt_ref[gi], h[tm - tail:tm])


def _proj_cd_prompt(x, g, w, wt, cw, n_batch, seq, tm):
    m = x.shape[0]
    tps = seq // tm
    row = lambda n, dt=None: pl.BlockSpec((tm, n), lambda i: (i, 0))
    hist = jnp.zeros((n_batch * SUBLANES, D_C), F32)
    out_shape = [jax.ShapeDtypeStruct((m, D_C), BF16), jax.ShapeDtypeStruct((m // tm * SUBLANES, D_C), F32)]
    out_specs = [row(D_C), pl.BlockSpec((SUBLANES, D_C), lambda i: (i, 0))]
    tails = []
    for win, dil in D_WINDOWS:
        keep = min(win, seq)
        tail = min(keep, tm)
        first_kept = tps - keep // tail
        tails.append((tail, first_kept))
        out_shape += [jax.ShapeDtypeStruct((n_batch, seq // dil, dil * D_DG), BF16)] * 3
        out_shape += [jax.ShapeDtypeStruct((n_batch, 2 * D_DG, keep), F32)]
        out_specs += [pl.BlockSpec((None, tm // dil, dil * D_DG), lambda i: (i // tps, i % tps, 0))] * 3
        out_specs += [pl.BlockSpec((None, 2 * D_DG, tail), functools.partial(
            lambda i, fk: (i // tps, 0, jnp.maximum(i % tps - fk, 0)), fk=first_kept))]
    return pl.pallas_call(
        functools.partial(_proj_cd_kernel, tps, tuple(tails)),
        grid=(m // tm,),
        in_specs=[row(D_MODEL), _resident((1, D_MODEL)), _resident(w.shape), _resident(wt.shape),
                  _resident(cw.shape), pl.BlockSpec((SUBLANES, D_C), lambda i: (i // tps, 0))],
        out_specs=out_specs,
        out_shape=out_shape,
        scratch_shapes=[pltpu.VMEM((SUBLANES, D_C), F32), pltpu.VMEM((D_DG // LANES, tm, LANES), F32)],
        compiler_params=_params("arbitrary"),
    )(x, g, w, wt, cw, hist)


def _proj_cd_sample(x, g, w, wt, cw, hist_rows):
    m = x.shape[0]
    full = lambda r, n: pl.BlockSpec((r, n), lambda i: (0, 0))
    out_shape = [jax.ShapeDtypeStruct((m, D_C), F32), jax.ShapeDtypeStruct((m, D_C), F32)]
    out_specs = [full(m, D_C), full(m, D_C)]
    for _ in D_WINDOWS:
        out_shape += [jax.ShapeDtypeStruct((m, D_DG), F32), jax.ShapeDtypeStruct((2 * D_DG, m), F32)]
        out_specs += [full(m, D_DG), full(2 * D_DG, m)]
    return pl.pallas_call(
        functools.partial(_proj_cd_kernel, 0, ()),
        grid=(1,),
        in_specs=[full(m, D_MODEL), _resident((1, D_MODEL)), _resident(w.shape), _resident(wt.shape),
                  _resident(cw.shape), full(m, D_C)],
        out_specs=out_specs,
        out_shape=out_shape,
        scratch_shapes=[pltpu.VMEM((SUBLANES, D_C), F32), pltpu.VMEM((D_DG // LANES, SUBLANES, LANES), F32)],
        compiler_params=_params("arbitrary"),
    )(x, g, w, wt, cw, hist_rows)


def _dilated_prompt_kernel(n_streams, q_ref, kc_ref, kp_ref, vc_ref, vp_ref, o_ref, lse_ref):
    bl = N_BACK
    n_qblk = q_ref.shape[0] // bl
    first_blk = pl.program_id(2) * n_qblk
    shape = (bl, 2 * bl)
    row = lax.broadcasted_iota(jnp.int32, shape, 0)
    col = lax.broadcasted_iota(jnp.int32, shape, 1)
    dist = row - col + bl
    band = (dist >= 0) & (dist <= N_BACK)
    hw = 2 * HD_D
    lane = lax.broadcasted_iota(jnp.int32, (bl, hw), 1)
    for r in range(n_streams):
        for qb in range(n_qblk):
            rows = slice(qb * bl, (qb + 1) * bl)
            prev_rows = slice((qb - 1) * bl, qb * bl)
            valid = (band & ((first_blk * bl + col - bl) >= 0)) if qb == 0 else band
            pairs = [slice(r * D_DG + p * hw, r * D_DG + (p + 1) * hw) for p in range(H_DG // 2)]
            scores = []
            for sl in pairs:
                qq = _split_heads_stack(q_ref[rows, sl], HD_D)
                k_prev = kp_ref[:, sl] if qb == 0 else kc_ref[prev_rows, sl]
                kcat = jnp.concatenate([k_prev, kc_ref[rows, sl]], axis=0)
                scores.append([_dot_nt(qq[e * bl:(e + 1) * bl], kcat) for e in range(2)])
            probs = []
            for pair_scores in scores:
                stats = []
                for s in pair_scores:
                    s = jnp.where(valid, s, -jnp.inf)
                    m = jnp.max(s, axis=-1, keepdims=True)
                    pr = jnp.exp(s - m)
                    l = jnp.sum(pr, axis=-1, keepdims=True)
                    stats.append((pr.astype(BF16), l, m + jnp.log(l)))
                probs.append(stats)
            for sl, stats in zip(pairs, probs):
                v_prev = vp_ref[:, sl] if qb == 0 else vc_ref[prev_rows, sl]
                vcat = jnp.concatenate([v_prev, vc_ref[rows, sl]], axis=0)
                outs = [_dot(pr, vcat) / l for pr, l, _ in stats]
                o_ref[rows, sl] = jnp.where(lane < HD_D, outs[0], outs[1]).astype(o_ref.dtype)
                lse_ref[rows, sl] = jnp.where(lane < HD_D, stats[0][2], stats[1][2])


def _dilated_prompt(q, k, v, dil, blocks_per_step):
    n_batch, length, _ = q.shape
    bl = N_BACK
    n_qblk = min(blocks_per_step, length // bl)
    n_streams = min(blocks_per_step // n_qblk, dil)
    rows, width = n_qblk * bl, n_streams * D_DG
    cur = pl.BlockSpec((None, rows, width), lambda b, r, j: (b, j, r))
    prev = pl.BlockSpec((None, bl, width), lambda b, r, j: (b, jnp.maximum(j * n_qblk - 1, 0), r))
    return pl.pallas_call(
        functools.partial(_dilated_prompt_kernel, n_streams),
        grid=(n_batch, dil // n_streams, length // rows),
        in_specs=[cur, cur, prev, cur, prev],
        out_specs=[cur, cur],
        out_shape=[jax.ShapeDtypeStruct(q.shape, BF16), jax.ShapeDtypeStruct(q.shape, F32)],
        compiler_params=_params("arbitrary", "arbitrary", "arbitrary"),
    )(q, k, k, v, v)


def _dilated_sample_kernel(dil, q_ref, cache_ref, kvnew_ref, newc_ref, o_ref, lse_ref):
    buf_len = cache_ref.shape[1]
    t_new = q_ref.shape[0]
    hw = 2 * HD_D
    n_blk = buf_len // LANES

    seq_in_block = pl.program_id(0) % (LANES // t_new)
    new_tile = pltpu.roll(kvnew_ref[...], (LANES - seq_in_block * t_new) % LANES, 1)

    lane = lax.broadcasted_iota(jnp.int32, (cache_ref.shape[0], LANES), 1)
    cur = pltpu.roll(cache_ref[:, 0:LANES], LANES - t_new, 1)
    for c in range(n_blk):
        following = cache_ref[:, (c + 1) * LANES:(c + 2) * LANES] if c + 1 < n_blk else new_tile
        nxt = pltpu.roll(following, LANES - t_new, 1)
        newc_ref[:, c * LANES:(c + 1) * LANES] = jnp.where(lane < LANES - t_new, cur, nxt)
        cur = nxt

    def iotas(n_keys):
        shape = (2 * t_new, n_keys)
        row = lax.broadcasted_iota(jnp.int32, shape, 0)
        return row & (t_new - 1), lax.broadcasted_iota(jnp.int32, shape, 1)

    qi, col = iotas(buf_len)
    d = buf_len + qi - col
    valid_buf = ((d & (dil - 1)) == 0) & (d <= N_BACK * dil)
    qi, col = iotas(LANES)
    d = qi - col
    valid_new = (col < t_new) & (d >= 0) & ((d & (dil - 1)) == 0)

    q = q_ref[...].astype(BF16)
    lane = lax.broadcasted_iota(jnp.int32, (t_new, hw), 1)
    for p in range(H_DG // 2):
        k_rows, v_rows = slice(p * hw, (p + 1) * hw), slice(D_DG + p * hw, D_DG + (p + 1) * hw)
        qq = _split_heads_stack(q[:, k_rows], HD_D)
        s_buf = jnp.where(valid_buf, _dot(qq, cache_ref[k_rows, :].astype(BF16)), NEG_BIG)
        s_new = jnp.where(valid_new, _dot(qq, new_tile[k_rows].astype(BF16)), NEG_BIG)
        m = jnp.maximum(jnp.max(s_buf, axis=-1, keepdims=True), jnp.max(s_new, axis=-1, keepdims=True))
        p_buf = jnp.where(valid_buf, jnp.exp(s_buf - m), 0.0)
        p_new = jnp.where(valid_new, jnp.exp(s_new - m), 0.0)
        l = jnp.sum(p_buf, axis=-1, keepdims=True) + jnp.sum(p_new, axis=-1, keepdims=True)
        acc = (_dot_nt(p_buf.astype(BF16), cache_ref[v_rows, :].astype(BF16))
               + _dot_nt(p_new.astype(BF16), new_tile[v_rows].astype(BF16)))
        o = acc / l
        lse = m + jnp.log(l)
        o_ref[:, k_rows] = jnp.where(lane < HD_D, o[0:t_new], o[t_new:2 * t_new])
        lse_ref[:, k_rows] = jnp.where(lane < HD_D, lse[0:t_new], lse[t_new:2 * t_new])


def _dilated_sample(q, cache_t, kvnew_t, dil, n_batch, t_new):
    buf_len = cache_t.shape[2]
    seqs_per_block = LANES // t_new
    small = lambda n: pl.BlockSpec((None, t_new, n), lambda b: (b, 0, 0))
    whole = pl.BlockSpec((None, 2 * D_DG, buf_len), lambda b: (b, 0, 0))
    newc, o, lse = pl.pallas_call(
        functools.partial(_dilated_sample_kernel, dil),
        grid=(n_batch,),
        in_specs=[small(D_DG), whole, pl.BlockSpec((2 * D_DG, LANES), lambda b: (0, b // seqs_per_block))],
        out_specs=[whole, small(D_DG), small(D_DG)],
        out_shape=[jax.ShapeDtypeStruct(cache_t.shape, F32),
                   jax.ShapeDtypeStruct((n_batch, t_new, D_DG), F32),
                   jax.ShapeDtypeStruct((n_batch, t_new, D_DG), F32)],
        compiler_params=_params("arbitrary"),
    )(q.reshape(n_batch, t_new, D_DG), cache_t, kvnew_t)
    return newc, o.reshape(n_batch * t_new, D_DG), lse.reshape(n_batch * t_new, D_DG)


def kernel(x_prompt, x_sample, state_a_pool, cache_b_kv, state_c_conv, cache_d0_kv, cache_d1_kv, cache_d2_kv, page_table, norm_mix_g, norm_mlp_g, norm_out_g, w_in_ab, w_out_ab, a_mix, a_scale, b_lam, b_subln_g, w_in_cd, w_out_cd, c_conv_w, w_up, w_down):
    bp, seq, _ = x_prompt.shape
    bs, t_new, _ = x_sample.shape
    n_pages = page_table.shape[1]
    past_len = n_pages * cache_b_kv.shape[2]
    mp, ms = bp * seq, bs * t_new
    tm_p = 512
    xp = x_prompt.reshape(mp, D_MODEL)
    xs = x_sample.reshape(ms, D_MODEL)
    g2 = lambda v: v.reshape(1, -1)
    d_caches = (cache_d0_kv, cache_d1_kv, cache_d2_kv)

    lam_init = 0.8 - 0.6 * math.exp(-0.3 * 0)
    w_in = w_in_ab[0].astype(BF16)
    w_out = w_out_ab[0].astype(BF16)
    wu, wd = w_up[0].astype(BF16), w_down[0].astype(BF16)
    amix = a_mix[0].astype(BF16)
    gm, gl = g2(norm_mix_g[0]), g2(norm_mlp_g[0])
    ascale, sg = g2(a_scale[0]), g2(b_subln_g[0])

    wv_t = w_in[:, D_A + 2 * D_B:].T
    u_p, q_p, kv_p, k_p, vt_p = _proj_ab(xp, gm, w_in, wv_t, tm_p, True, 256)
    hist_p = jnp.zeros((bp, POOL_HIST + 1, D_A), F32)
    a_p = _pool_mix(u_p, hist_p, amix, ascale, 0, bp, seq)
    ob_p = _diffattn_prompt(q_p, k_p, vt_p, b_lam[0], sg, lam_init, bp, seq, 128)

    u_s, q_s, kv_s, _, _ = _proj_ab(xs, gm, w_in, wv_t, ms, False, 256)
    hist_s = jnp.pad(state_a_pool[0], ((0, 0), (1, 0), (0, 0)))
    a_s = _pool_mix(u_s, hist_s, amix, ascale, past_len, bs, t_new)
    pool_pages = cache_b_kv[0].reshape(cache_b_kv.shape[1], PAGE_SIZE * 2 * H_B, 2 * HD_B)
    if mp // tm_p == bs and n_pages % MLP_PAGE_CHUNKS == 0:
        wu3 = jnp.transpose(wu.reshape(D_MODEL, MLP_PAGE_CHUNKS, D_FF // MLP_PAGE_CHUNKS), (1, 0, 2))
        wd3 = wd.reshape(MLP_PAGE_CHUNKS, D_FF // MLP_PAGE_CHUNKS, D_MODEL)
        xp, ob_s = _post_mixer_decode(xp, a_p, ob_p, w_out, gl, wu3, wd3, page_table, pool_pages, q_s, kv_s,
                                      b_lam[0], sg, lam_init, tm_p, t_new)
    else:
        xp = _post_mixer(xp, a_p, (ob_p,), (), (), w_out, gl, wu, wd, None, tm_p)
        ob_s = _diffattn_decode(page_table, pool_pages, q_s, kv_s, b_lam[0], sg, lam_init, bs, t_new,
                                min(32, n_pages))
    xs = _post_mixer(xs, a_s, (ob_s,), (), (), w_out, gl, wu, wd, None, ms)

    u_p3 = u_p.reshape(bp, seq, D_A)
    new_pool_p = u_p3[:, seq - POOL_HIST:][None]
    new_pool_s = jnp.concatenate([state_a_pool[0], u_s.reshape(bs, t_new, D_A)], axis=1)[:, -POOL_HIST:][None]
    new_bkv_p = kv_p.reshape(1, bp, seq, 2, H_B, 2 * HD_B)
    new_bkv_s = kv_s.reshape(1, bs, t_new, 2, H_B, 2 * HD_B)

    w_in = w_in_cd[0].astype(BF16)
    w_out = w_out_cd[0].astype(BF16)
    wu, wd = w_up[1].astype(BF16), w_down[1].astype(BF16)
    gm, gl, gf = g2(norm_mix_g[1]), g2(norm_mlp_g[1]), g2(norm_out_g)
    cw = c_conv_w[0]
    w_kv_t = jnp.stack([w_in[:, 3 * D_C + (3 * gi + 1) * D_DG:3 * D_C + (3 * gi + 3) * D_DG].T
                        for gi in range(len(D_WINDOWS))])

    def from_position_minor(a):
        a = a.reshape(a.shape[0], 2, H_DG, HD_D, a.shape[2])
        return jnp.transpose(a, (0, 4, 1, 2, 3))[None]

    outs = _proj_cd_prompt(xp, gm, w_in, w_kv_t, cw, bp, seq, tm_p)
    c_p, ztail_p = outs[0], outs[1]
    o_parts, lse_parts, new_d_p = [], [], []
    for gi, (win, dil) in enumerate(D_WINDOWS):
        q_g, k_g, v_g, kvt_g = outs[2 + 4 * gi:6 + 4 * gi]
        o_g, lse_g = _dilated_prompt(q_g, k_g, v_g, dil, 4)
        if dil == 1:
            o_g, lse_g = o_g.reshape(mp, D_DG), lse_g.reshape(mp, D_DG)
        o_parts.append(o_g)
        lse_parts.append(lse_g)
        new_d_p.append(from_position_minor(kvt_g))
    dils = tuple(dil for _, dil in D_WINDOWS)
    y_p = _post_mixer(xp, c_p, o_parts, lse_parts, dils, w_out, gl, wu, wd, gf, tm_p)
    tiles_per_seq = seq // tm_p
    new_conv_p = ztail_p.reshape(bp, tiles_per_seq, SUBLANES, D_C)[:, -1, SUBLANES - (CONV_W - 1):][None]

    hist_rows = jnp.pad(state_c_conv[0], ((0, 0), (t_new - (CONV_W - 1), 0), (0, 0))).reshape(ms, D_C)
    outs = _proj_cd_sample(xs, gm, w_in, w_kv_t, cw, hist_rows)
    c_s, z_s = outs[0], outs[1]
    o_parts, lse_parts, new_d_s = [], [], []
    for gi, (win, dil) in enumerate(D_WINDOWS):
        q_g, kvt_g = outs[2 + 2 * gi:4 + 2 * gi]
        cache = d_caches[gi][0]
        buf_len = cache.shape[1]
        cache_t = jnp.transpose(cache, (0, 2, 3, 4, 1)).reshape(bs, 2 * D_DG, buf_len)
        newc, o_g, lse_g = _dilated_sample(q_g, cache_t, kvt_g, dil, bs, t_new)
        o_parts.append(o_g)
        lse_parts.append(lse_g)
        new_d_s.append(from_position_minor(newc))
    y_s = _post_mixer(xs, c_s, o_parts, lse_parts, (1,) * len(D_WINDOWS), w_out, gl, wu, wd, gf, ms)
    new_conv_s = z_s.reshape(bs, t_new, D_C)[:, t_new - (CONV_W - 1):][None]

    return (y_p.reshape(bp, seq, D_MODEL), y_s.reshape(bs, t_new, D_MODEL),
            new_pool_p, new_pool_s, new_bkv_p, new_bkv_s, new_conv_p, new_conv_s,
            new_d_p[0], new_d_s[0], new_d_p[1], new_d_s[1], new_d_p[2], new_d_s[2])
```

```python
import functools
import math

import jax
import jax.numpy as jnp
from jax import lax
from jax.experimental import pallas as pl
from jax.experimental.pallas import tpu as pltpu

F32 = jnp.float32
BF16 = jnp.bfloat16

D_MODEL = 1024
RMS_EPS = 1e-6
D_A = 512
POOL_WINDOWS = (2, 4, 8, 16)
POOL_GROUP = 128
POOL_HIST = 15
H_B = 4
HD_B = 64
D_B = 512
D_C = 512
CONV_W = 3
D_WINDOWS = ((128, 1), (512, 4), (2048, 16))
N_BACK = 128
H_DG = 8
HD_D = 64
D_DG = 512
D_FF = 4096
PAGE_SIZE = 128

LANES = 128
SUBLANES = 8
VMEM_LIMIT_BYTES = 56 * 1024 * 1024
NEG_BIG = -1e30

QK_SCALE_B = HD_B ** -0.5
QK_SCALE_D = HD_D ** -0.5


def _params(*sem):
    return pltpu.CompilerParams(dimension_semantics=sem, vmem_limit_bytes=VMEM_LIMIT_BYTES)


def _resident(shape):
    nd = len(shape)
    return pl.BlockSpec(shape, lambda *_: (0,) * nd, pipeline_mode=pl.Buffered(1))


def _rmsnorm(x, g):
    return x * lax.rsqrt(jnp.mean(x * x, axis=-1, keepdims=True) + RMS_EPS) * g


def _dot(a, b):
    return jnp.dot(a, b, preferred_element_type=F32)


def _dot_nt(a, b):
    return lax.dot_general(a, b, (((1,), (1,)), ((), ())), preferred_element_type=F32)


def _split_heads_stack(q, half):
    lane = lax.broadcasted_iota(jnp.int32, q.shape, 1)
    zero = jnp.zeros_like(q)
    return jnp.concatenate([jnp.where(lane < half, q, zero), jnp.where(lane >= half, q, zero)], axis=0)


def _lane_blocks(n):
    return [slice(c * LANES, (c + 1) * LANES) for c in range(n // LANES)]


def _store_streams(ref, val, dil, scr):
    if dil == 1:
        ref[...] = val.astype(ref.dtype)
        return
    width = val.shape[1]
    for c, sl in enumerate(_lane_blocks(width)):
        scr[c] = val[:, sl]
    for r in range(dil):
        for c, sl in enumerate(_lane_blocks(width)):
            ref[:, r * width + sl.start:r * width + sl.stop] = (
                scr[c, pl.ds(r, ref.shape[0], stride=dil), :].astype(ref.dtype))


def _rows_from_streams(ref, dil, scr):
    width = ref.shape[1] // dil
    if dil == 1:
        return [ref[:, sl].astype(F32) for sl in _lane_blocks(width)]
    for r in range(dil):
        for c, sl in enumerate(_lane_blocks(width)):
            scr[c, pl.ds(r, ref.shape[0], stride=dil), :] = ref[:, r * width + sl.start:r * width + sl.stop].astype(F32)
    return [scr[c] for c in range(width // LANES)]


def _diff_lambda(blam, lam_init):
    a = jnp.sum(blam[0:1] * blam[1:2], axis=-1, keepdims=True)
    b = jnp.sum(blam[2:3] * blam[3:4], axis=-1, keepdims=True)
    return jnp.exp(a) - jnp.exp(b) + lam_init


def _proj_ab_kernel(prompt, x_ref, g_ref, w_ref, wvt_ref, u_ref, q_ref, kv_ref, k_ref, v_ref):
    h = _rmsnorm(x_ref[...], g_ref[...]).astype(BF16)
    u_ref[...] = _dot(h, w_ref[:, 0:D_A])
    q_ref[...] = (_dot(h, w_ref[:, D_A:D_A + D_B]) * QK_SCALE_B).astype(q_ref.dtype)
    kv = _dot(h, w_ref[:, D_A + D_B:D_A + 3 * D_B])
    k_ref[...] = kv[:, :D_B].astype(BF16)
    if not prompt:
        kv_ref[...] = kv
        v_ref[...] = kv[:, D_B:].astype(BF16)
        return
    n_blk = kv.shape[1] // LANES
    for c, sl in enumerate(_lane_blocks(kv.shape[1])):
        kv_ref[pl.ds(c, kv.shape[0], stride=n_blk), :] = kv[:, sl]
    vt = _dot_nt(wvt_ref[...], h).astype(BF16)
    tk = v_ref.shape[2]
    for j in range(v_ref.shape[0]):
        v_ref[j] = vt[:, j * tk:(j + 1) * tk]


def _proj_ab(x, g, w, wv_t, tm, prompt, tk):
    m = x.shape[0]
    n_in = w.shape[1]
    row = lambda n: pl.BlockSpec((tm, n), lambda i: (i, 0))
    n_blk = 2 * D_B // LANES
    if prompt:
        kv_shape, kv_spec = (m * n_blk, LANES), pl.BlockSpec((tm * n_blk, LANES), lambda i: (i, 0))
        v_shape, v_spec = (m // tk, D_B, tk), pl.BlockSpec((tm // tk, D_B, tk), lambda i: (i, 0, 0))
    else:
        kv_shape, kv_spec = (m, 2 * D_B), row(2 * D_B)
        v_shape, v_spec = (m, D_B), row(D_B)
    return pl.pallas_call(
        functools.partial(_proj_ab_kernel, prompt),
        grid=(m // tm,),
        in_specs=[row(D_MODEL), _resident((1, D_MODEL)), _resident((D_MODEL, n_in)), _resident(wv_t.shape)],
        out_specs=[row(D_A), row(D_B), kv_spec, row(D_B), v_spec],
        out_shape=[jax.ShapeDtypeStruct((m, D_A), F32), jax.ShapeDtypeStruct((m, D_B), BF16 if prompt else F32),
                   jax.ShapeDtypeStruct(kv_shape, F32), jax.ShapeDtypeStruct((m, D_B), BF16),
                   jax.ShapeDtypeStruct(v_shape, BF16)],
        compiler_params=_params("arbitrary"),
    )(x, g, w, wv_t)


def _pool_kernel(pos0, tchunk, u_ref, hist_ref, amix_ref, ascale_ref, o_ref, ext_ref):
    t_len = u_ref.shape[0]
    hpad = hist_ref.shape[0]
    ext_ref[0:hpad, :] = hist_ref[...]
    ext_ref[hpad:hpad + t_len, :] = u_ref[...]
    for t0 in range(0, t_len, tchunk):
        pos = pos0 + t0 + lax.broadcasted_iota(jnp.int32, (tchunk, 1), 0)
        for gi, w in enumerate(POOL_WINDOWS):
            sl = slice(gi * POOL_GROUP, (gi + 1) * POOL_GROUP)
            u = u_ref[t0:t0 + tchunk, sl]
            acc = u
            for j in range(1, w):
                acc = acc + ext_ref[hpad + t0 - j:hpad + t0 - j + tchunk, sl]
            cnt = jnp.minimum(pos + 1, w).astype(F32)
            p = (acc / cnt - u).astype(BF16)
            y = _dot(p, amix_ref[gi]) * ascale_ref[:, sl]
            o_ref[t0:t0 + tchunk, sl] = y.astype(o_ref.dtype)


def _pool_mix(u, hist16, amix, ascale, pos0, n_seq, t_len):
    hpad = hist16.shape[1]
    tchunk = min(t_len, 256)
    out = pl.pallas_call(
        functools.partial(_pool_kernel, pos0, tchunk),
        grid=(n_seq,),
        in_specs=[pl.BlockSpec((None, t_len, D_A), lambda s: (s, 0, 0)),
                  pl.BlockSpec((None, hpad, D_A), lambda s: (s, 0, 0)),
                  _resident(amix.shape), _resident((1, D_A))],
        out_specs=pl.BlockSpec((None, t_len, D_A), lambda s: (s, 0, 0)),
        out_shape=jax.ShapeDtypeStruct((n_seq, t_len, D_A), BF16 if t_len % 16 == 0 else F32),
        scratch_shapes=[pltpu.VMEM((hpad + t_len, D_A), F32)],
        compiler_params=_params("arbitrary"),
    )(u.reshape(n_seq, t_len, D_A), hist16, amix, ascale)
    return out.reshape(n_seq * t_len, D_A)


def _diff_finalize(acc, l, lam, sg, lam_init, t):
    o = acc[0:t] / l[0:t] - lam * (acc[t:2 * t] / l[t:2 * t])
    o = o * lax.rsqrt(jnp.mean(o * o, axis=-1, keepdims=True) + RMS_EPS) * sg
    return o * (1.0 - lam_init)


def _diffattn_prompt_kernel(lam_init, tc, q_ref, k_ref, vt_ref, blam_ref, sgt_ref, o_ref, m_scr, l_scr, acc_scr):
    seq = q_ref.shape[0]
    tk = vt_ref.shape[2]
    hw = 2 * HD_B
    heads = [slice(h * hw, (h + 1) * hw) for h in range(H_B)]
    lam = _diff_lambda(blam_ref[...], lam_init)
    sgt = sgt_ref[...]
    shape = (tk, 2 * tc)
    key = lax.broadcasted_iota(jnp.int32, shape, 0)
    col = lax.broadcasted_iota(jnp.int32, shape, 1)
    qcol = jnp.where(col >= tc, col - tc, col)

    def scores(j, h, qq):
        off = pl.multiple_of(j * tk, tk)
        return _dot_nt(k_ref[pl.ds(off, tk), heads[h]], qq)

    def chunk(c, carry):
        q0 = pl.multiple_of(c * tc, tc)
        jd = q0 // tk
        qqs = [_split_heads_stack(q_ref[pl.ds(q0, tc), sl], HD_B) for sl in heads]

        causal = (jd * tk + key) <= (q0 + qcol)
        ss = [scores(jd, h, qqs[h]) for h in range(H_B)]
        ps = []
        for h in range(H_B):
            s = jnp.where(causal, ss[h], -jnp.inf)
            m = jnp.max(s, axis=0, keepdims=True)
            p = jnp.exp(s - m)
            m_scr[h] = m
            l_scr[h] = jnp.sum(p, axis=0, keepdims=True)
            ps.append(p.astype(BF16))
        for h, sl in enumerate(heads):
            acc_scr[h] = _dot(vt_ref[jd, sl, :], ps[h])

        def body(j, inner):
            ss = [scores(j, h, qqs[h]) for h in range(H_B)]
            ps = []
            for h in range(H_B):
                m_prev = m_scr[h]
                m_new = jnp.maximum(m_prev, jnp.max(ss[h], axis=0, keepdims=True))
                alpha = jnp.exp(m_prev - m_new)
                p = jnp.exp(ss[h] - m_new)
                l_scr[h] = alpha * l_scr[h] + jnp.sum(p, axis=0, keepdims=True)
                m_scr[h] = m_new
                ps.append((alpha, p.astype(BF16)))
            for h, sl in enumerate(heads):
                alpha, p = ps[h]
                acc_scr[h] = alpha * acc_scr[h] + _dot(vt_ref[j, sl, :], p)
            return inner

        lax.fori_loop(0, jd, body, 0)
        for h, sl in enumerate(heads):
            o = acc_scr[h] / l_scr[h]
            o = o[:, 0:tc] - lam * o[:, tc:2 * tc]
            o = o * lax.rsqrt(jnp.mean(o * o, axis=0, keepdims=True) + RMS_EPS) * sgt * (1.0 - lam_init)
            o_ref[pl.ds(q0, tc), sl] = o.T.astype(o_ref.dtype)
        return carry

    lax.fori_loop(0, seq // tc, chunk, 0)


def _diffattn_prompt(q, k, vt, blam, sg, lam_init, n_batch, seq, tc):
    hw = 2 * HD_B
    tk = vt.shape[2]
    whole = pl.BlockSpec((seq, D_B), lambda b: (b, 0))
    return pl.pallas_call(
        functools.partial(_diffattn_prompt_kernel, lam_init, tc),
        grid=(n_batch,),
        in_specs=[whole, whole, pl.BlockSpec((seq // tk, D_B, tk), lambda b: (b, 0, 0)),
                  _resident(blam.shape), _resident((hw, 1))],
        out_specs=whole,
        out_shape=jax.ShapeDtypeStruct((n_batch * seq, D_B), BF16),
        scratch_shapes=[pltpu.VMEM((H_B, 1, 2 * tc), F32), pltpu.VMEM((H_B, 1, 2 * tc), F32),
                        pltpu.VMEM((H_B, hw, 2 * tc), F32)],
        compiler_params=_params("arbitrary"),
    )(q, k, vt, blam, sg.reshape(hw, 1))


def _online_softmax(rows, scores, valid, m_scr, l_scr):
    steps = []
    for r, s in zip(rows, scores):
        m_prev = m_scr[r]
        m_new = jnp.maximum(m_prev, jnp.max(s, axis=-1, keepdims=True))
        alpha = jnp.exp(m_prev - m_new)
        p = jnp.exp(s - m_new)
        if valid is not None:
            p = jnp.where(valid, p, 0.0)
        l_scr[r] = alpha * l_scr[r] + jnp.sum(p, axis=-1, keepdims=True)
        m_scr[r] = m_new
        steps.append((alpha, p.astype(BF16)))
    return steps


def _online_accumulate(rows, steps, values, acc_scr):
    for r, v, (alpha, p) in zip(rows, values, steps):
        acc_scr[r] = alpha * acc_scr[r] + _dot(p, v)


def _online_update(rows, scores, values, valid, m_scr, l_scr, acc_scr):
    _online_accumulate(rows, _online_softmax(rows, scores, valid, m_scr, l_scr), values, acc_scr)


def _init_online(m_scr, l_scr, acc_scr):
    m_scr[...] = jnp.full(m_scr.shape, NEG_BIG, F32)
    l_scr[...] = jnp.zeros(l_scr.shape, F32)
    acc_scr[...] = jnp.zeros(acc_scr.shape, F32)


def _decode_queries(q_ref):
    t_new = q_ref.shape[0]
    hw = 2 * HD_B
    q = q_ref[...].astype(BF16)
    qs = [_split_heads_stack(q[:, h * hw:(h + 1) * hw], HD_B) for h in range(H_B)]
    return qs, [slice(2 * t_new * h, 2 * t_new * (h + 1)) for h in range(H_B)]


def _decode_page_heads(page_refs, first_row):
    return jnp.concatenate([r[pl.ds(first_row, PAGE_SIZE, stride=2 * H_B), :] for r in page_refs],
                           axis=0).astype(BF16)


def _decode_finish(lam_init, qs, head_rows, kvnew_ref, blam_ref, sg_ref, o_ref, m_scr, l_scr, acc_scr, pad_scr):
    t_new = kvnew_ref.shape[0]
    hw = 2 * HD_B
    pad_scr[...] = jnp.zeros(pad_scr.shape, F32)
    pad_scr[0:t_new, :] = kvnew_ref[...]
    shape = (2 * t_new, pad_scr.shape[0])
    row = lax.broadcasted_iota(jnp.int32, shape, 0)
    col = lax.broadcasted_iota(jnp.int32, shape, 1)
    valid = col <= jnp.where(row >= t_new, row - t_new, row)
    scores = [jnp.where(valid, _dot_nt(qs[h], pad_scr[:, h * hw:(h + 1) * hw].astype(BF16)), NEG_BIG)
              for h in range(H_B)]
    values = [pad_scr[:, D_B + h * hw:D_B + (h + 1) * hw].astype(BF16) for h in range(H_B)]
    _online_update(head_rows, scores, values, valid, m_scr, l_scr, acc_scr)
    lam = _diff_lambda(blam_ref[...], lam_init)
    for h, rows in enumerate(head_rows):
        o = _diff_finalize(acc_scr[rows], l_scr[rows], lam, sg_ref[...], lam_init, t_new)
        o_ref[:, h * hw:(h + 1) * hw] = o.astype(o_ref.dtype)


def _diffattn_decode_kernel(pages_per_step, lam_init, pt_ref, q_ref, kvnew_ref, blam_ref, sg_ref, *rest):
    page_refs = rest[:pages_per_step]
    o_ref, m_scr, l_scr, acc_scr, pad_scr = rest[pages_per_step:]
    j = pl.program_id(1)
    qs, head_rows = _decode_queries(q_ref)

    @pl.when(j == 0)
    def _():
        _init_online(m_scr, l_scr, acc_scr)

    scores = [_dot_nt(qs[h], _decode_page_heads(page_refs, h)) for h in range(H_B)]
    values = [_decode_page_heads(page_refs, H_B + h) for h in range(H_B)]
    _online_update(head_rows, scores, values, None, m_scr, l_scr, acc_scr)

    @pl.when(j == pl.num_programs(1) - 1)
    def _():
        _decode_finish(lam_init, qs, head_rows, kvnew_ref, blam_ref, sg_ref, o_ref, m_scr, l_scr, acc_scr, pad_scr)


def _diffattn_decode(page_table, cache, q, kvnew, blam, sg, lam_init, n_batch, t_new, pages_per_step):
    n_pages = page_table.shape[1]
    hw = 2 * HD_B
    page_specs = [
        pl.BlockSpec((None, PAGE_SIZE * 2 * H_B, hw), functools.partial(
            lambda b, j, pt, t: (pt[b, j * pages_per_step + t], 0, 0), t=t))
        for t in range(pages_per_step)]
    grid_spec = pltpu.PrefetchScalarGridSpec(
        num_scalar_prefetch=1,
        grid=(n_batch, n_pages // pages_per_step),
        in_specs=[pl.BlockSpec((None, t_new, D_B), lambda b, j, pt: (b, 0, 0)),
                  pl.BlockSpec((None, t_new, 2 * D_B), lambda b, j, pt: (b, 0, 0)),
                  pl.BlockSpec(blam.shape, lambda b, j, pt: (0, 0)),
                  pl.BlockSpec((1, hw), lambda b, j, pt: (0, 0))] + page_specs,
        out_specs=pl.BlockSpec((None, t_new, D_B), lambda b, j, pt: (b, 0, 0)),
        scratch_shapes=[pltpu.VMEM((2 * t_new * H_B, 1), F32), pltpu.VMEM((2 * t_new * H_B, 1), F32),
                        pltpu.VMEM((2 * t_new * H_B, hw), F32), pltpu.VMEM((PAGE_SIZE, 2 * D_B), F32)])
    out = pl.pallas_call(
        functools.partial(_diffattn_decode_kernel, pages_per_step, lam_init),
        grid_spec=grid_spec,
        out_shape=jax.ShapeDtypeStruct((n_batch, t_new, D_B), F32),
        compiler_params=_params("arbitrary", "arbitrary"),
    )(page_table, q.reshape(n_batch, t_new, D_B), kvnew.reshape(n_batch, t_new, 2 * D_B), blam, sg,
      *([cache] * pages_per_step))
    return out.reshape(n_batch * t_new, D_B)


def _post_kernel(dils, ff_chunk, has_final, x_ref, first_ref, *rest):
    n_g = max(len(dils), 1)
    o_refs = rest[:n_g]
    lse_refs = rest[n_g:n_g + len(dils)]
    rest = rest[n_g + len(dils):]
    wout_ref, g_ref, wup_ref, wdn_ref = rest[:4]
    rest = rest[4:]
    gf_ref = rest[0] if has_final else None
    o_ref = rest[1] if has_final else rest[0]
    scratch = list(rest[2 if has_final else 1:])
    if not dils:
        second = o_refs[0][...].astype(BF16)
    else:
        scr_of = lambda d: scratch.pop(0) if d > 1 else None
        os_ = [_rows_from_streams(r, d, scr_of(d)) for r, d in zip(o_refs, dils)]
        ls_ = [_rows_from_streams(r, d, scr_of(d)) for r, d in zip(lse_refs, dils)]
        merged = []
        for c in range(len(os_[0])):
            lses = [l[c] for l in ls_]
            mx = functools.reduce(jnp.maximum, lses)
            es = [jnp.exp(l - mx) for l in lses]
            num = functools.reduce(lambda a, b: a + b, [e * o[c] for e, o in zip(es, os_)])
            merged.append((num / functools.reduce(lambda a, b: a + b, es)).astype(BF16))
        second = jnp.concatenate(merged, axis=-1)
    mix = jnp.concatenate([first_ref[...].astype(BF16), second], axis=-1)
    x1 = x_ref[...] + _dot(mix, wout_ref[...])
    h = _rmsnorm(x1, g_ref[...]).astype(BF16)
    acc = x1
    up = _dot(h, wup_ref[:, 0:ff_chunk])
    for c in range(0, D_FF, ff_chunk):
        nxt = _dot(h, wup_ref[:, c + ff_chunk:c + 2 * ff_chunk]) if c + ff_chunk < D_FF else None
        act = jnp.square(jnp.maximum(up, 0.0)).astype(BF16)
        acc = acc + _dot(act, wdn_ref[c:c + ff_chunk, :])
        up = nxt
    if has_final:
        acc = _rmsnorm(acc, gf_ref[...])
    o_ref[...] = acc


def _post_mixer(x, first, o_parts, lse_parts, dils, wout, g, wup, wdn, gf, tm):
    m = x.shape[0]
    row = lambda n: pl.BlockSpec((tm, n), lambda i: (i, 0))

    def part_spec(a, dil):
        if dil == 1:
            return row(a.shape[1])
        tiles_per_seq = a.shape[1] * dil // tm
        return pl.BlockSpec((None, tm // dil, a.shape[2]), lambda i: (i // tiles_per_seq, i % tiles_per_seq, 0))

    has_final = gf is not None
    part_dils = tuple(dils) if dils else (1,)
    args = [x, first, *o_parts, *lse_parts, wout, g, wup, wdn]
    specs = [row(D_MODEL), row(first.shape[1])]
    specs += [part_spec(a, d) for a, d in zip(o_parts, part_dils)] + [part_spec(a, d) for a, d in zip(lse_parts, dils)]
    specs += [_resident(wout.shape), _resident((1, D_MODEL)), _resident(wup.shape), _resident(wdn.shape)]
    if has_final:
        args.append(gf)
        specs.append(_resident((1, D_MODEL)))
    n_scratch = 2 * sum(d > 1 for d in dils)
    return pl.pallas_call(
        functools.partial(_post_kernel, tuple(dils), 1024, has_final),
        grid=(m // tm,),
        in_specs=specs,
        out_specs=row(D_MODEL),
        out_shape=jax.ShapeDtypeStruct((m, D_MODEL), F32),
        scratch_shapes=[pltpu.VMEM((D_DG // LANES, tm, LANES), F32)] * n_scratch,
        compiler_params=_params("arbitrary"),
    )(*args)


def _proj_cd_kernel(tiles_per_seq, tails, x_ref, g_ref, w_ref, wt_ref, cw_ref, hist_ref, *rest):
    n_g = len(D_WINDOWS)
    c_ref, ztail_ref = rest[0], rest[1]
    group_refs = rest[2:2 + 4 * n_g] if tiles_per_seq else rest[2:2 + 2 * n_g]
    carry_ref, stream_scr = rest[-2], rest[-1]
    tm = x_ref.shape[0]
    i = pl.program_id(0)
    h = _rmsnorm(x_ref[...], g_ref[...]).astype(BF16)
    gates = _dot(h, w_ref[:, 0:3 * D_C])
    b_gate = gates[:, 0:D_C]
    z = gates[:, D_C:2 * D_C] * gates[:, 2 * D_C:3 * D_C]
    row = lax.broadcasted_iota(jnp.int32, z.shape, 0)
    if tiles_per_seq:
        first = (i % tiles_per_seq) == 0
        prev = jnp.where(first, hist_ref[...], carry_ref[...])
        z1 = jnp.where(row == 0, prev[7:8], pltpu.roll(z, 1, 0))
        z2 = jnp.where(row == 0, prev[6:7], jnp.where(row == 1, prev[7:8], pltpu.roll(z, 2, 0)))
        carry_ref[...] = z[tm - SUBLANES:tm]
        ztail_ref[...] = z[tm - SUBLANES:tm]
    else:
        e = hist_ref[...]
        t = row & (SUBLANES - 1)
        z1 = jnp.where(t == 0, pltpu.roll(e, tm - 7, 0), pltpu.roll(z, 1, 0))
        z2 = jnp.where(t < 2, pltpu.roll(e, tm - 6, 0), pltpu.roll(z, 2, 0))
        ztail_ref[...] = z
    cw = cw_ref[...]
    cv = z2 * cw[0:1] + z1 * cw[1:2] + z * cw[2:3]
    c_ref[...] = (b_gate * cv).astype(c_ref.dtype)
    for gi in range(n_g):
        off = 3 * D_C + 3 * gi * D_DG
        if tiles_per_seq:
            q_ref, k_ref, v_ref, kvt_ref = group_refs[4 * gi:4 * gi + 4]
            res = _dot(h, w_ref[:, off:off + 3 * D_DG])
            dil = D_WINDOWS[gi][1]
            _store_streams(q_ref, res[:, 0:D_DG] * QK_SCALE_D, dil, stream_scr)
            _store_streams(k_ref, res[:, D_DG:2 * D_DG], dil, stream_scr)
            _store_streams(v_ref, res[:, 2 * D_DG:3 * D_DG], dil, stream_scr)
            tail, first_kept = tails[gi]
            if first_kept == 0:
                kvt_ref[...] = res[tm - tail:tm, D_DG:3 * D_DG].T
        else:
            q_ref, kvt_ref = group_refs[2 * gi:2 * gi + 2]
            q_ref[...] = _dot(h, w_ref[:, off:off + D_DG]) * QK_SCALE_D
            kvt_ref[...] = _dot_nt(wt_ref[gi], h)
    if tiles_per_seq:
        for gi in range(n_g):
            kvt_ref = group_refs[4 * gi + 3]
            tail, first_kept = tails[gi]
            if first_kept > 0:
                @pl.when(i % tiles_per_seq >= first_kept)
                def _(gi=gi, kvt_ref=kvt_ref, tail=tail):
                    kvt_ref[...] = _dot_nt(wt_ref[gi], h[tm - tail:tm])


def _proj_cd_prompt(x, g, w, wt, cw, n_batch, seq, tm):
    m = x.shape[0]
    tps = seq // tm
    row = lambda n, dt=None: pl.BlockSpec((tm, n), lambda i: (i, 0))
    hist = jnp.zeros((n_batch * SUBLANES, D_C), F32)
    out_shape = [jax.ShapeDtypeStruct((m, D_C), BF16), jax.ShapeDtypeStruct((m // tm * SUBLANES, D_C), F32)]
    out_specs = [row(D_C), pl.BlockSpec((SUBLANES, D_C), lambda i: (i, 0))]
    tails = []
    for win, dil in D_WINDOWS:
        keep = min(win, seq)
        tail = min(keep, tm)
        first_kept = tps - keep // tail
        tails.append((tail, first_kept))
        out_shape += [jax.ShapeDtypeStruct((n_batch, seq // dil, dil * D_DG), BF16)] * 3
        out_shape += [jax.ShapeDtypeStruct((n_batch, 2 * D_DG, keep), F32)]
        out_specs += [pl.BlockSpec((None, tm // dil, dil * D_DG), lambda i: (i // tps, i % tps, 0))] * 3
        out_specs += [pl.BlockSpec((None, 2 * D_DG, tail), functools.partial(
            lambda i, fk: (i // tps, 0, jnp.maximum(i % tps - fk, 0)), fk=first_kept))]
    return pl.pallas_call(
        functools.partial(_proj_cd_kernel, tps, tuple(tails)),
        grid=(m // tm,),
        in_specs=[row(D_MODEL), _resident((1, D_MODEL)), _resident(w.shape), _resident(wt.shape),
                  _resident(cw.shape), pl.BlockSpec((SUBLANES, D_C), lambda i: (i // tps, 0))],
        out_specs=out_specs,
        out_shape=out_shape,
        scratch_shapes=[pltpu.VMEM((SUBLANES, D_C), F32), pltpu.VMEM((D_DG // LANES, tm, LANES), F32)],
        compiler_params=_params("arbitrary"),
    )(x, g, w, wt, cw, hist)


def _proj_cd_sample(x, g, w, wt, cw, hist_rows):
    m = x.shape[0]
    full = lambda r, n: pl.BlockSpec((r, n), lambda i: (0, 0))
    out_shape = [jax.ShapeDtypeStruct((m, D_C), F32), jax.ShapeDtypeStruct((m, D_C), F32)]
    out_specs = [full(m, D_C), full(m, D_C)]
    for _ in D_WINDOWS:
        out_shape += [jax.ShapeDtypeStruct((m, D_DG), F32), jax.ShapeDtypeStruct((2 * D_DG, m), F32)]
        out_specs += [full(m, D_DG), full(2 * D_DG, m)]
    return pl.pallas_call(
        functools.partial(_proj_cd_kernel, 0, ()),
        grid=(1,),
        in_specs=[full(m, D_MODEL), _resident((1, D_MODEL)), _resident(w.shape), _resident(wt.shape),
                  _resident(cw.shape), full(m, D_C)],
        out_specs=out_specs,
        out_shape=out_shape,
        scratch_shapes=[pltpu.VMEM((SUBLANES, D_C), F32), pltpu.VMEM((D_DG // LANES, SUBLANES, LANES), F32)],
        compiler_params=_params("arbitrary"),
    )(x, g, w, wt, cw, hist_rows)


def _dilated_prompt_kernel(n_streams, q_ref, kc_ref, kp_ref, vc_ref, vp_ref, o_ref, lse_ref):
    bl = N_BACK
    n_qblk = q_ref.shape[0] // bl
    first_blk = pl.program_id(2) * n_qblk
    shape = (bl, 2 * bl)
    row = lax.broadcasted_iota(jnp.int32, shape, 0)
    col = lax.broadcasted_iota(jnp.int32, shape, 1)
    dist = row - col + bl
    band = (dist >= 0) & (dist <= N_BACK)
    hw = 2 * HD_D
    lane = lax.broadcasted_iota(jnp.int32, (bl, hw), 1)
    for r in range(n_streams):
        for qb in range(n_qblk):
            rows = slice(qb * bl, (qb + 1) * bl)
            prev_rows = slice((qb - 1) * bl, qb * bl)
            valid = (band & ((first_blk * bl + col - bl) >= 0)) if qb == 0 else band
            pairs = [slice(r * D_DG + p * hw, r * D_DG + (p + 1) * hw) for p in range(H_DG // 2)]
            scores = []
            for sl in pairs:
                qq = _split_heads_stack(q_ref[rows, sl], HD_D)
                k_prev = kp_ref[:, sl] if qb == 0 else kc_ref[prev_rows, sl]
                kcat = jnp.concatenate([k_prev, kc_ref[rows, sl]], axis=0)
                scores.append([_dot_nt(qq[e * bl:(e + 1) * bl], kcat) for e in range(2)])
            probs = []
            for pair_scores in scores:
                stats = []
                for s in pair_scores:
                    s = jnp.where(valid, s, -jnp.inf)
                    m = jnp.max(s, axis=-1, keepdims=True)
                    pr = jnp.exp(s - m)
                    l = jnp.sum(pr, axis=-1, keepdims=True)
                    stats.append((pr.astype(BF16), l, m + jnp.log(l)))
                probs.append(stats)
            for sl, stats in zip(pairs, probs):
                v_prev = vp_ref[:, sl] if qb == 0 else vc_ref[prev_rows, sl]
                vcat = jnp.concatenate([v_prev, vc_ref[rows, sl]], axis=0)
                outs = [_dot(pr, vcat) / l for pr, l, _ in stats]
                o_ref[rows, sl] = jnp.where(lane < HD_D, outs[0], outs[1]).astype(o_ref.dtype)
                lse_ref[rows, sl] = jnp.where(lane < HD_D, stats[0][2], stats[1][2])


def _dilated_prompt(q, k, v, dil, blocks_per_step):
    n_batch, length, _ = q.shape
    bl = N_BACK
    n_qblk = min(blocks_per_step, length // bl)
    n_streams = min(blocks_per_step // n_qblk, dil)
    rows, width = n_qblk * bl, n_streams * D_DG
    cur = pl.BlockSpec((None, rows, width), lambda b, r, j: (b, j, r))
    prev = pl.BlockSpec((None, bl, width), lambda b, r, j: (b, jnp.maximum(j * n_qblk - 1, 0), r))
    return pl.pallas_call(
        functools.partial(_dilated_prompt_kernel, n_streams),
        grid=(n_batch, dil // n_streams, length // rows),
        in_specs=[cur, cur, prev, cur, prev],
        out_specs=[cur, cur],
        out_shape=[jax.ShapeDtypeStruct(q.shape, BF16), jax.ShapeDtypeStruct(q.shape, F32)],
        compiler_params=_params("arbitrary", "arbitrary", "arbitrary"),
    )(q, k, k, v, v)


def _dilated_sample_kernel(dil, q_ref, cache_ref, kvnew_ref, newc_ref, o_ref, lse_ref):
    buf_len = cache_ref.shape[1]
    t_new = q_ref.shape[0]
    hw = 2 * HD_D
    n_blk = buf_len // LANES

    seq_in_block = pl.program_id(0) % (LANES // t_new)
    new_tile = pltpu.roll(kvnew_ref[...], (LANES - seq_in_block * t_new) % LANES, 1)

    lane = lax.broadcasted_iota(jnp.int32, (cache_ref.shape[0], LANES), 1)
    cur = pltpu.roll(cache_ref[:, 0:LANES], LANES - t_new, 1)
    for c in range(n_blk):
        following = cache_ref[:, (c + 1) * LANES:(c + 2) * LANES] if c + 1 < n_blk else new_tile
        nxt = pltpu.roll(following, LANES - t_new, 1)
        newc_ref[:, c * LANES:(c + 1) * LANES] = jnp.where(lane < LANES - t_new, cur, nxt)
        cur = nxt

    def iotas(n_keys):
        shape = (2 * t_new, n_keys)
        row = lax.broadcasted_iota(jnp.int32, shape, 0)
        return row & (t_new - 1), lax.broadcasted_iota(jnp.int32, shape, 1)

    qi, col = iotas(buf_len)
    d = buf_len + qi - col
    valid_buf = ((d & (dil - 1)) == 0) & (d <= N_BACK * dil)
    qi, col = iotas(LANES)
    d = qi - col
    valid_new = (col < t_new) & (d >= 0) & ((d & (dil - 1)) == 0)

    q = q_ref[...].astype(BF16)
    lane = lax.broadcasted_iota(jnp.int32, (t_new, hw), 1)
    for p in range(H_DG // 2):
        k_rows, v_rows = slice(p * hw, (p + 1) * hw), slice(D_DG + p * hw, D_DG + (p + 1) * hw)
        qq = _split_heads_stack(q[:, k_rows], HD_D)
        s_buf = jnp.where(valid_buf, _dot(qq, cache_ref[k_rows, :].astype(BF16)), NEG_BIG)
        s_new = jnp.where(valid_new, _dot(qq, new_tile[k_rows].astype(BF16)), NEG_BIG)
        m = jnp.maximum(jnp.max(s_buf, axis=-1, keepdims=True), jnp.max(s_new, axis=-1, keepdims=True))
        p_buf = jnp.where(valid_buf, jnp.exp(s_buf - m), 0.0)
        p_new = jnp.where(valid_new, jnp.exp(s_new - m), 0.0)
        l = jnp.sum(p_buf, axis=-1, keepdims=True) + jnp.sum(p_new, axis=-1, keepdims=True)
        acc = (_dot_nt(p_buf.astype(BF16), cache_ref[v_rows, :].astype(BF16))
               + _dot_nt(p_new.astype(BF16), new_tile[v_rows].astype(BF16)))
        o = acc / l
        lse = m + jnp.log(l)
        o_ref[:, k_rows] = jnp.where(lane < HD_D, o[0:t_new], o[t_new:2 * t_new])
        lse_ref[:, k_rows] = jnp.where(lane < HD_D, lse[0:t_new], lse[t_new:2 * t_new])


def _dilated_sample(q, cache_t, kvnew_t, dil, n_batch, t_new):
    buf_len = cache_t.shape[2]
    seqs_per_block = LANES // t_new
    small = lambda n: pl.BlockSpec((None, t_new, n), lambda b: (b, 0, 0))
    whole = pl.BlockSpec((None, 2 * D_DG, buf_len), lambda b: (b, 0, 0))
    newc, o, lse = pl.pallas_call(
        functools.partial(_dilated_sample_kernel, dil),
        grid=(n_batch,),
        in_specs=[small(D_DG), whole, pl.BlockSpec((2 * D_DG, LANES), lambda b: (0, b // seqs_per_block))],
        out_specs=[whole, small(D_DG), small(D_DG)],
        out_shape=[jax.ShapeDtypeStruct(cache_t.shape, F32),
                   jax.ShapeDtypeStruct((n_batch, t_new, D_DG), F32),
                   jax.ShapeDtypeStruct((n_batch, t_new, D_DG), F32)],
        compiler_params=_params("arbitrary"),
    )(q.reshape(n_batch, t_new, D_DG), cache_t, kvnew_t)
    return newc, o.reshape(n_batch * t_new, D_DG), lse.reshape(n_batch * t_new, D_DG)


def kernel(x_prompt, x_sample, state_a_pool, cache_b_kv, state_c_conv, cache_d0_kv, cache_d1_kv, cache_d2_kv, page_table, norm_mix_g, norm_mlp_g, norm_out_g, w_in_ab, w_out_ab, a_mix, a_scale, b_lam, b_subln_g, w_in_cd, w_out_cd, c_conv_w, w_up, w_down):
    bp, seq, _ = x_prompt.shape
    bs, t_new, _ = x_sample.shape
    n_pages = page_table.shape[1]
    past_len = n_pages * cache_b_kv.shape[2]
    mp, ms = bp * seq, bs * t_new
    tm_p = 512
    xp = x_prompt.reshape(mp, D_MODEL)
    xs = x_sample.reshape(ms, D_MODEL)
    g2 = lambda v: v.reshape(1, -1)
    d_caches = (cache_d0_kv, cache_d1_kv, cache_d2_kv)

    lam_init = 0.8 - 0.6 * math.exp(-0.3 * 0)
    w_in = w_in_ab[0].astype(BF16)
    w_out = w_out_ab[0].astype(BF16)
    wu, wd = w_up[0].astype(BF16), w_down[0].astype(BF16)
    amix = a_mix[0].astype(BF16)
    gm, gl = g2(norm_mix_g[0]), g2(norm_mlp_g[0])
    ascale, sg = g2(a_scale[0]), g2(b_subln_g[0])

    wv_t = w_in[:, D_A + 2 * D_B:].T
    u_p, q_p, kv_p, k_p, vt_p = _proj_ab(xp, gm, w_in, wv_t, tm_p, True, 512)
    hist_p = jnp.zeros((bp, POOL_HIST + 1, D_A), F32)
    a_p = _pool_mix(u_p, hist_p, amix, ascale, 0, bp, seq)
    ob_p = _diffattn_prompt(q_p, k_p, vt_p, b_lam[0], sg, lam_init, bp, seq, 256)

    u_s, q_s, kv_s, _, _ = _proj_ab(xs, gm, w_in, wv_t, ms, False, 256)
    hist_s = jnp.pad(state_a_pool[0], ((0, 0), (1, 0), (0, 0)))
    a_s = _pool_mix(u_s, hist_s, amix, ascale, past_len, bs, t_new)
    pool_pages = cache_b_kv[0].reshape(cache_b_kv.shape[1], PAGE_SIZE * 2 * H_B, 2 * HD_B)
    xp = _post_mixer(xp, a_p, (ob_p,), (), (), w_out, gl, wu, wd, None, tm_p)
    ob_s = _diffattn_decode(page_table, pool_pages, q_s, kv_s, b_lam[0], sg, lam_init, bs, t_new,
                            min(32, n_pages))
    xs = _post_mixer(xs, a_s, (ob_s,), (), (), w_out, gl, wu, wd, None, ms)

    u_p3 = u_p.reshape(bp, seq, D_A)
    new_pool_p = u_p3[:, seq - POOL_HIST:][None]
    new_pool_s = jnp.concatenate([state_a_pool[0], u_s.reshape(bs, t_new, D_A)], axis=1)[:, -POOL_HIST:][None]
    new_bkv_p = kv_p.reshape(1, bp, seq, 2, H_B, 2 * HD_B)
    new_bkv_s = kv_s.reshape(1, bs, t_new, 2, H_B, 2 * HD_B)

    w_in = w_in_cd[0].astype(BF16)
    w_out = w_out_cd[0].astype(BF16)
    wu, wd = w_up[1].astype(BF16), w_down[1].astype(BF16)
    gm, gl, gf = g2(norm_mix_g[1]), g2(norm_mlp_g[1]), g2(norm_out_g)
    cw = c_conv_w[0]
    w_kv_t = jnp.stack([w_in[:, 3 * D_C + (3 * gi + 1) * D_DG:3 * D_C + (3 * gi + 3) * D_DG].T
                        for gi in range(len(D_WINDOWS))])

    def from_position_minor(a):
        a = a.reshape(a.shape[0], 2, H_DG, HD_D, a.shape[2])
        return jnp.transpose(a, (0, 4, 1, 2, 3))[None]

    outs = _proj_cd_prompt(xp, gm, w_in, w_kv_t, cw, bp, seq, tm_p)
    c_p, ztail_p = outs[0], outs[1]
    o_parts, lse_parts, new_d_p = [], [], []
    for gi, (win, dil) in enumerate(D_WINDOWS):
        q_g, k_g, v_g, kvt_g = outs[2 + 4 * gi:6 + 4 * gi]
        o_g, lse_g = _dilated_prompt(q_g, k_g, v_g, dil, 4)
        if dil == 1:
            o_g, lse_g = o_g.reshape(mp, D_DG), lse_g.reshape(mp, D_DG)
        o_parts.append(o_g)
        lse_parts.append(lse_g)
        new_d_p.append(from_position_minor(kvt_g))
    dils = tuple(dil for _, dil in D_WINDOWS)
    y_p = _post_mixer(xp, c_p, o_parts, lse_parts, dils, w_out, gl, wu, wd, gf, tm_p)
    tiles_per_seq = seq // tm_p
    new_conv_p = ztail_p.reshape(bp, tiles_per_seq, SUBLANES, D_C)[:, -1, SUBLANES - (CONV_W - 1):][None]

    hist_rows = jnp.pad(state_c_conv[0], ((0, 0), (t_new - (CONV_W - 1), 0), (0, 0))).reshape(ms, D_C)
    outs = _proj_cd_sample(xs, gm, w_in, w_kv_t, cw, hist_rows)
    c_s, z_s = outs[0], outs[1]
    o_parts, lse_parts, new_d_s = [], [], []
    for gi, (win, dil) in enumerate(D_WINDOWS):
        q_g, kvt_g = outs[2 + 2 * gi:4 + 2 * gi]
        cache = d_caches[gi][0]
        buf_len = cache.shape[1]
        cache_t = jnp.transpose(cache, (0, 2, 3, 4, 1)).reshape(bs, 2 * D_DG, buf_len)
        newc, o_g, lse_g = _dilated_sample(q_g, cache_t, kvt_g, dil, bs, t_new)
        o_parts.append(o_g)
        lse_parts.append(lse_g)
        new_d_s.append(from_position_minor(newc))
    y_s = _post_mixer(xs, c_s, o_parts, lse_parts, (1,) * len(D_WINDOWS), w_out, gl, wu, wd, gf, ms)
    new_conv_s = z_s.reshape(bs, t_new, D_C)[:, t_new - (CONV_W - 1):][None]

    return (y_p.reshape(bp, seq, D_MODEL), y_s.reshape(bs, t_new, D_MODEL),
            new_pool_p, new_pool_s, new_bkv_p, new_bkv_s, new_conv_p, new_conv_s,
            new_d_p[0], new_d_s[0], new_d_p[1], new_d_s[1], new_d_p[2], new_d_s[2])
```

```python
import functools
import math

import jax
import jax.numpy as jnp
from jax import lax
from jax.experimental import pallas as pl
from jax.experimental.pallas import tpu as pltpu

F32 = jnp.float32
BF16 = jnp.bfloat16

D_MODEL = 1024
RMS_EPS = 1e-6
D_A = 512
POOL_WINDOWS = (2, 4, 8, 16)
POOL_GROUP = 128
POOL_HIST = 15
H_B = 4
HD_B = 64
D_B = 512
D_C = 512
CONV_W = 3
D_WINDOWS = ((128, 1), (512, 4), (2048, 16))
N_BACK = 128
H_DG = 8
HD_D = 64
D_DG = 512
D_FF = 4096
PAGE_SIZE = 128

LANES = 128
SUBLANES = 8
VMEM_LIMIT_BYTES = 56 * 1024 * 1024
NEG_BIG = -1e30

QK_SCALE_B = HD_B ** -0.5
QK_SCALE_D = HD_D ** -0.5


def _params(*sem):
    return pltpu.CompilerParams(dimension_semantics=sem, vmem_limit_bytes=VMEM_LIMIT_BYTES)


def _resident(shape):
    nd = len(shape)
    return pl.BlockSpec(shape, lambda *_: (0,) * nd, pipeline_mode=pl.Buffered(1))


def _rmsnorm(x, g):
    return x * lax.rsqrt(jnp.mean(x * x, axis=-1, keepdims=True) + RMS_EPS) * g


def _dot(a, b):
    return jnp.dot(a, b, preferred_element_type=F32)


def _dot_nt(a, b):
    return lax.dot_general(a, b, (((1,), (1,)), ((), ())), preferred_element_type=F32)


def _split_heads_stack(q, half):
    lane = lax.broadcasted_iota(jnp.int32, q.shape, 1)
    zero = jnp.zeros_like(q)
    return jnp.concatenate([jnp.where(lane < half, q, zero), jnp.where(lane >= half, q, zero)], axis=0)


def _lane_blocks(n):
    return [slice(c * LANES, (c + 1) * LANES) for c in range(n // LANES)]


def _store_streams(ref, val, dil, scr):
    if dil == 1:
        ref[...] = val.astype(ref.dtype)
        return
    width = val.shape[1]
    for c, sl in enumerate(_lane_blocks(width)):
        scr[c] = val[:, sl]
    for r in range(dil):
        for c, sl in enumerate(_lane_blocks(width)):
            ref[:, r * width + sl.start:r * width + sl.stop] = (
                scr[c, pl.ds(r, ref.shape[0], stride=dil), :].astype(ref.dtype))


def _rows_from_streams(ref, dil, scr):
    width = ref.shape[1] // dil
    if dil == 1:
        return [ref[:, sl].astype(F32) for sl in _lane_blocks(width)]
    for r in range(dil):
        for c, sl in enumerate(_lane_blocks(width)):
            scr[c, pl.ds(r, ref.shape[0], stride=dil), :] = ref[:, r * width + sl.start:r * width + sl.stop].astype(F32)
    return [scr[c] for c in range(width // LANES)]


def _diff_lambda(blam, lam_init):
    a = jnp.sum(blam[0:1] * blam[1:2], axis=-1, keepdims=True)
    b = jnp.sum(blam[2:3] * blam[3:4], axis=-1, keepdims=True)
    return jnp.exp(a) - jnp.exp(b) + lam_init


def _proj_ab_kernel(prompt, x_ref, g_ref, w_ref, u_ref, q_ref, kv_ref, k_ref, v_ref):
    h = _rmsnorm(x_ref[...], g_ref[...]).astype(BF16)
    u_ref[...] = _dot(h, w_ref[:, 0:D_A])
    q_ref[...] = (_dot(h, w_ref[:, D_A:D_A + D_B]) * QK_SCALE_B).astype(q_ref.dtype)
    kv = _dot(h, w_ref[:, D_A + D_B:D_A + 3 * D_B])
    k_ref[...] = kv[:, :D_B].astype(BF16)
    if not prompt:
        kv_ref[...] = kv
        v_ref[...] = kv[:, D_B:].astype(BF16)
        return
    n_blk = kv.shape[1] // LANES
    for c, sl in enumerate(_lane_blocks(kv.shape[1])):
        kv_ref[pl.ds(c, kv.shape[0], stride=n_blk), :] = kv[:, sl]
    vt = kv[:, D_B:].T.astype(BF16)
    tk = v_ref.shape[2]
    for j in range(v_ref.shape[0]):
        v_ref[j] = vt[:, j * tk:(j + 1) * tk]


def _proj_ab(x, g, w, tm, prompt, tk):
    m = x.shape[0]
    n_in = w.shape[1]
    row = lambda n: pl.BlockSpec((tm, n), lambda i: (i, 0))
    n_blk = 2 * D_B // LANES
    if prompt:
        kv_shape, kv_spec = (m * n_blk, LANES), pl.BlockSpec((tm * n_blk, LANES), lambda i: (i, 0))
        v_shape, v_spec = (m // tk, D_B, tk), pl.BlockSpec((tm // tk, D_B, tk), lambda i: (i, 0, 0))
    else:
        kv_shape, kv_spec = (m, 2 * D_B), row(2 * D_B)
        v_shape, v_spec = (m, D_B), row(D_B)
    return pl.pallas_call(
        functools.partial(_proj_ab_kernel, prompt),
        grid=(m // tm,),
        in_specs=[row(D_MODEL), _resident((1, D_MODEL)), _resident((D_MODEL, n_in))],
        out_specs=[row(D_A), row(D_B), kv_spec, row(D_B), v_spec],
        out_shape=[jax.ShapeDtypeStruct((m, D_A), F32), jax.ShapeDtypeStruct((m, D_B), BF16 if prompt else F32),
                   jax.ShapeDtypeStruct(kv_shape, F32), jax.ShapeDtypeStruct((m, D_B), BF16),
                   jax.ShapeDtypeStruct(v_shape, BF16)],
        compiler_params=_params("arbitrary"),
    )(x, g, w)


def _pool_kernel(pos0, tchunk, u_ref, hist_ref, amix_ref, ascale_ref, o_ref, ext_ref):
    t_len = u_ref.shape[0]
    hpad = hist_ref.shape[0]
    ext_ref[0:hpad, :] = hist_ref[...]
    ext_ref[hpad:hpad + t_len, :] = u_ref[...]
    for t0 in range(0, t_len, tchunk):
        pos = pos0 + t0 + lax.broadcasted_iota(jnp.int32, (tchunk, 1), 0)
        for gi, w in enumerate(POOL_WINDOWS):
            sl = slice(gi * POOL_GROUP, (gi + 1) * POOL_GROUP)
            u = u_ref[t0:t0 + tchunk, sl]
            acc = u
            for j in range(1, w):
                acc = acc + ext_ref[hpad + t0 - j:hpad + t0 - j + tchunk, sl]
            cnt = jnp.minimum(pos + 1, w).astype(F32)
            p = (acc / cnt - u).astype(BF16)
            y = _dot(p, amix_ref[gi]) * ascale_ref[:, sl]
            o_ref[t0:t0 + tchunk, sl] = y.astype(o_ref.dtype)


def _pool_mix(u, hist16, amix, ascale, pos0, n_seq, t_len):
    hpad = hist16.shape[1]
    tchunk = min(t_len, 256)
    out = pl.pallas_call(
        functools.partial(_pool_kernel, pos0, tchunk),
        grid=(n_seq,),
        in_specs=[pl.BlockSpec((None, t_len, D_A), lambda s: (s, 0, 0)),
                  pl.BlockSpec((None, hpad, D_A), lambda s: (s, 0, 0)),
                  _resident(amix.shape), _resident((1, D_A))],
        out_specs=pl.BlockSpec((None, t_len, D_A), lambda s: (s, 0, 0)),
        out_shape=jax.ShapeDtypeStruct((n_seq, t_len, D_A), BF16 if t_len % 16 == 0 else F32),
        scratch_shapes=[pltpu.VMEM((hpad + t_len, D_A), F32)],
        compiler_params=_params("arbitrary"),
    )(u.reshape(n_seq, t_len, D_A), hist16, amix, ascale)
    return out.reshape(n_seq * t_len, D_A)


def _diff_finalize(acc, l, lam, sg, lam_init, t):
    o = acc[0:t] / l[0:t] - lam * (acc[t:2 * t] / l[t:2 * t])
    o = o * lax.rsqrt(jnp.mean(o * o, axis=-1, keepdims=True) + RMS_EPS) * sg
    return o * (1.0 - lam_init)


def _diffattn_prompt_kernel(lam_init, tc, q_ref, k_ref, vt_ref, blam_ref, sgt_ref, o_ref, m_scr, l_scr, acc_scr):
    seq = q_ref.shape[0]
    tk = vt_ref.shape[2]
    hw = 2 * HD_B
    heads = [slice(h * hw, (h + 1) * hw) for h in range(H_B)]
    lam = _diff_lambda(blam_ref[...], lam_init)
    sgt = sgt_ref[...]
    shape = (tk, 2 * tc)
    key = lax.broadcasted_iota(jnp.int32, shape, 0)
    col = lax.broadcasted_iota(jnp.int32, shape, 1)
    qcol = jnp.where(col >= tc, col - tc, col)

    def scores(j, h, qq):
        off = pl.multiple_of(j * tk, tk)
        return _dot_nt(k_ref[pl.ds(off, tk), heads[h]], qq)

    def chunk(c, carry):
        q0 = pl.multiple_of(c * tc, tc)
        jd = q0 // tk
        qqs = [_split_heads_stack(q_ref[pl.ds(q0, tc), sl], HD_B) for sl in heads]

        causal = (jd * tk + key) <= (q0 + qcol)
        ss = [scores(jd, h, qqs[h]) for h in range(H_B)]
        ps = []
        for h in range(H_B):
            s = jnp.where(causal, ss[h], -jnp.inf)
            m = jnp.max(s, axis=0, keepdims=True)
            p = jnp.exp(s - m)
            m_scr[h] = m
            l_scr[h] = jnp.sum(p, axis=0, keepdims=True)
            ps.append(p.astype(BF16))
        for h, sl in enumerate(heads):
            acc_scr[h] = _dot(vt_ref[jd, sl, :], ps[h])

        def body(j, inner):
            ss = [scores(j, h, qqs[h]) for h in range(H_B)]
            ps = []
            for h in range(H_B):
                m_prev = m_scr[h]
                m_new = jnp.maximum(m_prev, jnp.max(ss[h], axis=0, keepdims=True))
                alpha = jnp.exp(m_prev - m_new)
                p = jnp.exp(ss[h] - m_new)
                l_scr[h] = alpha * l_scr[h] + jnp.sum(p, axis=0, keepdims=True)
                m_scr[h] = m_new
                ps.append((alpha, p.astype(BF16)))
            for h, sl in enumerate(heads):
                alpha, p = ps[h]
                acc_scr[h] = alpha * acc_scr[h] + _dot(vt_ref[j, sl, :], p)
            return inner

        lax.fori_loop(0, jd, body, 0)
        for h, sl in enumerate(heads):
            o = acc_scr[h] / l_scr[h]
            o = o[:, 0:tc] - lam * o[:, tc:2 * tc]
            o = o * lax.rsqrt(jnp.mean(o * o, axis=0, keepdims=True) + RMS_EPS) * sgt * (1.0 - lam_init)
            o_ref[pl.ds(q0, tc), sl] = o.T.astype(o_ref.dtype)
        return carry

    lax.fori_loop(0, seq // tc, chunk, 0)


def _diffattn_prompt(q, k, vt, blam, sg, lam_init, n_batch, seq, tc):
    hw = 2 * HD_B
    tk = vt.shape[2]
    whole = pl.BlockSpec((seq, D_B), lambda b: (b, 0))
    return pl.pallas_call(
        functools.partial(_diffattn_prompt_kernel, lam_init, tc),
        grid=(n_batch,),
        in_specs=[whole, whole, pl.BlockSpec((seq // tk, D_B, tk), lambda b: (b, 0, 0)),
                  _resident(blam.shape), _resident((hw, 1))],
        out_specs=whole,
        out_shape=jax.ShapeDtypeStruct((n_batch * seq, D_B), BF16),
        scratch_shapes=[pltpu.VMEM((H_B, 1, 2 * tc), F32), pltpu.VMEM((H_B, 1, 2 * tc), F32),
                        pltpu.VMEM((H_B, hw, 2 * tc), F32)],
        compiler_params=_params("arbitrary"),
    )(q, k, vt, blam, sg.reshape(hw, 1))


def _online_softmax(rows, scores, valid, m_scr, l_scr):
    steps = []
    for r, s in zip(rows, scores):
        m_prev = m_scr[r]
        m_new = jnp.maximum(m_prev, jnp.max(s, axis=-1, keepdims=True))
        alpha = jnp.exp(m_prev - m_new)
        p = jnp.exp(s - m_new)
        if valid is not None:
            p = jnp.where(valid, p, 0.0)
        l_scr[r] = alpha * l_scr[r] + jnp.sum(p, axis=-1, keepdims=True)
        m_scr[r] = m_new
        steps.append((alpha, p.astype(BF16)))
    return steps


def _online_accumulate(rows, steps, values, acc_scr):
    for r, v, (alpha, p) in zip(rows, values, steps):
        acc_scr[r] = alpha * acc_scr[r] + _dot(p, v)


def _online_update(rows, scores, values, valid, m_scr, l_scr, acc_scr):
    _online_accumulate(rows, _online_softmax(rows, scores, valid, m_scr, l_scr), values, acc_scr)


def _init_online(m_scr, l_scr, acc_scr):
    m_scr[...] = jnp.full(m_scr.shape, NEG_BIG, F32)
    l_scr[...] = jnp.zeros(l_scr.shape, F32)
    acc_scr[...] = jnp.zeros(acc_scr.shape, F32)


def _decode_queries(q_ref):
    t_new = q_ref.shape[0]
    hw = 2 * HD_B
    q = q_ref[...].astype(BF16)
    qs = [_split_heads_stack(q[:, h * hw:(h + 1) * hw], HD_B) for h in range(H_B)]
    return qs, [slice(2 * t_new * h, 2 * t_new * (h + 1)) for h in range(H_B)]


def _decode_page_heads(page_refs, first_row):
    return jnp.concatenate([r[pl.ds(first_row, PAGE_SIZE, stride=2 * H_B), :] for r in page_refs],
                           axis=0).astype(BF16)


def _decode_finish(lam_init, qs, head_rows, kvnew_ref, blam_ref, sg_ref, o_ref, m_scr, l_scr, acc_scr, pad_scr):
    t_new = kvnew_ref.shape[0]
    hw = 2 * HD_B
    pad_scr[...] = jnp.zeros(pad_scr.shape, F32)
    pad_scr[0:t_new, :] = kvnew_ref[...]
    shape = (2 * t_new, pad_scr.shape[0])
    row = lax.broadcasted_iota(jnp.int32, shape, 0)
    col = lax.broadcasted_iota(jnp.int32, shape, 1)
    valid = col <= jnp.where(row >= t_new, row - t_new, row)
    scores = [jnp.where(valid, _dot_nt(qs[h], pad_scr[:, h * hw:(h + 1) * hw].astype(BF16)), NEG_BIG)
              for h in range(H_B)]
    values = [pad_scr[:, D_B + h * hw:D_B + (h + 1) * hw].astype(BF16) for h in range(H_B)]
    _online_update(head_rows, scores, values, valid, m_scr, l_scr, acc_scr)
    lam = _diff_lambda(blam_ref[...], lam_init)
    for h, rows in enumerate(head_rows):
        o = _diff_finalize(acc_scr[rows], l_scr[rows], lam, sg_ref[...], lam_init, t_new)
        o_ref[:, h * hw:(h + 1) * hw] = o.astype(o_ref.dtype)


def _diffattn_decode_kernel(pages_per_step, lam_init, pt_ref, q_ref, kvnew_ref, blam_ref, sg_ref, *rest):
    page_refs = rest[:pages_per_step]
    o_ref, m_scr, l_scr, acc_scr, pad_scr = rest[pages_per_step:]
    j = pl.program_id(1)
    qs, head_rows = _decode_queries(q_ref)

    @pl.when(j == 0)
    def _():
        _init_online(m_scr, l_scr, acc_scr)

    scores = [_dot_nt(qs[h], _decode_page_heads(page_refs, h)) for h in range(H_B)]
    values = [_decode_page_heads(page_refs, H_B + h) for h in range(H_B)]
    _online_update(head_rows, scores, values, None, m_scr, l_scr, acc_scr)

    @pl.when(j == pl.num_programs(1) - 1)
    def _():
        _decode_finish(lam_init, qs, head_rows, kvnew_ref, blam_ref, sg_ref, o_ref, m_scr, l_scr, acc_scr, pad_scr)


def _diffattn_decode(page_table, cache, q, kvnew, blam, sg, lam_init, n_batch, t_new, pages_per_step):
    n_pages = page_table.shape[1]
    hw = 2 * HD_B
    page_specs = [
        pl.BlockSpec((None, PAGE_SIZE * 2 * H_B, hw), functools.partial(
            lambda b, j, pt, t: (pt[b, j * pages_per_step + t], 0, 0), t=t))
        for t in range(pages_per_step)]
    grid_spec = pltpu.PrefetchScalarGridSpec(
        num_scalar_prefetch=1,
        grid=(n_batch, n_pages // pages_per_step),
        in_specs=[pl.BlockSpec((None, t_new, D_B), lambda b, j, pt: (b, 0, 0)),
                  pl.BlockSpec((None, t_new, 2 * D_B), lambda b, j, pt: (b, 0, 0)),
                  pl.BlockSpec(blam.shape, lambda b, j, pt: (0, 0)),
                  pl.BlockSpec((1, hw), lambda b, j, pt: (0, 0))] + page_specs,
        out_specs=pl.BlockSpec((None, t_new, D_B), lambda b, j, pt: (b, 0, 0)),
        scratch_shapes=[pltpu.VMEM((2 * t_new * H_B, 1), F32), pltpu.VMEM((2 * t_new * H_B, 1), F32),
                        pltpu.VMEM((2 * t_new * H_B, hw), F32), pltpu.VMEM((PAGE_SIZE, 2 * D_B), F32)])
    out = pl.pallas_call(
        functools.partial(_diffattn_decode_kernel, pages_per_step, lam_init),
        grid_spec=grid_spec,
        out_shape=jax.ShapeDtypeStruct((n_batch, t_new, D_B), F32),
        compiler_params=_params("arbitrary", "arbitrary"),
    )(page_table, q.reshape(n_batch, t_new, D_B), kvnew.reshape(n_batch, t_new, 2 * D_B), blam, sg,
      *([cache] * pages_per_step))
    return out.reshape(n_batch * t_new, D_B)


def _post_kernel(dils, ff_chunk, has_final, x_ref, first_ref, *rest):
    n_g = max(len(dils), 1)
    o_refs = rest[:n_g]
    lse_refs = rest[n_g:n_g + len(dils)]
    rest = rest[n_g + len(dils):]
    wout_ref, g_ref, wup_ref, wdn_ref = rest[:4]
    rest = rest[4:]
    gf_ref = rest[0] if has_final else None
    o_ref = rest[1] if has_final else rest[0]
    scratch = list(rest[2 if has_final else 1:])
    if not dils:
        second = o_refs[0][...].astype(BF16)
    else:
        scr_of = lambda d: scratch.pop(0) if d > 1 else None
        os_ = [_rows_from_streams(r, d, scr_of(d)) for r, d in zip(o_refs, dils)]
        ls_ = [_rows_from_streams(r, d, scr_of(d)) for r, d in zip(lse_refs, dils)]
        merged = []
        for c in range(len(os_[0])):
            lses = [l[c] for l in ls_]
            mx = functools.reduce(jnp.maximum, lses)
            es = [jnp.exp(l - mx) for l in lses]
            num = functools.reduce(lambda a, b: a + b, [e * o[c] for e, o in zip(es, os_)])
            merged.append((num / functools.reduce(lambda a, b: a + b, es)).astype(BF16))
        second = jnp.concatenate(merged, axis=-1)
    mix = jnp.concatenate([first_ref[...].astype(BF16), second], axis=-1)
    x1 = x_ref[...] + _dot(mix, wout_ref[...])
    h = _rmsnorm(x1, g_ref[...]).astype(BF16)
    acc = x1
    up = _dot(h, wup_ref[:, 0:ff_chunk])
    for c in range(0, D_FF, ff_chunk):
        nxt = _dot(h, wup_ref[:, c + ff_chunk:c + 2 * ff_chunk]) if c + ff_chunk < D_FF else None
        act = jnp.square(jnp.maximum(up, 0.0)).astype(BF16)
        acc = acc + _dot(act, wdn_ref[c:c + ff_chunk, :])
        up = nxt
    if has_final:
        acc = _rmsnorm(acc, gf_ref[...])
    o_ref[...] = acc


def _post_mixer(x, first, o_parts, lse_parts, dils, wout, g, wup, wdn, gf, tm):
    m = x.shape[0]
    row = lambda n: pl.BlockSpec((tm, n), lambda i: (i, 0))

    def part_spec(a, dil):
        if dil == 1:
            return row(a.shape[1])
        tiles_per_seq = a.shape[1] * dil // tm
        return pl.BlockSpec((None, tm // dil, a.shape[2]), lambda i: (i // tiles_per_seq, i % tiles_per_seq, 0))

    has_final = gf is not None
    part_dils = tuple(dils) if dils else (1,)
    args = [x, first, *o_parts, *lse_parts, wout, g, wup, wdn]
    specs = [row(D_MODEL), row(first.shape[1])]
    specs += [part_spec(a, d) for a, d in zip(o_parts, part_dils)] + [part_spec(a, d) for a, d in zip(lse_parts, dils)]
    specs += [_resident(wout.shape), _resident((1, D_MODEL)), _resident(wup.shape), _resident(wdn.shape)]
    if has_final:
        args.append(gf)
        specs.append(_resident((1, D_MODEL)))
    n_scratch = 2 * sum(d > 1 for d in dils)
    return pl.pallas_call(
        functools.partial(_post_kernel, tuple(dils), 1024, has_final),
        grid=(m // tm,),
        in_specs=specs,
        out_specs=row(D_MODEL),
        out_shape=jax.ShapeDtypeStruct((m, D_MODEL), F32),
        scratch_shapes=[pltpu.VMEM((D_DG // LANES, tm, LANES), F32)] * n_scratch,
        compiler_params=_params("arbitrary"),
    )(*args)


def _proj_cd_kernel(tiles_per_seq, tails, x_ref, g_ref, w_ref, cw_ref, hist_ref, *rest):
    n_g = len(D_WINDOWS)
    c_ref, ztail_ref = rest[0], rest[1]
    group_refs = rest[2:2 + 4 * n_g] if tiles_per_seq else rest[2:2 + 2 * n_g]
    carry_ref, stream_scr = rest[-2], rest[-1]
    tm = x_ref.shape[0]
    i = pl.program_id(0)
    h = _rmsnorm(x_ref[...], g_ref[...]).astype(BF16)
    gates = _dot(h, w_ref[:, 0:3 * D_C])
    b_gate = gates[:, 0:D_C]
    z = gates[:, D_C:2 * D_C] * gates[:, 2 * D_C:3 * D_C]
    row = lax.broadcasted_iota(jnp.int32, z.shape, 0)
    if tiles_per_seq:
        first = (i % tiles_per_seq) == 0
        prev = jnp.where(first, hist_ref[...], carry_ref[...])
        z1 = jnp.where(row == 0, prev[7:8], pltpu.roll(z, 1, 0))
        z2 = jnp.where(row == 0, prev[6:7], jnp.where(row == 1, prev[7:8], pltpu.roll(z, 2, 0)))
        carry_ref[...] = z[tm - SUBLANES:tm]
        ztail_ref[...] = z[tm - SUBLANES:tm]
    else:
        e = hist_ref[...]
        t = row & (SUBLANES - 1)
        z1 = jnp.where(t == 0, pltpu.roll(e, tm - 7, 0), pltpu.roll(z, 1, 0))
        z2 = jnp.where(t < 2, pltpu.roll(e, tm - 6, 0), pltpu.roll(z, 2, 0))
        ztail_ref[...] = z
    cw = cw_ref[...]
    cv = z2 * cw[0:1] + z1 * cw[1:2] + z * cw[2:3]
    c_ref[...] = (b_gate * cv).astype(c_ref.dtype)
    for gi in range(n_g):
        off = 3 * D_C + 3 * gi * D_DG
        res = _dot(h, w_ref[:, off:off + 3 * D_DG])
        q = res[:, 0:D_DG] * QK_SCALE_D
        if tiles_per_seq:
            q_ref, k_ref, v_ref, kvt_ref = group_refs[4 * gi:4 * gi + 4]
            dil = D_WINDOWS[gi][1]
            _store_streams(q_ref, q, dil, stream_scr)
            _store_streams(k_ref, res[:, D_DG:2 * D_DG], dil, stream_scr)
            _store_streams(v_ref, res[:, 2 * D_DG:3 * D_DG], dil, stream_scr)
            kvt_ref[...] = res[tm - tails[gi]:tm, D_DG:3 * D_DG].T
        else:
            q_ref, kvt_ref = group_refs[2 * gi:2 * gi + 2]
            q_ref[...] = q
            kvt_ref[...] = res[:, D_DG:3 * D_DG].T


def _proj_cd_prompt(x, g, w, cw, n_batch, seq, tm):
    m = x.shape[0]
    tps = seq // tm
    row = lambda n, dt=None: pl.BlockSpec((tm, n), lambda i: (i, 0))
    hist = jnp.zeros((n_batch * SUBLANES, D_C), F32)
    out_shape = [jax.ShapeDtypeStruct((m, D_C), BF16), jax.ShapeDtypeStruct((m // tm * SUBLANES, D_C), F32)]
    out_specs = [row(D_C), pl.BlockSpec((SUBLANES, D_C), lambda i: (i, 0))]
    tails = []
    for win, dil in D_WINDOWS:
        keep = min(win, seq)
        tail = min(keep, tm)
        first_kept = tps - keep // tail
        tails.append(tail)
        out_shape += [jax.ShapeDtypeStruct((n_batch, seq // dil, dil * D_DG), BF16)] * 3
        out_shape += [jax.ShapeDtypeStruct((n_batch, 2 * D_DG, keep), F32)]
        out_specs += [pl.BlockSpec((None, tm // dil, dil * D_DG), lambda i: (i // tps, i % tps, 0))] * 3
        out_specs += [pl.BlockSpec((None, 2 * D_DG, tail), functools.partial(
            lambda i, fk: (i // tps, 0, jnp.maximum(i % tps - fk, 0)), fk=first_kept))]
    return pl.pallas_call(
        functools.partial(_proj_cd_kernel, tps, tuple(tails)),
        grid=(m // tm,),
        in_specs=[row(D_MODEL), _resident((1, D_MODEL)), _resident(w.shape),
                  _resident(cw.shape), pl.BlockSpec((SUBLANES, D_C), lambda i: (i // tps, 0))],
        out_specs=out_specs,
        out_shape=out_shape,
        scratch_shapes=[pltpu.VMEM((SUBLANES, D_C), F32), pltpu.VMEM((D_DG // LANES, tm, LANES), F32)],
        compiler_params=_params("arbitrary"),
    )(x, g, w, cw, hist)


def _proj_cd_sample(x, g, w, cw, hist_rows):
    m = x.shape[0]
    full = lambda r, n: pl.BlockSpec((r, n), lambda i: (0, 0))
    out_shape = [jax.ShapeDtypeStruct((m, D_C), F32), jax.ShapeDtypeStruct((m, D_C), F32)]
    out_specs = [full(m, D_C), full(m, D_C)]
    for _ in D_WINDOWS:
        out_shape += [jax.ShapeDtypeStruct((m, D_DG), F32), jax.ShapeDtypeStruct((2 * D_DG, m), F32)]
        out_specs += [full(m, D_DG), full(2 * D_DG, m)]
    return pl.pallas_call(
        functools.partial(_proj_cd_kernel, 0, ()),
        grid=(1,),
        in_specs=[full(m, D_MODEL), _resident((1, D_MODEL)), _resident(w.shape),
                  _resident(cw.shape), full(m, D_C)],
        out_specs=out_specs,
        out_shape=out_shape,
        scratch_shapes=[pltpu.VMEM((SUBLANES, D_C), F32), pltpu.VMEM((D_DG // LANES, SUBLANES, LANES), F32)],
        compiler_params=_params("arbitrary"),
    )(x, g, w, cw, hist_rows)


def _dilated_prompt_kernel(n_streams, q_ref, kc_ref, kp_ref, vc_ref, vp_ref, o_ref, lse_ref):
    bl = N_BACK
    n_qblk = q_ref.shape[0] // bl
    first_blk = pl.program_id(2) * n_qblk
    shape = (bl, 2 * bl)
    row = lax.broadcasted_iota(jnp.int32, shape, 0)
    col = lax.broadcasted_iota(jnp.int32, shape, 1)
    dist = row - col + bl
    band = (dist >= 0) & (dist <= N_BACK)
    hw = 2 * HD_D
    lane = lax.broadcasted_iota(jnp.int32, (bl, hw), 1)
    for r in range(n_streams):
        for qb in range(n_qblk):
            rows = slice(qb * bl, (qb + 1) * bl)
            prev_rows = slice((qb - 1) * bl, qb * bl)
            valid = (band & ((first_blk * bl + col - bl) >= 0)) if qb == 0 else band
            pairs = [slice(r * D_DG + p * hw, r * D_DG + (p + 1) * hw) for p in range(H_DG // 2)]
            scores = []
            for sl in pairs:
                qq = _split_heads_stack(q_ref[rows, sl], HD_D)
                k_prev = kp_ref[:, sl] if qb == 0 else kc_ref[prev_rows, sl]
                kcat = jnp.concatenate([k_prev, kc_ref[rows, sl]], axis=0)
                scores.append([_dot_nt(qq[e * bl:(e + 1) * bl], kcat) for e in range(2)])
            probs = []
            for pair_scores in scores:
                stats = []
                for s in pair_scores:
                    s = jnp.where(valid, s, -jnp.inf)
                    m = jnp.max(s, axis=-1, keepdims=True)
                    pr = jnp.exp(s - m)
                    l = jnp.sum(pr, axis=-1, keepdims=True)
                    stats.append((pr.astype(BF16), l, m + jnp.log(l)))
                probs.append(stats)
            for sl, stats in zip(pairs, probs):
                v_prev = vp_ref[:, sl] if qb == 0 else vc_ref[prev_rows, sl]
                vcat = jnp.concatenate([v_prev, vc_ref[rows, sl]], axis=0)
                outs = [_dot(pr, vcat) / l for pr, l, _ in stats]
                o_ref[rows, sl] = jnp.where(lane < HD_D, outs[0], outs[1]).astype(o_ref.dtype)
                lse_ref[rows, sl] = jnp.where(lane < HD_D, stats[0][2], stats[1][2])


def _dilated_prompt(q, k, v, dil, blocks_per_step):
    n_batch, length, _ = q.shape
    bl = N_BACK
    n_qblk = min(blocks_per_step, length // bl)
    n_streams = min(blocks_per_step // n_qblk, dil)
    rows, width = n_qblk * bl, n_streams * D_DG
    cur = pl.BlockSpec((None, rows, width), lambda b, r, j: (b, j, r))
    prev = pl.BlockSpec((None, bl, width), lambda b, r, j: (b, jnp.maximum(j * n_qblk - 1, 0), r))
    return pl.pallas_call(
        functools.partial(_dilated_prompt_kernel, n_streams),
        grid=(n_batch, dil // n_streams, length // rows),
        in_specs=[cur, cur, prev, cur, prev],
        out_specs=[cur, cur],
        out_shape=[jax.ShapeDtypeStruct(q.shape, BF16), jax.ShapeDtypeStruct(q.shape, F32)],
        compiler_params=_params("arbitrary", "arbitrary", "arbitrary"),
    )(q, k, k, v, v)


def _dilated_sample_kernel(dil, q_ref, cache_ref, kvnew_ref, newc_ref, o_ref, lse_ref):
    n_seq = q_ref.shape[0]
    for s in range(n_seq):
        _dilated_sample_one(dil, pl.program_id(0) * n_seq + s, q_ref.at[s], cache_ref.at[s], kvnew_ref,
                            newc_ref.at[s], o_ref.at[s], lse_ref.at[s])


def _dilated_sample_one(dil, seq, q_ref, cache_ref, kvnew_ref, newc_ref, o_ref, lse_ref):
    buf_len = cache_ref.shape[1]
    t_new = q_ref.shape[0]
    hw = 2 * HD_D
    n_blk = buf_len // LANES

    seq_in_block = seq % (LANES // t_new)
    new_tile = pltpu.roll(kvnew_ref[...], (LANES - seq_in_block * t_new) % LANES, 1)

    lane = lax.broadcasted_iota(jnp.int32, (cache_ref.shape[0], LANES), 1)
    cur = pltpu.roll(cache_ref[:, 0:LANES], LANES - t_new, 1)
    for c in range(n_blk):
        following = cache_ref[:, (c + 1) * LANES:(c + 2) * LANES] if c + 1 < n_blk else new_tile
        nxt = pltpu.roll(following, LANES - t_new, 1)
        newc_ref[:, c * LANES:(c + 1) * LANES] = jnp.where(lane < LANES - t_new, cur, nxt)
        cur = nxt

    def iotas(n_keys):
        shape = (2 * t_new, n_keys)
        row = lax.broadcasted_iota(jnp.int32, shape, 0)
        return row & (t_new - 1), lax.broadcasted_iota(jnp.int32, shape, 1)

    qi, col = iotas(buf_len)
    d = buf_len + qi - col
    valid_buf = ((d & (dil - 1)) == 0) & (d <= N_BACK * dil)
    qi, col = iotas(LANES)
    d = qi - col
    valid_new = (col < t_new) & (d >= 0) & ((d & (dil - 1)) == 0)

    q = q_ref[...].astype(BF16)
    lane = lax.broadcasted_iota(jnp.int32, (t_new, hw), 1)
    for p in range(H_DG // 2):
        k_rows, v_rows = slice(p * hw, (p + 1) * hw), slice(D_DG + p * hw, D_DG + (p + 1) * hw)
        qq = _split_heads_stack(q[:, k_rows], HD_D)
        s_buf = jnp.where(valid_buf, _dot(qq, cache_ref[k_rows, :].astype(BF16)), NEG_BIG)
        s_new = jnp.where(valid_new, _dot(qq, new_tile[k_rows].astype(BF16)), NEG_BIG)
        m = jnp.maximum(jnp.max(s_buf, axis=-1, keepdims=True), jnp.max(s_new, axis=-1, keepdims=True))
        p_buf = jnp.where(valid_buf, jnp.exp(s_buf - m), 0.0)
        p_new = jnp.where(valid_new, jnp.exp(s_new - m), 0.0)
        l = jnp.sum(p_buf, axis=-1, keepdims=True) + jnp.sum(p_new, axis=-1, keepdims=True)
        acc = (_dot_nt(p_buf.astype(BF16), cache_ref[v_rows, :].astype(BF16))
               + _dot_nt(p_new.astype(BF16), new_tile[v_rows].astype(BF16)))
        o = acc / l
        lse = m + jnp.log(l)
        o_ref[:, k_rows] = jnp.where(lane < HD_D, o[0:t_new], o[t_new:2 * t_new])
        lse_ref[:, k_rows] = jnp.where(lane < HD_D, lse[0:t_new], lse[t_new:2 * t_new])


def _dilated_sample(q, cache_t, kvnew_t, dil, n_batch, t_new, seqs_per_step):
    buf_len = cache_t.shape[2]
    steps_per_lane_block = LANES // t_new // seqs_per_step
    small = lambda n: pl.BlockSpec((seqs_per_step, t_new, n), lambda b: (b, 0, 0))
    whole = pl.BlockSpec((seqs_per_step, 2 * D_DG, buf_len), lambda b: (b, 0, 0))
    newc, o, lse = pl.pallas_call(
        functools.partial(_dilated_sample_kernel, dil),
        grid=(n_batch // seqs_per_step,),
        in_specs=[small(D_DG), whole, pl.BlockSpec((2 * D_DG, LANES), lambda b: (0, b // steps_per_lane_block))],
        out_specs=[whole, small(D_DG), small(D_DG)],
        out_shape=[jax.ShapeDtypeStruct(cache_t.shape, F32),
                   jax.ShapeDtypeStruct((n_batch, t_new, D_DG), F32),
                   jax.ShapeDtypeStruct((n_batch, t_new, D_DG), F32)],
        compiler_params=_params("arbitrary"),
    )(q.reshape(n_batch, t_new, D_DG), cache_t, kvnew_t)
    return newc, o.reshape(n_batch * t_new, D_DG), lse.reshape(n_batch * t_new, D_DG)


def kernel(x_prompt, x_sample, state_a_pool, cache_b_kv, state_c_conv, cache_d0_kv, cache_d1_kv, cache_d2_kv, page_table, norm_mix_g, norm_mlp_g, norm_out_g, w_in_ab, w_out_ab, a_mix, a_scale, b_lam, b_subln_g, w_in_cd, w_out_cd, c_conv_w, w_up, w_down):
    bp, seq, _ = x_prompt.shape
    bs, t_new, _ = x_sample.shape
    n_pages = page_table.shape[1]
    past_len = n_pages * cache_b_kv.shape[2]
    mp, ms = bp * seq, bs * t_new
    tm_p = 512
    xp = x_prompt.reshape(mp, D_MODEL)
    xs = x_sample.reshape(ms, D_MODEL)
    g2 = lambda v: v.reshape(1, -1)
    d_caches = (cache_d0_kv, cache_d1_kv, cache_d2_kv)

    lam_init = 0.8 - 0.6 * math.exp(-0.3 * 0)
    w_in = w_in_ab[0].astype(BF16)
    w_out = w_out_ab[0].astype(BF16)
    wu, wd = w_up[0].astype(BF16), w_down[0].astype(BF16)
    amix = a_mix[0].astype(BF16)
    gm, gl = g2(norm_mix_g[0]), g2(norm_mlp_g[0])
    ascale, sg = g2(a_scale[0]), g2(b_subln_g[0])

    u_p, q_p, kv_p, k_p, vt_p = _proj_ab(xp, gm, w_in, tm_p, True, 512)
    hist_p = jnp.zeros((bp, POOL_HIST + 1, D_A), F32)
    a_p = _pool_mix(u_p, hist_p, amix, ascale, 0, bp, seq)
    ob_p = _diffattn_prompt(q_p, k_p, vt_p, b_lam[0], sg, lam_init, bp, seq, 256)

    u_s, q_s, kv_s, _, _ = _proj_ab(xs, gm, w_in, ms, False, 256)
    hist_s = jnp.pad(state_a_pool[0], ((0, 0), (1, 0), (0, 0)))
    a_s = _pool_mix(u_s, hist_s, amix, ascale, past_len, bs, t_new)
    pool_pages = cache_b_kv[0].reshape(cache_b_kv.shape[1], PAGE_SIZE * 2 * H_B, 2 * HD_B)
    xp = _post_mixer(xp, a_p, (ob_p,), (), (), w_out, gl, wu, wd, None, tm_p)
    ob_s = _diffattn_decode(page_table, pool_pages, q_s, kv_s, b_lam[0], sg, lam_init, bs, t_new,
                            min(32, n_pages))
    xs = _post_mixer(xs, a_s, (ob_s,), (), (), w_out, gl, wu, wd, None, ms)

    u_p3 = u_p.reshape(bp, seq, D_A)
    new_pool_p = u_p3[:, seq - POOL_HIST:][None]
    new_pool_s = jnp.concatenate([state_a_pool[0], u_s.reshape(bs, t_new, D_A)], axis=1)[:, -POOL_HIST:][None]
    new_bkv_p = kv_p.reshape(1, bp, seq, 2, H_B, 2 * HD_B)
    new_bkv_s = kv_s.reshape(1, bs, t_new, 2, H_B, 2 * HD_B)

    w_in = w_in_cd[0].astype(BF16)
    w_out = w_out_cd[0].astype(BF16)
    wu, wd = w_up[1].astype(BF16), w_down[1].astype(BF16)
    gm, gl, gf = g2(norm_mix_g[1]), g2(norm_mlp_g[1]), g2(norm_out_g)
    cw = c_conv_w[0]

    def from_position_minor(a):
        a = a.reshape(a.shape[0], 2, H_DG, HD_D, a.shape[2])
        return jnp.transpose(a, (0, 4, 1, 2, 3))[None]

    outs = _proj_cd_prompt(xp, gm, w_in, cw, bp, seq, tm_p)
    c_p, ztail_p = outs[0], outs[1]
    o_parts, lse_parts, new_d_p = [], [], []
    for gi, (win, dil) in enumerate(D_WINDOWS):
        q_g, k_g, v_g, kvt_g = outs[2 + 4 * gi:6 + 4 * gi]
        o_g, lse_g = _dilated_prompt(q_g, k_g, v_g, dil, 4)
        if dil == 1:
            o_g, lse_g = o_g.reshape(mp, D_DG), lse_g.reshape(mp, D_DG)
        o_parts.append(o_g)
        lse_parts.append(lse_g)
        new_d_p.append(from_position_minor(kvt_g))
    dils = tuple(dil for _, dil in D_WINDOWS)
    y_p = _post_mixer(xp, c_p, o_parts, lse_parts, dils, w_out, gl, wu, wd, gf, tm_p)
    tiles_per_seq = seq // tm_p
    new_conv_p = ztail_p.reshape(bp, tiles_per_seq, SUBLANES, D_C)[:, -1, SUBLANES - (CONV_W - 1):][None]

    hist_rows = jnp.pad(state_c_conv[0], ((0, 0), (t_new - (CONV_W - 1), 0), (0, 0))).reshape(ms, D_C)
    outs = _proj_cd_sample(xs, gm, w_in, cw, hist_rows)
    c_s, z_s = outs[0], outs[1]
    o_parts, lse_parts, new_d_s = [], [], []
    for gi, (win, dil) in enumerate(D_WINDOWS):
        q_g, kvt_g = outs[2 + 2 * gi:4 + 2 * gi]
        cache = d_caches[gi][0]
        buf_len = cache.shape[1]
        cache_t = jnp.transpose(cache, (0, 2, 3, 4, 1)).reshape(bs, 2 * D_DG, buf_len)
        seqs_per_step = max(1, min(SUBLANES, D_WINDOWS[-1][0] // (2 * buf_len)))
        newc, o_g, lse_g = _dilated_sample(q_g, cache_t, kvt_g, dil, bs, t_new, seqs_per_step)
        o_parts.append(o_g)
        lse_parts.append(lse_g)
        new_d_s.append(from_position_minor(newc))
    y_s = _post_mixer(xs, c_s, o_parts, lse_parts, (1,) * len(D_WINDOWS), w_out, gl, wu, wd, gf, ms)
    new_conv_s = z_s.reshape(bs, t_new, D_C)[:, t_new - (CONV_W - 1):][None]

    return (y_p.reshape(bp, seq, D_MODEL), y_s.reshape(bs, t_new, D_MODEL),
            new_pool_p, new_pool_s, new_bkv_p, new_bkv_s, new_conv_p, new_conv_s,
            new_d_p[0], new_d_s[0], new_d_p[1], new_d_s[1], new_d_p[2], new_d_s[2])
```

```python
import functools
import math

import jax
import jax.numpy as jnp
from jax import lax
from jax.experimental import pallas as pl
from jax.experimental.pallas import tpu as pltpu

F32 = jnp.float32
BF16 = jnp.bfloat16

D_MODEL = 1024
RMS_EPS = 1e-6
D_A = 512
POOL_WINDOWS = (2, 4, 8, 16)
POOL_GROUP = 128
POOL_HIST = 15
H_B = 4
HD_B = 64
D_B = 512
D_C = 512
CONV_W = 3
D_WINDOWS = ((128, 1), (512, 4), (2048, 16))
N_BACK = 128
H_DG = 8
HD_D = 64
D_DG = 512
D_FF = 4096
PAGE_SIZE = 128

LANES = 128
SUBLANES = 8
VMEM_LIMIT_BYTES = 56 * 1024 * 1024
NEG_BIG = -1e30

LOG2_E = math.log2(math.e)
LN_2 = math.log(2.0)
QK_SCALE_B = HD_B ** -0.5 * LOG2_E
QK_SCALE_D = HD_D ** -0.5 * LOG2_E


def _params(*sem):
    return pltpu.CompilerParams(dimension_semantics=sem, vmem_limit_bytes=VMEM_LIMIT_BYTES)


def _resident(shape):
    nd = len(shape)
    return pl.BlockSpec(shape, lambda *_: (0,) * nd, pipeline_mode=pl.Buffered(1))


def _rmsnorm(x, g):
    return x * lax.rsqrt(jnp.mean(x * x, axis=-1, keepdims=True) + RMS_EPS) * g


def _dot(a, b):
    return jnp.dot(a, b, preferred_element_type=F32)


def _dot_nt(a, b):
    return lax.dot_general(a, b, (((1,), (1,)), ((), ())), preferred_element_type=F32)


def _split_heads_stack(q, half):
    lane = lax.broadcasted_iota(jnp.int32, q.shape, 1)
    zero = jnp.zeros_like(q)
    return jnp.concatenate([jnp.where(lane < half, q, zero), jnp.where(lane >= half, q, zero)], axis=0)


def _lane_blocks(n):
    return [slice(c * LANES, (c + 1) * LANES) for c in range(n // LANES)]


def _store_streams(ref, val, dil, scr):
    if dil == 1:
        ref[...] = val.astype(ref.dtype)
        return
    width = val.shape[1]
    for c, sl in enumerate(_lane_blocks(width)):
        scr[c] = val[:, sl]
    for r in range(dil):
        for c, sl in enumerate(_lane_blocks(width)):
            ref[:, r * width + sl.start:r * width + sl.stop] = (
                scr[c, pl.ds(r, ref.shape[0], stride=dil), :].astype(ref.dtype))


def _diff_lambda(blam, lam_init):
    a = jnp.sum(blam[0:1] * blam[1:2], axis=-1, keepdims=True)
    b = jnp.sum(blam[2:3] * blam[3:4], axis=-1, keepdims=True)
    return jnp.exp(a) - jnp.exp(b) + lam_init


def _proj_ab_kernel(prompt, x_ref, g_ref, w_ref, u_ref, q_ref, kv_ref, k_ref, v_ref):
    h = _rmsnorm(x_ref[...], g_ref[...]).astype(BF16)
    u_ref[...] = _dot(h, w_ref[:, 0:D_A])
    q_ref[...] = (_dot(h, w_ref[:, D_A:D_A + D_B]) * QK_SCALE_B).astype(q_ref.dtype)
    kv = _dot(h, w_ref[:, D_A + D_B:D_A + 3 * D_B])
    k_ref[...] = kv[:, :D_B].astype(BF16)
    if not prompt:
        kv_ref[...] = kv
        v_ref[...] = kv[:, D_B:].astype(BF16)
        return
    n_blk = kv.shape[1] // LANES
    for c, sl in enumerate(_lane_blocks(kv.shape[1])):
        kv_ref[pl.ds(c, kv.shape[0], stride=n_blk), :] = kv[:, sl]
    vt = kv[:, D_B:].T.astype(BF16)
    tk = v_ref.shape[2]
    for j in range(v_ref.shape[0]):
        v_ref[j] = vt[:, j * tk:(j + 1) * tk]


def _proj_ab(x, g, w, tm, prompt, tk):
    m = x.shape[0]
    n_in = w.shape[1]
    row = lambda n: pl.BlockSpec((tm, n), lambda i: (i, 0))
    n_blk = 2 * D_B // LANES
    if prompt:
        kv_shape, kv_spec = (m * n_blk, LANES), pl.BlockSpec((tm * n_blk, LANES), lambda i: (i, 0))
        v_shape, v_spec = (m // tk, D_B, tk), pl.BlockSpec((tm // tk, D_B, tk), lambda i: (i, 0, 0))
    else:
        kv_shape, kv_spec = (m, 2 * D_B), row(2 * D_B)
        v_shape, v_spec = (m, D_B), row(D_B)
    return pl.pallas_call(
        functools.partial(_proj_ab_kernel, prompt),
        grid=(m // tm,),
        in_specs=[row(D_MODEL), _resident((1, D_MODEL)), _resident((D_MODEL, n_in))],
        out_specs=[row(D_A), row(D_B), kv_spec, row(D_B), v_spec],
        out_shape=[jax.ShapeDtypeStruct((m, D_A), F32), jax.ShapeDtypeStruct((m, D_B), BF16 if prompt else F32),
                   jax.ShapeDtypeStruct(kv_shape, F32), jax.ShapeDtypeStruct((m, D_B), BF16),
                   jax.ShapeDtypeStruct(v_shape, BF16)],
        compiler_params=_params("arbitrary"),
    )(x, g, w)


def _pool_kernel(pos0, tchunk, u_ref, hist_ref, amix_ref, ascale_ref, o_ref, ext_ref):
    t_len = u_ref.shape[0]
    hpad = hist_ref.shape[0]
    ext_ref[0:hpad, :] = hist_ref[...]
    ext_ref[hpad:hpad + t_len, :] = u_ref[...]
    for t0 in range(0, t_len, tchunk):
        pos = pos0 + t0 + lax.broadcasted_iota(jnp.int32, (tchunk, 1), 0)
        for gi, w in enumerate(POOL_WINDOWS):
            sl = slice(gi * POOL_GROUP, (gi + 1) * POOL_GROUP)
            u = u_ref[t0:t0 + tchunk, sl]
            acc = u
            for j in range(1, w):
                acc = acc + ext_ref[hpad + t0 - j:hpad + t0 - j + tchunk, sl]
            cnt = jnp.minimum(pos + 1, w).astype(F32)
            p = (acc / cnt - u).astype(BF16)
            y = _dot(p, amix_ref[gi]) * ascale_ref[:, sl]
            o_ref[t0:t0 + tchunk, sl] = y.astype(o_ref.dtype)


def _pool_mix(u, hist16, amix, ascale, pos0, n_seq, t_len):
    hpad = hist16.shape[1]
    tchunk = min(t_len, 256)
    out = pl.pallas_call(
        functools.partial(_pool_kernel, pos0, tchunk),
        grid=(n_seq,),
        in_specs=[pl.BlockSpec((None, t_len, D_A), lambda s: (s, 0, 0)),
                  pl.BlockSpec((None, hpad, D_A), lambda s: (s, 0, 0)),
                  _resident(amix.shape), _resident((1, D_A))],
        out_specs=pl.BlockSpec((None, t_len, D_A), lambda s: (s, 0, 0)),
        out_shape=jax.ShapeDtypeStruct((n_seq, t_len, D_A), BF16 if t_len % 16 == 0 else F32),
        scratch_shapes=[pltpu.VMEM((hpad + t_len, D_A), F32)],
        compiler_params=_params("arbitrary"),
    )(u.reshape(n_seq, t_len, D_A), hist16, amix, ascale)
    return out.reshape(n_seq * t_len, D_A)


def _diff_finalize(acc, l, lam, sg, lam_init, t):
    o = acc[0:t] / l[0:t] - lam * (acc[t:2 * t] / l[t:2 * t])
    o = o * lax.rsqrt(jnp.mean(o * o, axis=-1, keepdims=True) + RMS_EPS) * sg
    return o * (1.0 - lam_init)


def _diffattn_prompt_kernel(lam_init, tc, q_ref, k_ref, vt_ref, blam_ref, sgt_ref, o_ref, m_scr, l_scr, acc_scr):
    seq = q_ref.shape[0]
    tk = vt_ref.shape[2]
    hw = 2 * HD_B
    heads = [slice(h * hw, (h + 1) * hw) for h in range(H_B)]
    lam = _diff_lambda(blam_ref[...], lam_init)
    sgt = sgt_ref[...]
    shape = (tk, 2 * tc)
    key = lax.broadcasted_iota(jnp.int32, shape, 0)
    col = lax.broadcasted_iota(jnp.int32, shape, 1)
    qcol = jnp.where(col >= tc, col - tc, col)

    def scores(j, h, qq):
        off = pl.multiple_of(j * tk, tk)
        return _dot_nt(k_ref[pl.ds(off, tk), heads[h]], qq)

    def chunk(c, carry):
        q0 = pl.multiple_of(c * tc, tc)
        jd = q0 // tk
        qqs = [_split_heads_stack(q_ref[pl.ds(q0, tc), sl], HD_B) for sl in heads]

        causal = (jd * tk + key) <= (q0 + qcol)
        ss = [scores(jd, h, qqs[h]) for h in range(H_B)]
        ps = []
        for h in range(H_B):
            s = jnp.where(causal, ss[h], -jnp.inf)
            m = jnp.max(s, axis=0, keepdims=True)
            p = jnp.exp2(s - m)
            m_scr[h] = m
            l_scr[h] = jnp.sum(p, axis=0, keepdims=True)
            ps.append(p.astype(BF16))
        for h, sl in enumerate(heads):
            acc_scr[h] = _dot(vt_ref[jd, sl, :], ps[h])

        def body(j, inner):
            ss = [scores(j, h, qqs[h]) for h in range(H_B)]
            ps = []
            for h in range(H_B):
                m_prev = m_scr[h]
                m_new = jnp.maximum(m_prev, jnp.max(ss[h], axis=0, keepdims=True))
                alpha = jnp.exp2(m_prev - m_new)
                p = jnp.exp2(ss[h] - m_new)
                l_scr[h] = alpha * l_scr[h] + jnp.sum(p, axis=0, keepdims=True)
                m_scr[h] = m_new
                ps.append((alpha, p.astype(BF16)))
            for h, sl in enumerate(heads):
                alpha, p = ps[h]
                acc_scr[h] = alpha * acc_scr[h] + _dot(vt_ref[j, sl, :], p)
            return inner

        lax.fori_loop(0, jd, body, 0)
        for h, sl in enumerate(heads):
            o = acc_scr[h] / l_scr[h]
            o = o[:, 0:tc] - lam * o[:, tc:2 * tc]
            o = o * lax.rsqrt(jnp.mean(o * o, axis=0, keepdims=True) + RMS_EPS) * sgt * (1.0 - lam_init)
            o_ref[pl.ds(q0, tc), sl] = o.T.astype(o_ref.dtype)
        return carry

    lax.fori_loop(0, seq // tc, chunk, 0)


def _diffattn_prompt(q, k, vt, blam, sg, lam_init, n_batch, seq, tc):
    hw = 2 * HD_B
    tk = vt.shape[2]
    whole = pl.BlockSpec((seq, D_B), lambda b: (b, 0))
    return pl.pallas_call(
        functools.partial(_diffattn_prompt_kernel, lam_init, tc),
        grid=(n_batch,),
        in_specs=[whole, whole, pl.BlockSpec((seq // tk, D_B, tk), lambda b: (b, 0, 0)),
                  _resident(blam.shape), _resident((hw, 1))],
        out_specs=whole,
        out_shape=jax.ShapeDtypeStruct((n_batch * seq, D_B), BF16),
        scratch_shapes=[pltpu.VMEM((H_B, 1, 2 * tc), F32), pltpu.VMEM((H_B, 1, 2 * tc), F32),
                        pltpu.VMEM((H_B, hw, 2 * tc), F32)],
        compiler_params=_params("arbitrary"),
    )(q, k, vt, blam, sg.reshape(hw, 1))


def _online_softmax(rows, scores, valid, m_scr, l_scr):
    steps = []
    for r, s in zip(rows, scores):
        m_prev = m_scr[r]
        m_new = jnp.maximum(m_prev, jnp.max(s, axis=-1, keepdims=True))
        alpha = jnp.exp2(m_prev - m_new)
        p = jnp.exp2(s - m_new)
        if valid is not None:
            p = jnp.where(valid, p, 0.0)
        l_scr[r] = alpha * l_scr[r] + jnp.sum(p, axis=-1, keepdims=True)
        m_scr[r] = m_new
        steps.append((alpha, p.astype(BF16)))
    return steps


def _online_accumulate(rows, steps, values, acc_scr):
    for r, v, (alpha, p) in zip(rows, values, steps):
        acc_scr[r] = alpha * acc_scr[r] + _dot(p, v)


def _online_update(rows, scores, values, valid, m_scr, l_scr, acc_scr):
    _online_accumulate(rows, _online_softmax(rows, scores, valid, m_scr, l_scr), values, acc_scr)


def _init_online(m_scr, l_scr, acc_scr):
    m_scr[...] = jnp.full(m_scr.shape, NEG_BIG, F32)
    l_scr[...] = jnp.zeros(l_scr.shape, F32)
    acc_scr[...] = jnp.zeros(acc_scr.shape, F32)


def _decode_queries(q_ref):
    t_new = q_ref.shape[0]
    hw = 2 * HD_B
    q = q_ref[...].astype(BF16)
    qs = [_split_heads_stack(q[:, h * hw:(h + 1) * hw], HD_B) for h in range(H_B)]
    return qs, [slice(2 * t_new * h, 2 * t_new * (h + 1)) for h in range(H_B)]


def _decode_page_heads(page_refs, first_row):
    return jnp.concatenate([r[pl.ds(first_row, PAGE_SIZE, stride=2 * H_B), :] for r in page_refs],
                           axis=0).astype(BF16)


def _decode_finish(lam_init, qs, head_rows, kvnew_ref, blam_ref, sg_ref, o_ref, m_scr, l_scr, acc_scr, pad_scr):
    t_new = kvnew_ref.shape[0]
    hw = 2 * HD_B
    pad_scr[...] = jnp.zeros(pad_scr.shape, F32)
    pad_scr[0:t_new, :] = kvnew_ref[...]
    shape = (2 * t_new, pad_scr.shape[0])
    row = lax.broadcasted_iota(jnp.int32, shape, 0)
    col = lax.broadcasted_iota(jnp.int32, shape, 1)
    valid = col <= jnp.where(row >= t_new, row - t_new, row)
    scores = [jnp.where(valid, _dot_nt(qs[h], pad_scr[:, h * hw:(h + 1) * hw].astype(BF16)), NEG_BIG)
              for h in range(H_B)]
    values = [pad_scr[:, D_B + h * hw:D_B + (h + 1) * hw].astype(BF16) for h in range(H_B)]
    _online_update(head_rows, scores, values, valid, m_scr, l_scr, acc_scr)
    lam = _diff_lambda(blam_ref[...], lam_init)
    for h, rows in enumerate(head_rows):
        o = _diff_finalize(acc_scr[rows], l_scr[rows], lam, sg_ref[...], lam_init, t_new)
        o_ref[:, h * hw:(h + 1) * hw] = o.astype(o_ref.dtype)


def _diffattn_decode_kernel(pages_per_step, lam_init, pt_ref, q_ref, kvnew_ref, blam_ref, sg_ref, *rest):
    page_refs = rest[:pages_per_step]
    o_ref, m_scr, l_scr, acc_scr, pad_scr = rest[pages_per_step:]
    j = pl.program_id(1)
    qs, head_rows = _decode_queries(q_ref)

    @pl.when(j == 0)
    def _():
        _init_online(m_scr, l_scr, acc_scr)

    scores = [_dot_nt(qs[h], _decode_page_heads(page_refs, h)) for h in range(H_B)]
    values = [_decode_page_heads(page_refs, H_B + h) for h in range(H_B)]
    _online_update(head_rows, scores, values, None, m_scr, l_scr, acc_scr)

    @pl.when(j == pl.num_programs(1) - 1)
    def _():
        _decode_finish(lam_init, qs, head_rows, kvnew_ref, blam_ref, sg_ref, o_ref, m_scr, l_scr, acc_scr, pad_scr)


def _diffattn_decode(page_table, cache, q, kvnew, blam, sg, lam_init, n_batch, t_new, pages_per_step):
    n_pages = page_table.shape[1]
    hw = 2 * HD_B
    page_specs = [
        pl.BlockSpec((None, PAGE_SIZE * 2 * H_B, hw), functools.partial(
            lambda b, j, pt, t: (pt[b, j * pages_per_step + t], 0, 0), t=t))
        for t in range(pages_per_step)]
    grid_spec = pltpu.PrefetchScalarGridSpec(
        num_scalar_prefetch=1,
        grid=(n_batch, n_pages // pages_per_step),
        in_specs=[pl.BlockSpec((None, t_new, D_B), lambda b, j, pt: (b, 0, 0)),
                  pl.BlockSpec((None, t_new, 2 * D_B), lambda b, j, pt: (b, 0, 0)),
                  pl.BlockSpec(blam.shape, lambda b, j, pt: (0, 0)),
                  pl.BlockSpec((1, hw), lambda b, j, pt: (0, 0))] + page_specs,
        out_specs=pl.BlockSpec((None, t_new, D_B), lambda b, j, pt: (b, 0, 0)),
        scratch_shapes=[pltpu.VMEM((2 * t_new * H_B, 1), F32), pltpu.VMEM((2 * t_new * H_B, 1), F32),
                        pltpu.VMEM((2 * t_new * H_B, hw), F32), pltpu.VMEM((PAGE_SIZE, 2 * D_B), F32)])
    out = pl.pallas_call(
        functools.partial(_diffattn_decode_kernel, pages_per_step, lam_init),
        grid_spec=grid_spec,
        out_shape=jax.ShapeDtypeStruct((n_batch, t_new, D_B), F32),
        compiler_params=_params("arbitrary", "arbitrary"),
    )(page_table, q.reshape(n_batch, t_new, D_B), kvnew.reshape(n_batch, t_new, 2 * D_B), blam, sg,
      *([cache] * pages_per_step))
    return out.reshape(n_batch * t_new, D_B)


def _post_kernel(n_groups, ff_chunk, has_final, x_ref, first_ref, *rest):
    n_o = max(n_groups, 1)
    o_refs = rest[:n_o]
    lse_refs = rest[n_o:n_o + n_groups]
    rest = rest[n_o + n_groups:]
    wout_ref, g_ref, wup_ref, wdn_ref = rest[:4]
    rest = rest[4:]
    gf_ref = rest[0] if has_final else None
    o_ref = rest[-1]
    if not n_groups:
        second = o_refs[0][...].astype(BF16)
    else:
        merged = []
        for c in range(o_refs[0].shape[0]):
            lses = [r[c] for r in lse_refs]
            mx = functools.reduce(jnp.maximum, lses)
            es = [jnp.exp(l - mx) for l in lses]
            num = functools.reduce(lambda a, b: a + b, [e * r[c] for e, r in zip(es, o_refs)])
            merged.append((num / functools.reduce(lambda a, b: a + b, es)).astype(BF16))
        second = jnp.concatenate(merged, axis=-1)
    mix = jnp.concatenate([first_ref[...].astype(BF16), second], axis=-1)
    x1 = x_ref[...] + _dot(mix, wout_ref[...])
    h = _rmsnorm(x1, g_ref[...]).astype(BF16)
    acc = x1
    up = _dot(h, wup_ref[:, 0:ff_chunk])
    for c in range(0, D_FF, ff_chunk):
        nxt = _dot(h, wup_ref[:, c + ff_chunk:c + 2 * ff_chunk]) if c + ff_chunk < D_FF else None
        act = jnp.square(jnp.maximum(up, 0.0)).astype(BF16)
        acc = acc + _dot(act, wdn_ref[c:c + ff_chunk, :])
        up = nxt
    if has_final:
        acc = _rmsnorm(acc, gf_ref[...])
    o_ref[...] = acc


def _post_mixer(x, first, o_parts, lse_parts, wout, g, wup, wdn, gf, tm):
    m = x.shape[0]
    row = lambda n: pl.BlockSpec((tm, n), lambda i: (i, 0))

    def slab_spec(a):
        tiles_per_seq = a.shape[2] // tm
        return pl.BlockSpec((None, a.shape[1], tm, LANES), lambda i: (i // tiles_per_seq, 0, i % tiles_per_seq, 0))

    has_final = gf is not None
    args = [x, first, *o_parts, *lse_parts, wout, g, wup, wdn]
    specs = [row(D_MODEL), row(first.shape[1])]
    specs += [slab_spec(a) for a in (*o_parts, *lse_parts)] if lse_parts else [row(o_parts[0].shape[1])]
    specs += [_resident(wout.shape), _resident((1, D_MODEL)), _resident(wup.shape), _resident(wdn.shape)]
    if has_final:
        args.append(gf)
        specs.append(_resident((1, D_MODEL)))
    return pl.pallas_call(
        functools.partial(_post_kernel, len(lse_parts), 1024, has_final),
        grid=(m // tm,),
        in_specs=specs,
        out_specs=row(D_MODEL),
        out_shape=jax.ShapeDtypeStruct((m, D_MODEL), F32),
        compiler_params=_params("arbitrary"),
    )(*args)


def _proj_cd_kernel(tiles_per_seq, tails, x_ref, g_ref, w_ref, cw_ref, hist_ref, *rest):
    n_g = len(D_WINDOWS)
    c_ref, ztail_ref = rest[0], rest[1]
    group_refs = rest[2:2 + 4 * n_g] if tiles_per_seq else rest[2:2 + 2 * n_g]
    carry_ref, stream_scr = rest[-2], rest[-1]
    tm = x_ref.shape[0]
    i = pl.program_id(0)
    h = _rmsnorm(x_ref[...], g_ref[...]).astype(BF16)
    gates = _dot(h, w_ref[:, 0:3 * D_C])
    b_gate = gates[:, 0:D_C]
    z = gates[:, D_C:2 * D_C] * gates[:, 2 * D_C:3 * D_C]
    row = lax.broadcasted_iota(jnp.int32, z.shape, 0)
    if tiles_per_seq:
        first = (i % tiles_per_seq) == 0
        prev = jnp.where(first, hist_ref[...], carry_ref[...])
        z1 = jnp.where(row == 0, prev[7:8], pltpu.roll(z, 1, 0))
        z2 = jnp.where(row == 0, prev[6:7], jnp.where(row == 1, prev[7:8], pltpu.roll(z, 2, 0)))
        carry_ref[...] = z[tm - SUBLANES:tm]
        ztail_ref[...] = z[tm - SUBLANES:tm]
    else:
        e = hist_ref[...]
        t = row & (SUBLANES - 1)
        z1 = jnp.where(t == 0, pltpu.roll(e, tm - 7, 0), pltpu.roll(z, 1, 0))
        z2 = jnp.where(t < 2, pltpu.roll(e, tm - 6, 0), pltpu.roll(z, 2, 0))
        ztail_ref[...] = z
    cw = cw_ref[...]
    cv = z2 * cw[0:1] + z1 * cw[1:2] + z * cw[2:3]
    c_ref[...] = (b_gate * cv).astype(c_ref.dtype)
    for gi in range(n_g):
        off = 3 * D_C + 3 * gi * D_DG
        res = _dot(h, w_ref[:, off:off + 3 * D_DG])
        q = res[:, 0:D_DG] * QK_SCALE_D
        if tiles_per_seq:
            q_ref, k_ref, v_ref, kvt_ref = group_refs[4 * gi:4 * gi + 4]
            dil = D_WINDOWS[gi][1]
            _store_streams(q_ref, q, dil, stream_scr)
            _store_streams(k_ref, res[:, D_DG:2 * D_DG], dil, stream_scr)
            _store_streams(v_ref, res[:, 2 * D_DG:3 * D_DG], dil, stream_scr)
            kvt_ref[...] = res[tm - tails[gi]:tm, D_DG:3 * D_DG].T
        else:
            q_ref, kvt_ref = group_refs[2 * gi:2 * gi + 2]
            q_ref[...] = q
            kvt_ref[...] = res[:, D_DG:3 * D_DG].T


def _proj_cd_prompt(x, g, w, cw, n_batch, seq, tm):
    m = x.shape[0]
    tps = seq // tm
    row = lambda n, dt=None: pl.BlockSpec((tm, n), lambda i: (i, 0))
    hist = jnp.zeros((n_batch * SUBLANES, D_C), F32)
    out_shape = [jax.ShapeDtypeStruct((m, D_C), BF16), jax.ShapeDtypeStruct((m // tm * SUBLANES, D_C), F32)]
    out_specs = [row(D_C), pl.BlockSpec((SUBLANES, D_C), lambda i: (i, 0))]
    tails = []
    for win, dil in D_WINDOWS:
        keep = min(win, seq)
        tail = min(keep, tm)
        first_kept = tps - keep // tail
        tails.append(tail)
        out_shape += [jax.ShapeDtypeStruct((n_batch, seq // dil, dil * D_DG), BF16)] * 3
        out_shape += [jax.ShapeDtypeStruct((n_batch, 2 * D_DG, keep), F32)]
        out_specs += [pl.BlockSpec((None, tm // dil, dil * D_DG), lambda i: (i // tps, i % tps, 0))] * 3
        out_specs += [pl.BlockSpec((None, 2 * D_DG, tail), functools.partial(
            lambda i, fk: (i // tps, 0, jnp.maximum(i % tps - fk, 0)), fk=first_kept))]
    return pl.pallas_call(
        functools.partial(_proj_cd_kernel, tps, tuple(tails)),
        grid=(m // tm,),
        in_specs=[row(D_MODEL), _resident((1, D_MODEL)), _resident(w.shape),
                  _resident(cw.shape), pl.BlockSpec((SUBLANES, D_C), lambda i: (i // tps, 0))],
        out_specs=out_specs,
        out_shape=out_shape,
        scratch_shapes=[pltpu.VMEM((SUBLANES, D_C), F32), pltpu.VMEM((D_DG // LANES, tm, LANES), F32)],
        compiler_params=_params("arbitrary"),
    )(x, g, w, cw, hist)


def _proj_cd_sample(x, g, w, cw, hist_rows):
    m = x.shape[0]
    full = lambda r, n: pl.BlockSpec((r, n), lambda i: (0, 0))
    out_shape = [jax.ShapeDtypeStruct((m, D_C), F32), jax.ShapeDtypeStruct((m, D_C), F32)]
    out_specs = [full(m, D_C), full(m, D_C)]
    for _ in D_WINDOWS:
        out_shape += [jax.ShapeDtypeStruct((m, D_DG), F32), jax.ShapeDtypeStruct((2 * D_DG, m), F32)]
        out_specs += [full(m, D_DG), full(2 * D_DG, m)]
    return pl.pallas_call(
        functools.partial(_proj_cd_kernel, 0, ()),
        grid=(1,),
        in_specs=[full(m, D_MODEL), _resident((1, D_MODEL)), _resident(w.shape),
                  _resident(cw.shape), full(m, D_C)],
        out_specs=out_specs,
        out_shape=out_shape,
        scratch_shapes=[pltpu.VMEM((SUBLANES, D_C), F32), pltpu.VMEM((D_DG // LANES, SUBLANES, LANES), F32)],
        compiler_params=_params("arbitrary"),
    )(x, g, w, cw, hist_rows)


def _dilated_prompt_kernel(n_streams, dil, q_ref, kc_ref, kp_ref, vc_ref, vp_ref, o_ref, lse_ref):
    bl = N_BACK
    n_qblk = q_ref.shape[0] // bl
    first_blk = pl.program_id(2) * n_qblk
    shape = (bl, 2 * bl)
    row = lax.broadcasted_iota(jnp.int32, shape, 0)
    col = lax.broadcasted_iota(jnp.int32, shape, 1)
    dist = row - col + bl
    band = (dist >= 0) & (dist <= N_BACK)
    hw = 2 * HD_D
    lane = lax.broadcasted_iota(jnp.int32, (bl, hw), 1)
    for r in range(n_streams):
        for qb in range(n_qblk):
            rows = slice(qb * bl, (qb + 1) * bl)
            prev_rows = slice((qb - 1) * bl, qb * bl)
            valid = (band & ((first_blk * bl + col - bl) >= 0)) if qb == 0 else band
            pairs = [slice(r * D_DG + p * hw, r * D_DG + (p + 1) * hw) for p in range(H_DG // 2)]
            scores = []
            for sl in pairs:
                qq = _split_heads_stack(q_ref[rows, sl], HD_D)
                k_prev = kp_ref[:, sl] if qb == 0 else kc_ref[prev_rows, sl]
                kcat = jnp.concatenate([k_prev, kc_ref[rows, sl]], axis=0)
                scores.append([_dot_nt(qq[e * bl:(e + 1) * bl], kcat) for e in range(2)])
            probs = []
            for pair_scores in scores:
                stats = []
                for s in pair_scores:
                    s = jnp.where(valid, s, -jnp.inf)
                    m = jnp.max(s, axis=-1, keepdims=True)
                    pr = jnp.exp2(s - m)
                    l = jnp.sum(pr, axis=-1, keepdims=True)
                    stats.append((pr.astype(BF16), l, (m + jnp.log2(l)) * LN_2))
                probs.append(stats)
            first_token = (first_blk + qb) * bl * dil + pl.program_id(1) * n_streams + r
            tokens = pl.ds(first_token, bl, stride=dil) if dil > 1 else pl.ds(first_token, bl)
            for p, (sl, stats) in enumerate(zip(pairs, probs)):
                v_prev = vp_ref[:, sl] if qb == 0 else vc_ref[prev_rows, sl]
                vcat = jnp.concatenate([v_prev, vc_ref[rows, sl]], axis=0)
                outs = [_dot(pr, vcat) / l for pr, l, _ in stats]
                o_ref[p, tokens, :] = jnp.where(lane < HD_D, outs[0], outs[1])
                lse_ref[p, tokens, :] = jnp.where(lane < HD_D, stats[0][2], stats[1][2])


def _dilated_prompt(q, k, v, dil, blocks_per_step):
    n_batch, length, _ = q.shape
    bl = N_BACK
    n_qblk = min(blocks_per_step, length // bl)
    n_streams = min(blocks_per_step // n_qblk, dil)
    rows, width = n_qblk * bl, n_streams * D_DG
    cur = pl.BlockSpec((None, rows, width), lambda b, r, j: (b, j, r))
    prev = pl.BlockSpec((None, bl, width), lambda b, r, j: (b, jnp.maximum(j * n_qblk - 1, 0), r))
    out_shape = (n_batch, D_DG // LANES, length * dil, LANES)
    whole = pl.BlockSpec((None,) + out_shape[1:], lambda b, r, j: (b, 0, 0, 0))
    return pl.pallas_call(
        functools.partial(_dilated_prompt_kernel, n_streams, dil),
        grid=(n_batch, dil // n_streams, length // rows),
        in_specs=[cur, cur, prev, cur, prev],
        out_specs=[whole, whole],
        out_shape=[jax.ShapeDtypeStruct(out_shape, F32), jax.ShapeDtypeStruct(out_shape, F32)],
        compiler_params=_params("arbitrary", "arbitrary", "arbitrary"),
    )(q, k, k, v, v)


def _dilated_sample_kernel(dil, q_ref, cache_ref, kvnew_ref, newc_ref, o_ref, lse_ref):
    n_seq = q_ref.shape[0]
    for s in range(n_seq):
        _dilated_sample_one(dil, pl.program_id(0) * n_seq + s, q_ref.at[s], cache_ref.at[s], kvnew_ref,
                            newc_ref.at[s], o_ref.at[s], lse_ref.at[s])


def _dilated_sample_one(dil, seq, q_ref, cache_ref, kvnew_ref, newc_ref, o_ref, lse_ref):
    buf_len = cache_ref.shape[1]
    t_new = q_ref.shape[0]
    hw = 2 * HD_D
    n_blk = buf_len // LANES

    seq_in_block = seq % (LANES // t_new)
    new_tile = pltpu.roll(kvnew_ref[...], (LANES - seq_in_block * t_new) % LANES, 1)

    lane = lax.broadcasted_iota(jnp.int32, (cache_ref.shape[0], LANES), 1)
    cur = pltpu.roll(cache_ref[:, 0:LANES], LANES - t_new, 1)
    for c in range(n_blk):
        following = cache_ref[:, (c + 1) * LANES:(c + 2) * LANES] if c + 1 < n_blk else new_tile
        nxt = pltpu.roll(following, LANES - t_new, 1)
        newc_ref[:, c * LANES:(c + 1) * LANES] = jnp.where(lane < LANES - t_new, cur, nxt)
        cur = nxt

    def iotas(n_keys):
        shape = (2 * t_new, n_keys)
        row = lax.broadcasted_iota(jnp.int32, shape, 0)
        return row & (t_new - 1), lax.broadcasted_iota(jnp.int32, shape, 1)

    qi, col = iotas(buf_len)
    d = buf_len + qi - col
    valid_buf = ((d & (dil - 1)) == 0) & (d <= N_BACK * dil)
    qi, col = iotas(LANES)
    d = qi - col
    valid_new = (col < t_new) & (d >= 0) & ((d & (dil - 1)) == 0)

    q = q_ref[...].astype(BF16)
    lane = lax.broadcasted_iota(jnp.int32, (t_new, hw), 1)
    for p in range(H_DG // 2):
        k_rows, v_rows = slice(p * hw, (p + 1) * hw), slice(D_DG + p * hw, D_DG + (p + 1) * hw)
        qq = _split_heads_stack(q[:, k_rows], HD_D)
        s_buf = jnp.where(valid_buf, _dot(qq, cache_ref[k_rows, :].astype(BF16)), NEG_BIG)
        s_new = jnp.where(valid_new, _dot(qq, new_tile[k_rows].astype(BF16)), NEG_BIG)
        m = jnp.maximum(jnp.max(s_buf, axis=-1, keepdims=True), jnp.max(s_new, axis=-1, keepdims=True))
        p_buf = jnp.where(valid_buf, jnp.exp2(s_buf - m), 0.0)
        p_new = jnp.where(valid_new, jnp.exp2(s_new - m), 0.0)
        l = jnp.sum(p_buf, axis=-1, keepdims=True) + jnp.sum(p_new, axis=-1, keepdims=True)
        acc = (_dot_nt(p_buf.astype(BF16), cache_ref[v_rows, :].astype(BF16))
               + _dot_nt(p_new.astype(BF16), new_tile[v_rows].astype(BF16)))
        o = acc / l
        lse = (m + jnp.log2(l)) * LN_2
        o_ref[:, k_rows] = jnp.where(lane < HD_D, o[0:t_new], o[t_new:2 * t_new])
        lse_ref[:, k_rows] = jnp.where(lane < HD_D, lse[0:t_new], lse[t_new:2 * t_new])


def _dilated_sample(q, cache_t, kvnew_t, dil, n_batch, t_new, seqs_per_step):
    buf_len = cache_t.shape[2]
    steps_per_lane_block = LANES // t_new // seqs_per_step
    small = lambda n: pl.BlockSpec((seqs_per_step, t_new, n), lambda b: (b, 0, 0))
    whole = pl.BlockSpec((seqs_per_step, 2 * D_DG, buf_len), lambda b: (b, 0, 0))
    newc, o, lse = pl.pallas_call(
        functools.partial(_dilated_sample_kernel, dil),
        grid=(n_batch // seqs_per_step,),
        in_specs=[small(D_DG), whole, pl.BlockSpec((2 * D_DG, LANES), lambda b: (0, b // steps_per_lane_block))],
        out_specs=[whole, small(D_DG), small(D_DG)],
        out_shape=[jax.ShapeDtypeStruct(cache_t.shape, F32),
                   jax.ShapeDtypeStruct((n_batch, t_new, D_DG), F32),
                   jax.ShapeDtypeStruct((n_batch, t_new, D_DG), F32)],
        compiler_params=_params("arbitrary"),
    )(q.reshape(n_batch, t_new, D_DG), cache_t, kvnew_t)
    return newc, o.reshape(n_batch * t_new, D_DG), lse.reshape(n_batch * t_new, D_DG)


def kernel(x_prompt, x_sample, state_a_pool, cache_b_kv, state_c_conv, cache_d0_kv, cache_d1_kv, cache_d2_kv, page_table, norm_mix_g, norm_mlp_g, norm_out_g, w_in_ab, w_out_ab, a_mix, a_scale, b_lam, b_subln_g, w_in_cd, w_out_cd, c_conv_w, w_up, w_down):
    bp, seq, _ = x_prompt.shape
    bs, t_new, _ = x_sample.shape
    n_pages = page_table.shape[1]
    past_len = n_pages * cache_b_kv.shape[2]
    mp, ms = bp * seq, bs * t_new
    tm_p = 512
    xp = x_prompt.reshape(mp, D_MODEL)
    xs = x_sample.reshape(ms, D_MODEL)
    g2 = lambda v: v.reshape(1, -1)
    d_caches = (cache_d0_kv, cache_d1_kv, cache_d2_kv)

    lam_init = 0.8 - 0.6 * math.exp(-0.3 * 0)
    w_in = w_in_ab[0].astype(BF16)
    w_out = w_out_ab[0].astype(BF16)
    wu, wd = w_up[0].astype(BF16), w_down[0].astype(BF16)
    amix = a_mix[0].astype(BF16)
    gm, gl = g2(norm_mix_g[0]), g2(norm_mlp_g[0])
    ascale, sg = g2(a_scale[0]), g2(b_subln_g[0])

    u_p, q_p, kv_p, k_p, vt_p = _proj_ab(xp, gm, w_in, tm_p, True, 512)
    hist_p = jnp.zeros((bp, POOL_HIST + 1, D_A), F32)
    a_p = _pool_mix(u_p, hist_p, amix, ascale, 0, bp, seq)
    ob_p = _diffattn_prompt(q_p, k_p, vt_p, b_lam[0], sg, lam_init, bp, seq, 256)

    u_s, q_s, kv_s, _, _ = _proj_ab(xs, gm, w_in, ms, False, 256)
    hist_s = jnp.pad(state_a_pool[0], ((0, 0), (1, 0), (0, 0)))
    a_s = _pool_mix(u_s, hist_s, amix, ascale, past_len, bs, t_new)
    pool_pages = cache_b_kv[0].reshape(cache_b_kv.shape[1], PAGE_SIZE * 2 * H_B, 2 * HD_B)
    xp = _post_mixer(xp, a_p, (ob_p,), (), w_out, gl, wu, wd, None, tm_p)
    ob_s = _diffattn_decode(page_table, pool_pages, q_s, kv_s, b_lam[0], sg, lam_init, bs, t_new,
                            min(32, n_pages))
    xs = _post_mixer(xs, a_s, (ob_s,), (), w_out, gl, wu, wd, None, ms)

    u_p3 = u_p.reshape(bp, seq, D_A)
    new_pool_p = u_p3[:, seq - POOL_HIST:][None]
    new_pool_s = jnp.concatenate([state_a_pool[0], u_s.reshape(bs, t_new, D_A)], axis=1)[:, -POOL_HIST:][None]
    new_bkv_p = kv_p.reshape(1, bp, seq, 2, H_B, 2 * HD_B)
    new_bkv_s = kv_s.reshape(1, bs, t_new, 2, H_B, 2 * HD_B)

    w_in = w_in_cd[0].astype(BF16)
    w_out = w_out_cd[0].astype(BF16)
    wu, wd = w_up[1].astype(BF16), w_down[1].astype(BF16)
    gm, gl, gf = g2(norm_mix_g[1]), g2(norm_mlp_g[1]), g2(norm_out_g)
    cw = c_conv_w[0]

    def to_slabs(a):
        return jnp.transpose(a.reshape(a.shape[0], a.shape[1] // LANES, LANES), (1, 0, 2))[None]

    def from_position_minor(a):
        a = a.reshape(a.shape[0], 2, H_DG, HD_D, a.shape[2])
        return jnp.transpose(a, (0, 4, 1, 2, 3))[None]

    outs = _proj_cd_prompt(xp, gm, w_in, cw, bp, seq, tm_p)
    c_p, ztail_p = outs[0], outs[1]
    o_parts, lse_parts, new_d_p = [], [], []
    for gi, (win, dil) in enumerate(D_WINDOWS):
        q_g, k_g, v_g, kvt_g = outs[2 + 4 * gi:6 + 4 * gi]
        o_g, lse_g = _dilated_prompt(q_g, k_g, v_g, dil, 4)
        o_parts.append(o_g)
        lse_parts.append(lse_g)
        new_d_p.append(from_position_minor(kvt_g))
    y_p = _post_mixer(xp, c_p, o_parts, lse_parts, w_out, gl, wu, wd, gf, tm_p)
    tiles_per_seq = seq // tm_p
    new_conv_p = ztail_p.reshape(bp, tiles_per_seq, SUBLANES, D_C)[:, -1, SUBLANES - (CONV_W - 1):][None]

    hist_rows = jnp.pad(state_c_conv[0], ((0, 0), (t_new - (CONV_W - 1), 0), (0, 0))).reshape(ms, D_C)
    outs = _proj_cd_sample(xs, gm, w_in, cw, hist_rows)
    c_s, z_s = outs[0], outs[1]
    o_parts, lse_parts, new_d_s = [], [], []
    for gi, (win, dil) in enumerate(D_WINDOWS):
        q_g, kvt_g = outs[2 + 2 * gi:4 + 2 * gi]
        cache = d_caches[gi][0]
        buf_len = cache.shape[1]
        cache_t = jnp.transpose(cache, (0, 2, 3, 4, 1)).reshape(bs, 2 * D_DG, buf_len)
        seqs_per_step = max(1, min(SUBLANES, D_WINDOWS[-1][0] // (2 * buf_len)))
        newc, o_g, lse_g = _dilated_sample(q_g, cache_t, kvt_g, dil, bs, t_new, seqs_per_step)
        o_parts.append(to_slabs(o_g))
        lse_parts.append(to_slabs(lse_g))
        new_d_s.append(from_position_minor(newc))
    y_s = _post_mixer(xs, c_s, o_parts, lse_parts, w_out, gl, wu, wd, gf, ms)
    new_conv_s = z_s.reshape(bs, t_new, D_C)[:, t_new - (CONV_W - 1):][None]

    return (y_p.reshape(bp, seq, D_MODEL), y_s.reshape(bs, t_new, D_MODEL),
            new_pool_p, new_pool_s, new_bkv_p, new_bkv_s, new_conv_p, new_conv_s,
            new_d_p[0], new_d_s[0], new_d_p[1], new_d_s[1], new_d_p[2], new_d_s[2])
```

```python
import functools
import math

import jax
import jax.numpy as jnp
from jax import lax
from jax.experimental import pallas as pl
from jax.experimental.pallas import tpu as pltpu

F32 = jnp.float32
BF16 = jnp.bfloat16

D_MODEL = 1024
RMS_EPS = 1e-6
D_A = 512
POOL_WINDOWS = (2, 4, 8, 16)
POOL_GROUP = 128
POOL_HIST = 15
H_B = 4
HD_B = 64
D_B = 512
D_C = 512
CONV_W = 3
D_WINDOWS = ((128, 1), (512, 4), (2048, 16))
N_BACK = 128
H_DG = 8
HD_D = 64
D_DG = 512
D_FF = 4096
PAGE_SIZE = 128

LANES = 128
SUBLANES = 8
VMEM_LIMIT_BYTES = 56 * 1024 * 1024
NEG_BIG = -1e30

LOG2_E = math.log2(math.e)
LN_2 = math.log(2.0)
QK_SCALE_B = HD_B ** -0.5 * LOG2_E
QK_SCALE_D = HD_D ** -0.5 * LOG2_E


def _params(*sem):
    return pltpu.CompilerParams(dimension_semantics=sem, vmem_limit_bytes=VMEM_LIMIT_BYTES)


def _resident(shape):
    nd = len(shape)
    return pl.BlockSpec(shape, lambda *_: (0,) * nd, pipeline_mode=pl.Buffered(1))


def _rmsnorm(x, g):
    return x * lax.rsqrt(jnp.mean(x * x, axis=-1, keepdims=True) + RMS_EPS) * g


def _dot(a, b):
    return jnp.dot(a, b, preferred_element_type=F32)


def _dot_nt(a, b):
    return lax.dot_general(a, b, (((1,), (1,)), ((), ())), preferred_element_type=F32)


def _split_heads_stack(q, half):
    lane = lax.broadcasted_iota(jnp.int32, q.shape, 1)
    zero = jnp.zeros_like(q)
    return jnp.concatenate([jnp.where(lane < half, q, zero), jnp.where(lane >= half, q, zero)], axis=0)


def _lane_blocks(n):
    return [slice(c * LANES, (c + 1) * LANES) for c in range(n // LANES)]


def _store_streams(ref, val, dil, scr):
    if dil == 1:
        ref[...] = val.astype(ref.dtype)
        return
    width = val.shape[1]
    for c, sl in enumerate(_lane_blocks(width)):
        scr[c] = val[:, sl]
    for r in range(dil):
        for c, sl in enumerate(_lane_blocks(width)):
            ref[:, r * width + sl.start:r * width + sl.stop] = (
                scr[c, pl.ds(r, ref.shape[0], stride=dil), :].astype(ref.dtype))


def _rows_from_streams(ref, dil, scr):
    width = ref.shape[1] // dil
    if dil == 1:
        return [ref[:, sl].astype(F32) for sl in _lane_blocks(width)]
    for r in range(dil):
        for c, sl in enumerate(_lane_blocks(width)):
            scr[c, pl.ds(r, ref.shape[0], stride=dil), :] = ref[:, r * width + sl.start:r * width + sl.stop].astype(F32)
    return [scr[c] for c in range(width // LANES)]


def _diff_lambda(blam, lam_init):
    a = jnp.sum(blam[0:1] * blam[1:2], axis=-1, keepdims=True)
    b = jnp.sum(blam[2:3] * blam[3:4], axis=-1, keepdims=True)
    return jnp.exp(a) - jnp.exp(b) + lam_init


def _proj_ab_kernel(prompt, x_ref, g_ref, w_ref, u_ref, q_ref, kv_ref, k_ref, v_ref):
    h = _rmsnorm(x_ref[...], g_ref[...]).astype(BF16)
    u_ref[...] = _dot(h, w_ref[:, 0:D_A])
    q_ref[...] = (_dot(h, w_ref[:, D_A:D_A + D_B]) * QK_SCALE_B).astype(q_ref.dtype)
    kv = _dot(h, w_ref[:, D_A + D_B:D_A + 3 * D_B])
    k_ref[...] = kv[:, :D_B].astype(BF16)
    if not prompt:
        kv_ref[...] = kv
        v_ref[...] = kv[:, D_B:].astype(BF16)
        return
    n_blk = kv.shape[1] // LANES
    for c, sl in enumerate(_lane_blocks(kv.shape[1])):
        kv_ref[pl.ds(c, kv.shape[0], stride=n_blk), :] = kv[:, sl]
    vt = kv[:, D_B:].T.astype(BF16)
    tk = v_ref.shape[2]
    for j in range(v_ref.shape[0]):
        v_ref[j] = vt[:, j * tk:(j + 1) * tk]


def _proj_ab(x, g, w, tm, prompt, tk):
    m = x.shape[0]
    n_in = w.shape[1]
    row = lambda n: pl.BlockSpec((tm, n), lambda i: (i, 0))
    n_blk = 2 * D_B // LANES
    if prompt:
        kv_shape, kv_spec = (m * n_blk, LANES), pl.BlockSpec((tm * n_blk, LANES), lambda i: (i, 0))
        v_shape, v_spec = (m // tk, D_B, tk), pl.BlockSpec((tm // tk, D_B, tk), lambda i: (i, 0, 0))
    else:
        kv_shape, kv_spec = (m, 2 * D_B), row(2 * D_B)
        v_shape, v_spec = (m, D_B), row(D_B)
    return pl.pallas_call(
        functools.partial(_proj_ab_kernel, prompt),
        grid=(m // tm,),
        in_specs=[row(D_MODEL), _resident((1, D_MODEL)), _resident((D_MODEL, n_in))],
        out_specs=[row(D_A), row(D_B), kv_spec, row(D_B), v_spec],
        out_shape=[jax.ShapeDtypeStruct((m, D_A), F32), jax.ShapeDtypeStruct((m, D_B), BF16 if prompt else F32),
                   jax.ShapeDtypeStruct(kv_shape, F32), jax.ShapeDtypeStruct((m, D_B), BF16),
                   jax.ShapeDtypeStruct(v_shape, BF16)],
        compiler_params=_params("arbitrary"),
    )(x, g, w)


def _pool_kernel(pos0, tchunk, u_ref, hist_ref, amix_ref, ascale_ref, o_ref, ext_ref):
    t_len = u_ref.shape[0]
    hpad = hist_ref.shape[0]
    ext_ref[0:hpad, :] = hist_ref[...]
    ext_ref[hpad:hpad + t_len, :] = u_ref[...]
    for t0 in range(0, t_len, tchunk):
        pos = pos0 + t0 + lax.broadcasted_iota(jnp.int32, (tchunk, 1), 0)
        for gi, w in enumerate(POOL_WINDOWS):
            sl = slice(gi * POOL_GROUP, (gi + 1) * POOL_GROUP)
            u = u_ref[t0:t0 + tchunk, sl]
            acc = u
            for j in range(1, w):
                acc = acc + ext_ref[hpad + t0 - j:hpad + t0 - j + tchunk, sl]
            cnt = jnp.minimum(pos + 1, w).astype(F32)
            p = (acc / cnt - u).astype(BF16)
            y = _dot(p, amix_ref[gi]) * ascale_ref[:, sl]
            o_ref[t0:t0 + tchunk, sl] = y.astype(o_ref.dtype)


def _pool_mix(u, hist16, amix, ascale, pos0, n_seq, t_len):
    hpad = hist16.shape[1]
    tchunk = min(t_len, 256)
    out = pl.pallas_call(
        functools.partial(_pool_kernel, pos0, tchunk),
        grid=(n_seq,),
        in_specs=[pl.BlockSpec((None, t_len, D_A), lambda s: (s, 0, 0)),
                  pl.BlockSpec((None, hpad, D_A), lambda s: (s, 0, 0)),
                  _resident(amix.shape), _resident((1, D_A))],
        out_specs=pl.BlockSpec((None, t_len, D_A), lambda s: (s, 0, 0)),
        out_shape=jax.ShapeDtypeStruct((n_seq, t_len, D_A), BF16 if t_len % 16 == 0 else F32),
        scratch_shapes=[pltpu.VMEM((hpad + t_len, D_A), F32)],
        compiler_params=_params("arbitrary"),
    )(u.reshape(n_seq, t_len, D_A), hist16, amix, ascale)
    return out.reshape(n_seq * t_len, D_A)


def _diff_finalize(acc, l, lam, sg, lam_init, t):
    o = acc[0:t] / l[0:t] - lam * (acc[t:2 * t] / l[t:2 * t])
    o = o * lax.rsqrt(jnp.mean(o * o, axis=-1, keepdims=True) + RMS_EPS) * sg
    return o * (1.0 - lam_init)


def _diffattn_prompt_kernel(lam_init, tc, q_ref, k_ref, vt_ref, blam_ref, sgt_ref, o_ref, m_scr, l_scr, acc_scr):
    seq = q_ref.shape[0]
    tk = vt_ref.shape[2]
    hw = 2 * HD_B
    heads = [slice(h * hw, (h + 1) * hw) for h in range(H_B)]
    lam = _diff_lambda(blam_ref[...], lam_init)
    sgt = sgt_ref[...]
    shape = (tk, 2 * tc)
    key = lax.broadcasted_iota(jnp.int32, shape, 0)
    col = lax.broadcasted_iota(jnp.int32, shape, 1)
    qcol = jnp.where(col >= tc, col - tc, col)

    def scores(j, h, qq):
        off = pl.multiple_of(j * tk, tk)
        return _dot_nt(k_ref[pl.ds(off, tk), heads[h]], qq)

    def chunk(c, carry):
        q0 = pl.multiple_of(c * tc, tc)
        jd = q0 // tk
        qqs = [_split_heads_stack(q_ref[pl.ds(q0, tc), sl], HD_B) for sl in heads]

        causal = (jd * tk + key) <= (q0 + qcol)
        ss = [scores(jd, h, qqs[h]) for h in range(H_B)]
        ps = []
        for h in range(H_B):
            s = jnp.where(causal, ss[h], -jnp.inf)
            m = jnp.max(s, axis=0, keepdims=True)
            p = jnp.exp2(s - m)
            m_scr[h] = m
            l_scr[h] = jnp.sum(p, axis=0, keepdims=True)
            ps.append(p.astype(BF16))
        for h, sl in enumerate(heads):
            acc_scr[h] = _dot(vt_ref[jd, sl, :], ps[h])

        def body(j, inner):
            ss = [scores(j, h, qqs[h]) for h in range(H_B)]
            ps = []
            for h in range(H_B):
                m_prev = m_scr[h]
                m_new = jnp.maximum(m_prev, jnp.max(ss[h], axis=0, keepdims=True))
                alpha = jnp.exp2(m_prev - m_new)
                p = jnp.exp2(ss[h] - m_new)
                l_scr[h] = alpha * l_scr[h] + jnp.sum(p, axis=0, keepdims=True)
                m_scr[h] = m_new
                ps.append((alpha, p.astype(BF16)))
            for h, sl in enumerate(heads):
                alpha, p = ps[h]
                acc_scr[h] = alpha * acc_scr[h] + _dot(vt_ref[j, sl, :], p)
            return inner

        lax.fori_loop(0, jd, body, 0)
        for h, sl in enumerate(heads):
            o = acc_scr[h] / l_scr[h]
            o = o[:, 0:tc] - lam * o[:, tc:2 * tc]
            o = o * lax.rsqrt(jnp.mean(o * o, axis=0, keepdims=True) + RMS_EPS) * sgt * (1.0 - lam_init)
            o_ref[pl.ds(q0, tc), sl] = o.T.astype(o_ref.dtype)
        return carry

    lax.fori_loop(0, seq // tc, chunk, 0)


def _diffattn_prompt(q, k, vt, blam, sg, lam_init, n_batch, seq, tc):
    hw = 2 * HD_B
    tk = vt.shape[2]
    whole = pl.BlockSpec((seq, D_B), lambda b: (b, 0))
    return pl.pallas_call(
        functools.partial(_diffattn_prompt_kernel, lam_init, tc),
        grid=(n_batch,),
        in_specs=[whole, whole, pl.BlockSpec((seq // tk, D_B, tk), lambda b: (b, 0, 0)),
                  _resident(blam.shape), _resident((hw, 1))],
        out_specs=whole,
        out_shape=jax.ShapeDtypeStruct((n_batch * seq, D_B), BF16),
        scratch_shapes=[pltpu.VMEM((H_B, 1, 2 * tc), F32), pltpu.VMEM((H_B, 1, 2 * tc), F32),
                        pltpu.VMEM((H_B, hw, 2 * tc), F32)],
        compiler_params=_params("arbitrary"),
    )(q, k, vt, blam, sg.reshape(hw, 1))


def _online_softmax(rows, scores, valid, m_scr, l_scr):
    steps = []
    for r, s in zip(rows, scores):
        m_prev = m_scr[r]
        m_new = jnp.maximum(m_prev, jnp.max(s, axis=-1, keepdims=True))
        alpha = jnp.exp2(m_prev - m_new)
        p = jnp.exp2(s - m_new)
        if valid is not None:
            p = jnp.where(valid, p, 0.0)
        l_scr[r] = alpha * l_scr[r] + jnp.sum(p, axis=-1, keepdims=True)
        m_scr[r] = m_new
        steps.append((alpha, p.astype(BF16)))
    return steps


def _online_accumulate(rows, steps, values, acc_scr):
    for r, v, (alpha, p) in zip(rows, values, steps):
        acc_scr[r] = alpha * acc_scr[r] + _dot(p, v)


def _online_update(rows, scores, values, valid, m_scr, l_scr, acc_scr):
    _online_accumulate(rows, _online_softmax(rows, scores, valid, m_scr, l_scr), values, acc_scr)


def _init_online(m_scr, l_scr, acc_scr):
    m_scr[...] = jnp.full(m_scr.shape, NEG_BIG, F32)
    l_scr[...] = jnp.zeros(l_scr.shape, F32)
    acc_scr[...] = jnp.zeros(acc_scr.shape, F32)


def _decode_queries(q_ref):
    t_new = q_ref.shape[0]
    hw = 2 * HD_B
    q = q_ref[...].astype(BF16)
    qs = [_split_heads_stack(q[:, h * hw:(h + 1) * hw], HD_B) for h in range(H_B)]
    return qs, [slice(2 * t_new * h, 2 * t_new * (h + 1)) for h in range(H_B)]


def _decode_page_heads(page_refs, first_row):
    return jnp.concatenate([r[pl.ds(first_row, PAGE_SIZE, stride=2 * H_B), :] for r in page_refs],
                           axis=0).astype(BF16)


def _decode_finish(lam_init, qs, head_rows, kvnew_ref, blam_ref, sg_ref, o_ref, m_scr, l_scr, acc_scr, pad_scr):
    t_new = kvnew_ref.shape[0]
    hw = 2 * HD_B
    pad_scr[...] = jnp.zeros(pad_scr.shape, F32)
    pad_scr[0:t_new, :] = kvnew_ref[...]
    shape = (2 * t_new, pad_scr.shape[0])
    row = lax.broadcasted_iota(jnp.int32, shape, 0)
    col = lax.broadcasted_iota(jnp.int32, shape, 1)
    valid = col <= jnp.where(row >= t_new, row - t_new, row)
    scores = [jnp.where(valid, _dot_nt(qs[h], pad_scr[:, h * hw:(h + 1) * hw].astype(BF16)), NEG_BIG)
              for h in range(H_B)]
    values = [pad_scr[:, D_B + h * hw:D_B + (h + 1) * hw].astype(BF16) for h in range(H_B)]
    _online_update(head_rows, scores, values, valid, m_scr, l_scr, acc_scr)
    lam = _diff_lambda(blam_ref[...], lam_init)
    for h, rows in enumerate(head_rows):
        o = _diff_finalize(acc_scr[rows], l_scr[rows], lam, sg_ref[...], lam_init, t_new)
        o_ref[:, h * hw:(h + 1) * hw] = o.astype(o_ref.dtype)


def _diffattn_decode_kernel(pages_per_step, lam_init, pt_ref, q_ref, kvnew_ref, blam_ref, sg_ref, *rest):
    page_refs = rest[:pages_per_step]
    o_ref, m_scr, l_scr, acc_scr, pad_scr = rest[pages_per_step:]
    j = pl.program_id(1)
    qs, head_rows = _decode_queries(q_ref)

    @pl.when(j == 0)
    def _():
        _init_online(m_scr, l_scr, acc_scr)

    scores = [_dot_nt(qs[h], _decode_page_heads(page_refs, h)) for h in range(H_B)]
    values = [_decode_page_heads(page_refs, H_B + h) for h in range(H_B)]
    _online_update(head_rows, scores, values, None, m_scr, l_scr, acc_scr)

    @pl.when(j == pl.num_programs(1) - 1)
    def _():
        _decode_finish(lam_init, qs, head_rows, kvnew_ref, blam_ref, sg_ref, o_ref, m_scr, l_scr, acc_scr, pad_scr)


def _diffattn_decode(page_table, cache, q, kvnew, blam, sg, lam_init, n_batch, t_new, pages_per_step):
    n_pages = page_table.shape[1]
    hw = 2 * HD_B
    page_specs = [
        pl.BlockSpec((None, PAGE_SIZE * 2 * H_B, hw), functools.partial(
            lambda b, j, pt, t: (pt[b, j * pages_per_step + t], 0, 0), t=t))
        for t in range(pages_per_step)]
    grid_spec = pltpu.PrefetchScalarGridSpec(
        num_scalar_prefetch=1,
        grid=(n_batch, n_pages // pages_per_step),
        in_specs=[pl.BlockSpec((None, t_new, D_B), lambda b, j, pt: (b, 0, 0)),
                  pl.BlockSpec((None, t_new, 2 * D_B), lambda b, j, pt: (b, 0, 0)),
                  pl.BlockSpec(blam.shape, lambda b, j, pt: (0, 0)),
                  pl.BlockSpec((1, hw), lambda b, j, pt: (0, 0))] + page_specs,
        out_specs=pl.BlockSpec((None, t_new, D_B), lambda b, j, pt: (b, 0, 0)),
        scratch_shapes=[pltpu.VMEM((2 * t_new * H_B, 1), F32), pltpu.VMEM((2 * t_new * H_B, 1), F32),
                        pltpu.VMEM((2 * t_new * H_B, hw), F32), pltpu.VMEM((PAGE_SIZE, 2 * D_B), F32)])
    out = pl.pallas_call(
        functools.partial(_diffattn_decode_kernel, pages_per_step, lam_init),
        grid_spec=grid_spec,
        out_shape=jax.ShapeDtypeStruct((n_batch, t_new, D_B), F32),
        compiler_params=_params("arbitrary", "arbitrary"),
    )(page_table, q.reshape(n_batch, t_new, D_B), kvnew.reshape(n_batch, t_new, 2 * D_B), blam, sg,
      *([cache] * pages_per_step))
    return out.reshape(n_batch * t_new, D_B)


def _post_kernel(dils, ff_chunk, has_final, x_ref, first_ref, *rest):
    n_g = max(len(dils), 1)
    o_refs = rest[:n_g]
    lse_refs = rest[n_g:n_g + len(dils)]
    rest = rest[n_g + len(dils):]
    wout_ref, g_ref, wup_ref, wdn_ref = rest[:4]
    rest = rest[4:]
    gf_ref = rest[0] if has_final else None
    o_ref = rest[1] if has_final else rest[0]
    scratch = list(rest[2 if has_final else 1:])
    if not dils:
        second = o_refs[0][...].astype(BF16)
    else:
        scr_of = lambda d: scratch.pop(0) if d > 1 else None
        os_ = [_rows_from_streams(r, d, scr_of(d)) for r, d in zip(o_refs, dils)]
        ls_ = [_rows_from_streams(r, d, scr_of(d)) for r, d in zip(lse_refs, dils)]
        merged = []
        for c in range(len(os_[0])):
            lses = [l[c] for l in ls_]
            mx = functools.reduce(jnp.maximum, lses)
            es = [jnp.exp(l - mx) for l in lses]
            num = functools.reduce(lambda a, b: a + b, [e * o[c] for e, o in zip(es, os_)])
            merged.append((num / functools.reduce(lambda a, b: a + b, es)).astype(BF16))
        second = jnp.concatenate(merged, axis=-1)
    mix = jnp.concatenate([first_ref[...].astype(BF16), second], axis=-1)
    x1 = x_ref[...] + _dot(mix, wout_ref[...])
    h = _rmsnorm(x1, g_ref[...]).astype(BF16)
    acc = x1
    up = _dot(h, wup_ref[:, 0:ff_chunk])
    for c in range(0, D_FF, ff_chunk):
        nxt = _dot(h, wup_ref[:, c + ff_chunk:c + 2 * ff_chunk]) if c + ff_chunk < D_FF else None
        act = jnp.square(jnp.maximum(up, 0.0)).astype(BF16)
        acc = acc + _dot(act, wdn_ref[c:c + ff_chunk, :])
        up = nxt
    if has_final:
        acc = _rmsnorm(acc, gf_ref[...])
    o_ref[...] = acc


def _post_mixer(x, first, o_parts, lse_parts, dils, wout, g, wup, wdn, gf, tm):
    m = x.shape[0]
    row = lambda n: pl.BlockSpec((tm, n), lambda i: (i, 0))

    def part_spec(a, dil):
        if dil == 1:
            return row(a.shape[1])
        tiles_per_seq = a.shape[1] * dil // tm
        return pl.BlockSpec((None, tm // dil, a.shape[2]), lambda i: (i // tiles_per_seq, i % tiles_per_seq, 0))

    has_final = gf is not None
    part_dils = tuple(dils) if dils else (1,)
    args = [x, first, *o_parts, *lse_parts, wout, g, wup, wdn]
    specs = [row(D_MODEL), row(first.shape[1])]
    specs += [part_spec(a, d) for a, d in zip(o_parts, part_dils)] + [part_spec(a, d) for a, d in zip(lse_parts, dils)]
    specs += [_resident(wout.shape), _resident((1, D_MODEL)), _resident(wup.shape), _resident(wdn.shape)]
    if has_final:
        args.append(gf)
        specs.append(_resident((1, D_MODEL)))
    n_scratch = 2 * sum(d > 1 for d in dils)
    return pl.pallas_call(
        functools.partial(_post_kernel, tuple(dils), 1024, has_final),
        grid=(m // tm,),
        in_specs=specs,
        out_specs=row(D_MODEL),
        out_shape=jax.ShapeDtypeStruct((m, D_MODEL), F32),
        scratch_shapes=[pltpu.VMEM((D_DG // LANES, tm, LANES), F32)] * n_scratch,
        compiler_params=_params("arbitrary"),
    )(*args)


def _proj_cd_kernel(tiles_per_seq, tails, x_ref, g_ref, w_ref, cw_ref, hist_ref, *rest):
    n_g = len(D_WINDOWS)
    c_ref, ztail_ref = rest[0], rest[1]
    group_refs = rest[2:2 + 4 * n_g] if tiles_per_seq else rest[2:2 + 2 * n_g]
    carry_ref, stream_scr = rest[-2], rest[-1]
    tm = x_ref.shape[0]
    i = pl.program_id(0)
    h = _rmsnorm(x_ref[...], g_ref[...]).astype(BF16)
    gates = _dot(h, w_ref[:, 0:3 * D_C])
    b_gate = gates[:, 0:D_C]
    z = gates[:, D_C:2 * D_C] * gates[:, 2 * D_C:3 * D_C]
    row = lax.broadcasted_iota(jnp.int32, z.shape, 0)
    if tiles_per_seq:
        first = (i % tiles_per_seq) == 0
        prev = jnp.where(first, hist_ref[...], carry_ref[...])
        z1 = jnp.where(row == 0, prev[7:8], pltpu.roll(z, 1, 0))
        z2 = jnp.where(row == 0, prev[6:7], jnp.where(row == 1, prev[7:8], pltpu.roll(z, 2, 0)))
        carry_ref[...] = z[tm - SUBLANES:tm]
        ztail_ref[...] = z[tm - SUBLANES:tm]
    else:
        e = hist_ref[...]
        t = row & (SUBLANES - 1)
        z1 = jnp.where(t == 0, pltpu.roll(e, tm - 7, 0), pltpu.roll(z, 1, 0))
        z2 = jnp.where(t < 2, pltpu.roll(e, tm - 6, 0), pltpu.roll(z, 2, 0))
        ztail_ref[...] = z
    cw = cw_ref[...]
    cv = z2 * cw[0:1] + z1 * cw[1:2] + z * cw[2:3]
    c_ref[...] = (b_gate * cv).astype(c_ref.dtype)
    for gi in range(n_g):
        off = 3 * D_C + 3 * gi * D_DG
        res = _dot(h, w_ref[:, off:off + 3 * D_DG])
        q = res[:, 0:D_DG] * QK_SCALE_D
        if tiles_per_seq:
            q_ref, k_ref, v_ref, kvt_ref = group_refs[4 * gi:4 * gi + 4]
            dil = D_WINDOWS[gi][1]
            _store_streams(q_ref, q, dil, stream_scr)
            _store_streams(k_ref, res[:, D_DG:2 * D_DG], dil, stream_scr)
            _store_streams(v_ref, res[:, 2 * D_DG:3 * D_DG], dil, stream_scr)
            kvt_ref[...] = res[tm - tails[gi]:tm, D_DG:3 * D_DG].T
        else:
            q_ref, kvt_ref = group_refs[2 * gi:2 * gi + 2]
            q_ref[...] = q
            kvt_ref[...] = res[:, D_DG:3 * D_DG].T


def _proj_cd_prompt(x, g, w, cw, n_batch, seq, tm):
    m = x.shape[0]
    tps = seq // tm
    row = lambda n, dt=None: pl.BlockSpec((tm, n), lambda i: (i, 0))
    hist = jnp.zeros((n_batch * SUBLANES, D_C), F32)
    out_shape = [jax.ShapeDtypeStruct((m, D_C), BF16), jax.ShapeDtypeStruct((m // tm * SUBLANES, D_C), F32)]
    out_specs = [row(D_C), pl.BlockSpec((SUBLANES, D_C), lambda i: (i, 0))]
    tails = []
    for win, dil in D_WINDOWS:
        keep = min(win, seq)
        tail = min(keep, tm)
        first_kept = tps - keep // tail
        tails.append(tail)
        out_shape += [jax.ShapeDtypeStruct((n_batch, seq // dil, dil * D_DG), BF16)] * 3
        out_shape += [jax.ShapeDtypeStruct((n_batch, 2 * D_DG, keep), F32)]
        out_specs += [pl.BlockSpec((None, tm // dil, dil * D_DG), lambda i: (i // tps, i % tps, 0))] * 3
        out_specs += [pl.BlockSpec((None, 2 * D_DG, tail), functools.partial(
            lambda i, fk: (i // tps, 0, jnp.maximum(i % tps - fk, 0)), fk=first_kept))]
    return pl.pallas_call(
        functools.partial(_proj_cd_kernel, tps, tuple(tails)),
        grid=(m // tm,),
        in_specs=[row(D_MODEL), _resident((1, D_MODEL)), _resident(w.shape),
                  _resident(cw.shape), pl.BlockSpec((SUBLANES, D_C), lambda i: (i // tps, 0))],
        out_specs=out_specs,
        out_shape=out_shape,
        scratch_shapes=[pltpu.VMEM((SUBLANES, D_C), F32), pltpu.VMEM((D_DG // LANES, tm, LANES), F32)],
        compiler_params=_params("arbitrary"),
    )(x, g, w, cw, hist)


def _proj_cd_sample(x, g, w, cw, hist_rows):
    m = x.shape[0]
    full = lambda r, n: pl.BlockSpec((r, n), lambda i: (0, 0))
    out_shape = [jax.ShapeDtypeStruct((m, D_C), F32), jax.ShapeDtypeStruct((m, D_C), F32)]
    out_specs = [full(m, D_C), full(m, D_C)]
    for _ in D_WINDOWS:
        out_shape += [jax.ShapeDtypeStruct((m, D_DG), F32), jax.ShapeDtypeStruct((2 * D_DG, m), F32)]
        out_specs += [full(m, D_DG), full(2 * D_DG, m)]
    return pl.pallas_call(
        functools.partial(_proj_cd_kernel, 0, ()),
        grid=(1,),
        in_specs=[full(m, D_MODEL), _resident((1, D_MODEL)), _resident(w.shape),
                  _resident(cw.shape), full(m, D_C)],
        out_specs=out_specs,
        out_shape=out_shape,
        scratch_shapes=[pltpu.VMEM((SUBLANES, D_C), F32), pltpu.VMEM((D_DG // LANES, SUBLANES, LANES), F32)],
        compiler_params=_params("arbitrary"),
    )(x, g, w, cw, hist_rows)


def _dilated_prompt_kernel(n_streams, q_ref, kc_ref, kp_ref, vc_ref, vp_ref, o_ref, lse_ref):
    bl = N_BACK
    n_qblk = q_ref.shape[0] // bl
    first_blk = pl.program_id(2) * n_qblk
    shape = (bl, 2 * bl)
    row = lax.broadcasted_iota(jnp.int32, shape, 0)
    col = lax.broadcasted_iota(jnp.int32, shape, 1)
    dist = row - col + bl
    band = (dist >= 0) & (dist <= N_BACK)
    hw = 2 * HD_D
    lane = lax.broadcasted_iota(jnp.int32, (bl, hw), 1)
    for r in range(n_streams):
        for qb in range(n_qblk):
            rows = slice(qb * bl, (qb + 1) * bl)
            prev_rows = slice((qb - 1) * bl, qb * bl)
            valid = (band & ((first_blk * bl + col - bl) >= 0)) if qb == 0 else band
            pairs = [slice(r * D_DG + p * hw, r * D_DG + (p + 1) * hw) for p in range(H_DG // 2)]
            scores = []
            for sl in pairs:
                qq = _split_heads_stack(q_ref[rows, sl], HD_D)
                k_prev = kp_ref[:, sl] if qb == 0 else kc_ref[prev_rows, sl]
                kcat = jnp.concatenate([k_prev, kc_ref[rows, sl]], axis=0)
                scores.append([_dot_nt(qq[e * bl:(e + 1) * bl], kcat) for e in range(2)])
            probs = []
            for pair_scores in scores:
                stats = []
                for s in pair_scores:
                    s = jnp.where(valid, s, -jnp.inf)
                    m = jnp.max(s, axis=-1, keepdims=True)
                    pr = jnp.exp2(s - m)
                    l = jnp.sum(pr, axis=-1, keepdims=True)
                    stats.append((pr.astype(BF16), l, (m + jnp.log2(l)) * LN_2))
                probs.append(stats)
            for sl, stats in zip(pairs, probs):
                v_prev = vp_ref[:, sl] if qb == 0 else vc_ref[prev_rows, sl]
                vcat = jnp.concatenate([v_prev, vc_ref[rows, sl]], axis=0)
                outs = [_dot(pr, vcat) / l for pr, l, _ in stats]
                o_ref[rows, sl] = jnp.where(lane < HD_D, outs[0], outs[1]).astype(o_ref.dtype)
                lse_ref[rows, sl] = jnp.where(lane < HD_D, stats[0][2], stats[1][2])


def _dilated_prompt(q, k, v, dil, blocks_per_step):
    n_batch, length, _ = q.shape
    bl = N_BACK
    n_qblk = min(blocks_per_step, length // bl)
    n_streams = min(blocks_per_step // n_qblk, dil)
    rows, width = n_qblk * bl, n_streams * D_DG
    cur = pl.BlockSpec((None, rows, width), lambda b, r, j: (b, j, r))
    prev = pl.BlockSpec((None, bl, width), lambda b, r, j: (b, jnp.maximum(j * n_qblk - 1, 0), r))
    return pl.pallas_call(
        functools.partial(_dilated_prompt_kernel, n_streams),
        grid=(n_batch, dil // n_streams, length // rows),
        in_specs=[cur, cur, prev, cur, prev],
        out_specs=[cur, cur],
        out_shape=[jax.ShapeDtypeStruct(q.shape, BF16), jax.ShapeDtypeStruct(q.shape, F32)],
        compiler_params=_params("arbitrary", "arbitrary", "arbitrary"),
    )(q, k, k, v, v)


def _dilated_sample_kernel(dil, q_ref, cache_ref, kvnew_ref, newc_ref, o_ref, lse_ref):
    n_seq = q_ref.shape[0]
    for s in range(n_seq):
        _dilated_sample_one(dil, pl.program_id(0) * n_seq + s, q_ref.at[s], cache_ref.at[s], kvnew_ref,
                            newc_ref.at[s], o_ref.at[s], lse_ref.at[s])


def _dilated_sample_one(dil, seq, q_ref, cache_ref, kvnew_ref, newc_ref, o_ref, lse_ref):
    buf_len = cache_ref.shape[1]
    t_new = q_ref.shape[0]
    hw = 2 * HD_D
    n_blk = buf_len // LANES

    seq_in_block = seq % (LANES // t_new)
    new_tile = pltpu.roll(kvnew_ref[...], (LANES - seq_in_block * t_new) % LANES, 1)

    lane = lax.broadcasted_iota(jnp.int32, (cache_ref.shape[0], LANES), 1)
    cur = pltpu.roll(cache_ref[:, 0:LANES], LANES - t_new, 1)
    for c in range(n_blk):
        following = cache_ref[:, (c + 1) * LANES:(c + 2) * LANES] if c + 1 < n_blk else new_tile
        nxt = pltpu.roll(following, LANES - t_new, 1)
        newc_ref[:, c * LANES:(c + 1) * LANES] = jnp.where(lane < LANES - t_new, cur, nxt)
        cur = nxt

    def iotas(n_keys):
        shape = (2 * t_new, n_keys)
        row = lax.broadcasted_iota(jnp.int32, shape, 0)
        return row & (t_new - 1), lax.broadcasted_iota(jnp.int32, shape, 1)

    qi, col = iotas(buf_len)
    d = buf_len + qi - col
    valid_buf = ((d & (dil - 1)) == 0) & (d <= N_BACK * dil)
    qi, col = iotas(LANES)
    d = qi - col
    valid_new = (col < t_new) & (d >= 0) & ((d & (dil - 1)) == 0)

    q = q_ref[...].astype(BF16)
    lane = lax.broadcasted_iota(jnp.int32, (t_new, hw), 1)
    for p in range(H_DG // 2):
        k_rows, v_rows = slice(p * hw, (p + 1) * hw), slice(D_DG + p * hw, D_DG + (p + 1) * hw)
        qq = _split_heads_stack(q[:, k_rows], HD_D)
        s_buf = jnp.where(valid_buf, _dot(qq, cache_ref[k_rows, :].astype(BF16)), NEG_BIG)
        s_new = jnp.where(valid_new, _dot(qq, new_tile[k_rows].astype(BF16)), NEG_BIG)
        m = jnp.maximum(jnp.max(s_buf, axis=-1, keepdims=True), jnp.max(s_new, axis=-1, keepdims=True))
        p_buf = jnp.where(valid_buf, jnp.exp2(s_buf - m), 0.0)
        p_new = jnp.where(valid_new, jnp.exp2(s_new - m), 0.0)
        l = jnp.sum(p_buf, axis=-1, keepdims=True) + jnp.sum(p_new, axis=-1, keepdims=True)
        acc = (_dot_nt(p_buf.astype(BF16), cache_ref[v_rows, :].astype(BF16))
               + _dot_nt(p_new.astype(BF16), new_tile[v_rows].astype(BF16)))
        o = acc / l
        lse = (m + jnp.log2(l)) * LN_2
        o_ref[:, k_rows] = jnp.where(lane < HD_D, o[0:t_new], o[t_new:2 * t_new])
        lse_ref[:, k_rows] = jnp.where(lane < HD_D, lse[0:t_new], lse[t_new:2 * t_new])


def _dilated_sample(q, cache_t, kvnew_t, dil, n_batch, t_new, seqs_per_step):
    buf_len = cache_t.shape[2]
    steps_per_lane_block = LANES // t_new // seqs_per_step
    small = lambda n: pl.BlockSpec((seqs_per_step, t_new, n), lambda b: (b, 0, 0))
    whole = pl.BlockSpec((seqs_per_step, 2 * D_DG, buf_len), lambda b: (b, 0, 0))
    newc, o, lse = pl.pallas_call(
        functools.partial(_dilated_sample_kernel, dil),
        grid=(n_batch // seqs_per_step,),
        in_specs=[small(D_DG), whole, pl.BlockSpec((2 * D_DG, LANES), lambda b: (0, b // steps_per_lane_block))],
        out_specs=[whole, small(D_DG), small(D_DG)],
        out_shape=[jax.ShapeDtypeStruct(cache_t.shape, F32),
                   jax.ShapeDtypeStruct((n_batch, t_new, D_DG), F32),
                   jax.ShapeDtypeStruct((n_batch, t_new, D_DG), F32)],
        compiler_params=_params("arbitrary"),
    )(q.reshape(n_batch, t_new, D_DG), cache_t, kvnew_t)
    return newc, o.reshape(n_batch * t_new, D_DG), lse.reshape(n_batch * t_new, D_DG)


def kernel(x_prompt, x_sample, state_a_pool, cache_b_kv, state_c_conv, cache_d0_kv, cache_d1_kv, cache_d2_kv, page_table, norm_mix_g, norm_mlp_g, norm_out_g, w_in_ab, w_out_ab, a_mix, a_scale, b_lam, b_subln_g, w_in_cd, w_out_cd, c_conv_w, w_up, w_down):
    bp, seq, _ = x_prompt.shape
    bs, t_new, _ = x_sample.shape
    n_pages = page_table.shape[1]
    past_len = n_pages * cache_b_kv.shape[2]
    mp, ms = bp * seq, bs * t_new
    tm_p = 512
    xp = x_prompt.reshape(mp, D_MODEL)
    xs = x_sample.reshape(ms, D_MODEL)
    g2 = lambda v: v.reshape(1, -1)
    d_caches = (cache_d0_kv, cache_d1_kv, cache_d2_kv)

    lam_init = 0.8 - 0.6 * math.exp(-0.3 * 0)
    w_in = w_in_ab[0].astype(BF16)
    w_out = w_out_ab[0].astype(BF16)
    wu, wd = w_up[0].astype(BF16), w_down[0].astype(BF16)
    amix = a_mix[0].astype(BF16)
    gm, gl = g2(norm_mix_g[0]), g2(norm_mlp_g[0])
    ascale, sg = g2(a_scale[0]), g2(b_subln_g[0])

    u_p, q_p, kv_p, k_p, vt_p = _proj_ab(xp, gm, w_in, tm_p, True, 512)
    hist_p = jnp.zeros((bp, POOL_HIST + 1, D_A), F32)
    a_p = _pool_mix(u_p, hist_p, amix, ascale, 0, bp, seq)
    ob_p = _diffattn_prompt(q_p, k_p, vt_p, b_lam[0], sg, lam_init, bp, seq, 256)

    u_s, q_s, kv_s, _, _ = _proj_ab(xs, gm, w_in, ms, False, 256)
    hist_s = jnp.pad(state_a_pool[0], ((0, 0), (1, 0), (0, 0)))
    a_s = _pool_mix(u_s, hist_s, amix, ascale, past_len, bs, t_new)
    pool_pages = cache_b_kv[0].reshape(cache_b_kv.shape[1], PAGE_SIZE * 2 * H_B, 2 * HD_B)
    xp = _post_mixer(xp, a_p, (ob_p,), (), (), w_out, gl, wu, wd, None, tm_p)
    ob_s = _diffattn_decode(page_table, pool_pages, q_s, kv_s, b_lam[0], sg, lam_init, bs, t_new,
                            min(32, n_pages))
    xs = _post_mixer(xs, a_s, (ob_s,), (), (), w_out, gl, wu, wd, None, ms)

    u_p3 = u_p.reshape(bp, seq, D_A)
    new_pool_p = u_p3[:, seq - POOL_HIST:][None]
    new_pool_s = jnp.concatenate([state_a_pool[0], u_s.reshape(bs, t_new, D_A)], axis=1)[:, -POOL_HIST:][None]
    new_bkv_p = kv_p.reshape(1, bp, seq, 2, H_B, 2 * HD_B)
    new_bkv_s = kv_s.reshape(1, bs, t_new, 2, H_B, 2 * HD_B)

    w_in = w_in_cd[0].astype(BF16)
    w_out = w_out_cd[0].astype(BF16)
    wu, wd = w_up[1].astype(BF16), w_down[1].astype(BF16)
    gm, gl, gf = g2(norm_mix_g[1]), g2(norm_mlp_g[1]), g2(norm_out_g)
    cw = c_conv_w[0]

    def from_position_minor(a):
        a = a.reshape(a.shape[0], 2, H_DG, HD_D, a.shape[2])
        return jnp.transpose(a, (0, 4, 1, 2, 3))[None]

    outs = _proj_cd_prompt(xp, gm, w_in, cw, bp, seq, tm_p)
    c_p, ztail_p = outs[0], outs[1]
    o_parts, lse_parts, new_d_p = [], [], []
    for gi, (win, dil) in enumerate(D_WINDOWS):
        q_g, k_g, v_g, kvt_g = outs[2 + 4 * gi:6 + 4 * gi]
        o_g, lse_g = _dilated_prompt(q_g, k_g, v_g, dil, 4)
        if dil == 1:
            o_g, lse_g = o_g.reshape(mp, D_DG), lse_g.reshape(mp, D_DG)
        o_parts.append(o_g)
        lse_parts.append(lse_g)
        new_d_p.append(from_position_minor(kvt_g))
    dils = tuple(dil for _, dil in D_WINDOWS)
    y_p = _post_mixer(xp, c_p, o_parts, lse_parts, dils, w_out, gl, wu, wd, gf, tm_p)
    tiles_per_seq = seq // tm_p
    new_conv_p = ztail_p.reshape(bp, tiles_per_seq, SUBLANES, D_C)[:, -1, SUBLANES - (CONV_W - 1):][None]

    hist_rows = jnp.pad(state_c_conv[0], ((0, 0), (t_new - (CONV_W - 1), 0), (0, 0))).reshape(ms, D_C)
    outs = _proj_cd_sample(xs, gm, w_in, cw, hist_rows)
    c_s, z_s = outs[0], outs[1]
    o_parts, lse_parts, new_d_s = [], [], []
    for gi, (win, dil) in enumerate(D_WINDOWS):
        q_g, kvt_g = outs[2 + 2 * gi:4 + 2 * gi]
        cache = d_caches[gi][0]
        buf_len = cache.shape[1]
        cache_t = jnp.transpose(cache, (0, 2, 3, 4, 1)).reshape(bs, 2 * D_DG, buf_len)
        seqs_per_step = max(1, min(SUBLANES, D_WINDOWS[-1][0] // (2 * buf_len)))
        newc, o_g, lse_g = _dilated_sample(q_g, cache_t, kvt_g, dil, bs, t_new, seqs_per_step)
        o_parts.append(o_g)
        lse_parts.append(lse_g)
        new_d_s.append(from_position_minor(newc))
    y_s = _post_mixer(xs, c_s, o_parts, lse_parts, (1,) * len(D_WINDOWS), w_out, gl, wu, wd, gf, ms)
    new_conv_s = z_s.reshape(bs, t_new, D_C)[:, t_new - (CONV_W - 1):][None]

    return (y_p.reshape(bp, seq, D_MODEL), y_s.reshape(bs, t_new, D_MODEL),
            new_pool_p, new_pool_s, new_bkv_p, new_bkv_s, new_conv_p, new_conv_s,
            new_d_p[0], new_d_s[0], new_d_p[1], new_d_s[1], new_d_p[2], new_d_s[2])
```

```python
import functools
import math

import jax
import jax.numpy as jnp
from jax import lax
from jax.experimental import pallas as pl
from jax.experimental.pallas import tpu as pltpu

F32 = jnp.float32
BF16 = jnp.bfloat16

D_MODEL = 1024
RMS_EPS = 1e-6
D_A = 512
POOL_WINDOWS = (2, 4, 8, 16)
POOL_GROUP = 128
POOL_HIST = 15
H_B = 4
HD_B = 64
D_B = 512
D_C = 512
CONV_W = 3
D_WINDOWS = ((128, 1), (512, 4), (2048, 16))
N_BACK = 128
H_DG = 8
HD_D = 64
D_DG = 512
D_FF = 4096
PAGE_SIZE = 128

LANES = 128
SUBLANES = 8
VMEM_LIMIT_BYTES = 56 * 1024 * 1024
NEG_BIG = -1e30

ROW_TILE = 512
FF_CHUNK = 1024
ATTN_QUERY_CHUNK = 256
ATTN_KEY_BLOCK = 512
PAGES_PER_STEP = 32
DILATED_BLOCKS_PER_STEP = 4

LOG2_E = math.log2(math.e)
LN_2 = math.log(2.0)
QK_SCALE_B = HD_B ** -0.5 * LOG2_E
QK_SCALE_D = HD_D ** -0.5 * LOG2_E


def _params(*sem):
    return pltpu.CompilerParams(dimension_semantics=sem, vmem_limit_bytes=VMEM_LIMIT_BYTES)


def _resident(shape):
    nd = len(shape)
    return pl.BlockSpec(shape, lambda *_: (0,) * nd, pipeline_mode=pl.Buffered(1))


def _rmsnorm(x, g):
    return x * lax.rsqrt(jnp.mean(x * x, axis=-1, keepdims=True) + RMS_EPS) * g


def _dot(a, b):
    return jnp.dot(a, b, preferred_element_type=F32)


def _dot_nt(a, b):
    return lax.dot_general(a, b, (((1,), (1,)), ((), ())), preferred_element_type=F32)


def _split_heads_stack(q, half):
    lane = lax.broadcasted_iota(jnp.int32, q.shape, 1)
    zero = jnp.zeros_like(q)
    return jnp.concatenate([jnp.where(lane < half, q, zero), jnp.where(lane >= half, q, zero)], axis=0)


def _lane_blocks(n):
    return [slice(c * LANES, (c + 1) * LANES) for c in range(n // LANES)]


def _store_streams(ref, val, dil, scr):
    if dil == 1:
        ref[...] = val.astype(ref.dtype)
        return
    width = val.shape[1]
    for c, sl in enumerate(_lane_blocks(width)):
        scr[c] = val[:, sl]
    for r in range(dil):
        for c, sl in enumerate(_lane_blocks(width)):
            ref[:, r * width + sl.start:r * width + sl.stop] = (
                scr[c, pl.ds(r, ref.shape[0], stride=dil), :].astype(ref.dtype))


def _rows_from_streams(ref, dil, scr):
    width = ref.shape[1] // dil
    if dil == 1:
        return [ref[:, sl].astype(F32) for sl in _lane_blocks(width)]
    for r in range(dil):
        for c, sl in enumerate(_lane_blocks(width)):
            scr[c, pl.ds(r, ref.shape[0], stride=dil), :] = ref[:, r * width + sl.start:r * width + sl.stop].astype(F32)
    return [scr[c] for c in range(width // LANES)]


def _diff_lambda(blam, lam_init):
    a = jnp.sum(blam[0:1] * blam[1:2], axis=-1, keepdims=True)
    b = jnp.sum(blam[2:3] * blam[3:4], axis=-1, keepdims=True)
    return jnp.exp(a) - jnp.exp(b) + lam_init


def _proj_ab_kernel(prompt, x_ref, g_ref, w_ref, u_ref, q_ref, kv_ref, k_ref, v_ref):
    h = _rmsnorm(x_ref[...], g_ref[...]).astype(BF16)
    kv = _dot(h, w_ref[:, D_A + D_B:D_A + 3 * D_B])
    u_ref[...] = _dot(h, w_ref[:, 0:D_A])
    q_ref[...] = (_dot(h, w_ref[:, D_A:D_A + D_B]) * QK_SCALE_B).astype(q_ref.dtype)
    k_ref[...] = kv[:, :D_B].astype(BF16)
    if not prompt:
        kv_ref[...] = kv
        v_ref[...] = kv[:, D_B:].astype(BF16)
        return
    n_blk = kv.shape[1] // LANES
    for c, sl in enumerate(_lane_blocks(kv.shape[1])):
        kv_ref[pl.ds(c, kv.shape[0], stride=n_blk), :] = kv[:, sl]
    vt = kv[:, D_B:].T.astype(BF16)
    tk = v_ref.shape[2]
    for j in range(v_ref.shape[0]):
        v_ref[j] = vt[:, j * tk:(j + 1) * tk]


def _proj_ab(x, g, w, tm, prompt, tk):
    m = x.shape[0]
    n_in = w.shape[1]
    row = lambda n: pl.BlockSpec((tm, n), lambda i: (i, 0))
    n_blk = 2 * D_B // LANES
    if prompt:
        kv_shape, kv_spec = (m * n_blk, LANES), pl.BlockSpec((tm * n_blk, LANES), lambda i: (i, 0))
        v_shape, v_spec = (m // tk, D_B, tk), pl.BlockSpec((tm // tk, D_B, tk), lambda i: (i, 0, 0))
    else:
        kv_shape, kv_spec = (m, 2 * D_B), row(2 * D_B)
        v_shape, v_spec = (m, D_B), row(D_B)
    return pl.pallas_call(
        functools.partial(_proj_ab_kernel, prompt),
        grid=(m // tm,),
        in_specs=[row(D_MODEL), _resident((1, D_MODEL)), _resident((D_MODEL, n_in))],
        out_specs=[row(D_A), row(D_B), kv_spec, row(D_B), v_spec],
        out_shape=[jax.ShapeDtypeStruct((m, D_A), F32), jax.ShapeDtypeStruct((m, D_B), BF16 if prompt else F32),
                   jax.ShapeDtypeStruct(kv_shape, F32), jax.ShapeDtypeStruct((m, D_B), BF16),
                   jax.ShapeDtypeStruct(v_shape, BF16)],
        compiler_params=_params("arbitrary"),
    )(x, g, w)


def _pool_kernel(pos0, tchunk, u_ref, hist_ref, amix_ref, ascale_ref, o_ref, ext_ref):
    t_len = u_ref.shape[0]
    hpad = hist_ref.shape[0]
    ext_ref[0:hpad, :] = hist_ref[...]
    ext_ref[hpad:hpad + t_len, :] = u_ref[...]
    for t0 in range(0, t_len, tchunk):
        pos = pos0 + t0 + lax.broadcasted_iota(jnp.int32, (tchunk, 1), 0)
        for gi, w in enumerate(POOL_WINDOWS):
            sl = slice(gi * POOL_GROUP, (gi + 1) * POOL_GROUP)
            u = u_ref[t0:t0 + tchunk, sl]
            acc = u
            for j in range(1, w):
                acc = acc + ext_ref[hpad + t0 - j:hpad + t0 - j + tchunk, sl]
            cnt = jnp.minimum(pos + 1, w).astype(F32)
            p = (acc / cnt - u).astype(BF16)
            y = _dot(p, amix_ref[gi]) * ascale_ref[:, sl]
            o_ref[t0:t0 + tchunk, sl] = y.astype(o_ref.dtype)


def _pool_mix(u, hist16, amix, ascale, pos0, n_seq, t_len):
    hpad = hist16.shape[1]
    tchunk = min(t_len, 256)
    out = pl.pallas_call(
        functools.partial(_pool_kernel, pos0, tchunk),
        grid=(n_seq,),
        in_specs=[pl.BlockSpec((None, t_len, D_A), lambda s: (s, 0, 0)),
                  pl.BlockSpec((None, hpad, D_A), lambda s: (s, 0, 0)),
                  _resident(amix.shape), _resident((1, D_A))],
        out_specs=pl.BlockSpec((None, t_len, D_A), lambda s: (s, 0, 0)),
        out_shape=jax.ShapeDtypeStruct((n_seq, t_len, D_A), BF16 if t_len % 16 == 0 else F32),
        scratch_shapes=[pltpu.VMEM((hpad + t_len, D_A), F32)],
        compiler_params=_params("arbitrary"),
    )(u.reshape(n_seq, t_len, D_A), hist16, amix, ascale)
    return out.reshape(n_seq * t_len, D_A)


def _diff_finalize(acc, l, lam, sg, lam_init, t):
    o = acc[0:t] / l[0:t] - lam * (acc[t:2 * t] / l[t:2 * t])
    o = o * lax.rsqrt(jnp.mean(o * o, axis=-1, keepdims=True) + RMS_EPS) * sg
    return o * (1.0 - lam_init)


def _diffattn_prompt_kernel(lam_init, tc, q_ref, k_ref, vt_ref, blam_ref, sgt_ref, o_ref, m_scr, l_scr, acc_scr):
    seq = q_ref.shape[0]
    tk = vt_ref.shape[2]
    hw = 2 * HD_B
    heads = [slice(h * hw, (h + 1) * hw) for h in range(H_B)]
    lam = _diff_lambda(blam_ref[...], lam_init)
    sgt = sgt_ref[...]
    shape = (tk, 2 * tc)
    key = lax.broadcasted_iota(jnp.int32, shape, 0)
    col = lax.broadcasted_iota(jnp.int32, shape, 1)
    qcol = jnp.where(col >= tc, col - tc, col)

    def scores(j, h, qq):
        off = pl.multiple_of(j * tk, tk)
        return _dot_nt(k_ref[pl.ds(off, tk), heads[h]], qq)

    def chunk(c, carry):
        q0 = pl.multiple_of(c * tc, tc)
        jd = q0 // tk
        qqs = [_split_heads_stack(q_ref[pl.ds(q0, tc), sl], HD_B) for sl in heads]

        causal = (jd * tk + key) <= (q0 + qcol)
        ss = [scores(jd, h, qqs[h]) for h in range(H_B)]
        ps = []
        for h in range(H_B):
            s = jnp.where(causal, ss[h], -jnp.inf)
            m = jnp.max(s, axis=0, keepdims=True)
            p = jnp.exp2(s - m)
            m_scr[h] = m
            l_scr[h] = jnp.sum(p, axis=0, keepdims=True)
            ps.append(p.astype(BF16))
        for h, sl in enumerate(heads):
            acc_scr[h] = _dot(vt_ref[jd, sl, :], ps[h])

        def body(j, inner):
            ss = [scores(j, h, qqs[h]) for h in range(H_B)]
            ps = []
            for h in range(H_B):
                m_prev = m_scr[h]
                m_new = jnp.maximum(m_prev, jnp.max(ss[h], axis=0, keepdims=True))
                alpha = jnp.exp2(m_prev - m_new)
                p = jnp.exp2(ss[h] - m_new)
                l_scr[h] = alpha * l_scr[h] + jnp.sum(p, axis=0, keepdims=True)
                m_scr[h] = m_new
                ps.append((alpha, p.astype(BF16)))
            for h, sl in enumerate(heads):
                alpha, p = ps[h]
                acc_scr[h] = alpha * acc_scr[h] + _dot(vt_ref[j, sl, :], p)
            return inner

        lax.fori_loop(0, jd, body, 0)
        for h, sl in enumerate(heads):
            o = acc_scr[h] / l_scr[h]
            o = o[:, 0:tc] - lam * o[:, tc:2 * tc]
            o = o * lax.rsqrt(jnp.mean(o * o, axis=0, keepdims=True) + RMS_EPS) * sgt * (1.0 - lam_init)
            o_ref[pl.ds(q0, tc), sl] = o.T.astype(o_ref.dtype)
        return carry

    lax.fori_loop(0, seq // tc, chunk, 0)


def _diffattn_prompt(q, k, vt, blam, sg, lam_init, n_batch, seq, tc):
    hw = 2 * HD_B
    tk = vt.shape[2]
    whole = pl.BlockSpec((seq, D_B), lambda b: (b, 0))
    return pl.pallas_call(
        functools.partial(_diffattn_prompt_kernel, lam_init, tc),
        grid=(n_batch,),
        in_specs=[whole, whole, pl.BlockSpec((seq // tk, D_B, tk), lambda b: (b, 0, 0)),
                  _resident(blam.shape), _resident((hw, 1))],
        out_specs=whole,
        out_shape=jax.ShapeDtypeStruct((n_batch * seq, D_B), BF16),
        scratch_shapes=[pltpu.VMEM((H_B, 1, 2 * tc), F32), pltpu.VMEM((H_B, 1, 2 * tc), F32),
                        pltpu.VMEM((H_B, hw, 2 * tc), F32)],
        compiler_params=_params("arbitrary"),
    )(q, k, vt, blam, sg.reshape(hw, 1))


def _online_softmax(rows, scores, valid, m_scr, l_scr):
    steps = []
    for r, s in zip(rows, scores):
        m_prev = m_scr[r]
        m_new = jnp.maximum(m_prev, jnp.max(s, axis=-1, keepdims=True))
        alpha = jnp.exp2(m_prev - m_new)
        p = jnp.exp2(s - m_new)
        if valid is not None:
            p = jnp.where(valid, p, 0.0)
        l_scr[r] = alpha * l_scr[r] + jnp.sum(p, axis=-1, keepdims=True)
        m_scr[r] = m_new
        steps.append((alpha, p.astype(BF16)))
    return steps


def _online_accumulate(rows, steps, values, acc_scr):
    for r, v, (alpha, p) in zip(rows, values, steps):
        acc_scr[r] = alpha * acc_scr[r] + _dot(p, v)


def _online_update(rows, scores, values, valid, m_scr, l_scr, acc_scr):
    _online_accumulate(rows, _online_softmax(rows, scores, valid, m_scr, l_scr), values, acc_scr)


def _init_online(m_scr, l_scr, acc_scr):
    m_scr[...] = jnp.full(m_scr.shape, NEG_BIG, F32)
    l_scr[...] = jnp.zeros(l_scr.shape, F32)
    acc_scr[...] = jnp.zeros(acc_scr.shape, F32)


def _decode_queries(q_ref):
    t_new = q_ref.shape[0]
    hw = 2 * HD_B
    q = q_ref[...].astype(BF16)
    qs = [_split_heads_stack(q[:, h * hw:(h + 1) * hw], HD_B) for h in range(H_B)]
    return qs, [slice(2 * t_new * h, 2 * t_new * (h + 1)) for h in range(H_B)]


def _decode_page_heads(page_refs, first_row):
    return jnp.concatenate([r[pl.ds(first_row, PAGE_SIZE, stride=2 * H_B), :] for r in page_refs],
                           axis=0).astype(BF16)


def _decode_finish(lam_init, qs, head_rows, kvnew_ref, blam_ref, sg_ref, o_ref, m_scr, l_scr, acc_scr, pad_scr):
    t_new = kvnew_ref.shape[0]
    hw = 2 * HD_B
    pad_scr[...] = jnp.zeros(pad_scr.shape, F32)
    pad_scr[0:t_new, :] = kvnew_ref[...]
    shape = (2 * t_new, pad_scr.shape[0])
    row = lax.broadcasted_iota(jnp.int32, shape, 0)
    col = lax.broadcasted_iota(jnp.int32, shape, 1)
    valid = col <= jnp.where(row >= t_new, row - t_new, row)
    scores = [jnp.where(valid, _dot_nt(qs[h], pad_scr[:, h * hw:(h + 1) * hw].astype(BF16)), NEG_BIG)
              for h in range(H_B)]
    values = [pad_scr[:, D_B + h * hw:D_B + (h + 1) * hw].astype(BF16) for h in range(H_B)]
    _online_update(head_rows, scores, values, valid, m_scr, l_scr, acc_scr)
    lam = _diff_lambda(blam_ref[...], lam_init)
    for h, rows in enumerate(head_rows):
        o = _diff_finalize(acc_scr[rows], l_scr[rows], lam, sg_ref[...], lam_init, t_new)
        o_ref[:, h * hw:(h + 1) * hw] = o.astype(o_ref.dtype)


def _diffattn_decode_kernel(pages_per_step, lam_init, pt_ref, q_ref, kvnew_ref, blam_ref, sg_ref, *rest):
    page_refs = rest[:pages_per_step]
    o_ref, m_scr, l_scr, acc_scr, pad_scr = rest[pages_per_step:]
    j = pl.program_id(1)
    qs, head_rows = _decode_queries(q_ref)

    @pl.when(j == 0)
    def _():
        _init_online(m_scr, l_scr, acc_scr)

    scores = [_dot_nt(qs[h], _decode_page_heads(page_refs, h)) for h in range(H_B)]
    values = [_decode_page_heads(page_refs, H_B + h) for h in range(H_B)]
    _online_update(head_rows, scores, values, None, m_scr, l_scr, acc_scr)

    @pl.when(j == pl.num_programs(1) - 1)
    def _():
        _decode_finish(lam_init, qs, head_rows, kvnew_ref, blam_ref, sg_ref, o_ref, m_scr, l_scr, acc_scr, pad_scr)


def _diffattn_decode(page_table, cache, q, kvnew, blam, sg, lam_init, n_batch, t_new, pages_per_step):
    n_pages = page_table.shape[1]
    hw = 2 * HD_B
    page_specs = [
        pl.BlockSpec((None, PAGE_SIZE * 2 * H_B, hw), functools.partial(
            lambda b, j, pt, t: (pt[b, j * pages_per_step + t], 0, 0), t=t))
        for t in range(pages_per_step)]
    grid_spec = pltpu.PrefetchScalarGridSpec(
        num_scalar_prefetch=1,
        grid=(n_batch, n_pages // pages_per_step),
        in_specs=[pl.BlockSpec((None, t_new, D_B), lambda b, j, pt: (b, 0, 0)),
                  pl.BlockSpec((None, t_new, 2 * D_B), lambda b, j, pt: (b, 0, 0)),
                  pl.BlockSpec(blam.shape, lambda b, j, pt: (0, 0)),
                  pl.BlockSpec((1, hw), lambda b, j, pt: (0, 0))] + page_specs,
        out_specs=pl.BlockSpec((None, t_new, D_B), lambda b, j, pt: (b, 0, 0)),
        scratch_shapes=[pltpu.VMEM((2 * t_new * H_B, 1), F32), pltpu.VMEM((2 * t_new * H_B, 1), F32),
                        pltpu.VMEM((2 * t_new * H_B, hw), F32), pltpu.VMEM((PAGE_SIZE, 2 * D_B), F32)])
    out = pl.pallas_call(
        functools.partial(_diffattn_decode_kernel, pages_per_step, lam_init),
        grid_spec=grid_spec,
        out_shape=jax.ShapeDtypeStruct((n_batch, t_new, D_B), F32),
        compiler_params=_params("arbitrary", "arbitrary"),
    )(page_table, q.reshape(n_batch, t_new, D_B), kvnew.reshape(n_batch, t_new, 2 * D_B), blam, sg,
      *([cache] * pages_per_step))
    return out.reshape(n_batch * t_new, D_B)


def _post_kernel(dils, ff_chunk, has_final, x_ref, first_ref, *rest):
    n_g = max(len(dils), 1)
    o_refs = rest[:n_g]
    lse_refs = rest[n_g:n_g + len(dils)]
    rest = rest[n_g + len(dils):]
    wout_ref, g_ref, wup_ref, wdn_ref = rest[:4]
    rest = rest[4:]
    gf_ref = rest[0] if has_final else None
    o_ref = rest[1] if has_final else rest[0]
    scratch = list(rest[2 if has_final else 1:])
    if not dils:
        second = o_refs[0][...].astype(BF16)
    else:
        scr_of = lambda d: scratch.pop(0) if d > 1 else None
        os_ = [_rows_from_streams(r, d, scr_of(d)) for r, d in zip(o_refs, dils)]
        ls_ = [_rows_from_streams(r, d, scr_of(d)) for r, d in zip(lse_refs, dils)]
        merged = []
        for c in range(len(os_[0])):
            lses = [l[c] for l in ls_]
            mx = functools.reduce(jnp.maximum, lses)
            es = [jnp.exp(l - mx) for l in lses]
            num = functools.reduce(lambda a, b: a + b, [e * o[c] for e, o in zip(es, os_)])
            merged.append((num / functools.reduce(lambda a, b: a + b, es)).astype(BF16))
        second = jnp.concatenate(merged, axis=-1)
    mix = jnp.concatenate([first_ref[...].astype(BF16), second], axis=-1)
    x1 = x_ref[...] + _dot(mix, wout_ref[...])
    h = _rmsnorm(x1, g_ref[...]).astype(BF16)
    acc = x1
    up = _dot(h, wup_ref[:, 0:ff_chunk])
    for c in range(0, D_FF, ff_chunk):
        nxt = _dot(h, wup_ref[:, c + ff_chunk:c + 2 * ff_chunk]) if c + ff_chunk < D_FF else None
        act = jnp.square(jnp.maximum(up, 0.0)).astype(BF16)
        acc = acc + _dot(act, wdn_ref[c:c + ff_chunk, :])
        up = nxt
    if has_final:
        acc = _rmsnorm(acc, gf_ref[...])
    o_ref[...] = acc


def _post_mixer(x, first, o_parts, lse_parts, dils, wout, g, wup, wdn, gf, tm):
    m = x.shape[0]
    row = lambda n: pl.BlockSpec((tm, n), lambda i: (i, 0))

    def part_spec(a, dil):
        if dil == 1:
            return row(a.shape[1])
        tiles_per_seq = a.shape[1] * dil // tm
        return pl.BlockSpec((None, tm // dil, a.shape[2]), lambda i: (i // tiles_per_seq, i % tiles_per_seq, 0))

    has_final = gf is not None
    part_dils = tuple(dils) if dils else (1,)
    args = [x, first, *o_parts, *lse_parts, wout, g, wup, wdn]
    specs = [row(D_MODEL), row(first.shape[1])]
    specs += [part_spec(a, d) for a, d in zip(o_parts, part_dils)] + [part_spec(a, d) for a, d in zip(lse_parts, dils)]
    specs += [_resident(wout.shape), _resident((1, D_MODEL)), _resident(wup.shape), _resident(wdn.shape)]
    if has_final:
        args.append(gf)
        specs.append(_resident((1, D_MODEL)))
    n_scratch = 2 * sum(d > 1 for d in dils)
    return pl.pallas_call(
        functools.partial(_post_kernel, tuple(dils), FF_CHUNK, has_final),
        grid=(m // tm,),
        in_specs=specs,
        out_specs=row(D_MODEL),
        out_shape=jax.ShapeDtypeStruct((m, D_MODEL), F32),
        scratch_shapes=[pltpu.VMEM((D_DG // LANES, tm, LANES), F32)] * n_scratch,
        compiler_params=_params("arbitrary"),
    )(*args)


def _proj_cd_kernel(tiles_per_seq, tails, x_ref, g_ref, w_ref, cw_ref, hist_ref, *rest):
    n_g = len(D_WINDOWS)
    c_ref, ztail_ref = rest[0], rest[1]
    group_refs = rest[2:2 + 4 * n_g] if tiles_per_seq else rest[2:2 + 2 * n_g]
    carry_ref, stream_scr = rest[-2], rest[-1]
    tm = x_ref.shape[0]
    i = pl.program_id(0)
    h = _rmsnorm(x_ref[...], g_ref[...]).astype(BF16)
    gates = _dot(h, w_ref[:, 0:3 * D_C])
    b_gate = gates[:, 0:D_C]
    z = gates[:, D_C:2 * D_C] * gates[:, 2 * D_C:3 * D_C]
    row = lax.broadcasted_iota(jnp.int32, z.shape, 0)
    if tiles_per_seq:
        first = (i % tiles_per_seq) == 0
        prev = jnp.where(first, hist_ref[...], carry_ref[...])
        z1 = jnp.where(row == 0, prev[7:8], pltpu.roll(z, 1, 0))
        z2 = jnp.where(row == 0, prev[6:7], jnp.where(row == 1, prev[7:8], pltpu.roll(z, 2, 0)))
        carry_ref[...] = z[tm - SUBLANES:tm]
        ztail_ref[...] = z[tm - SUBLANES:tm]
    else:
        e = hist_ref[...]
        t = row & (SUBLANES - 1)
        z1 = jnp.where(t == 0, pltpu.roll(e, tm - 7, 0), pltpu.roll(z, 1, 0))
        z2 = jnp.where(t < 2, pltpu.roll(e, tm - 6, 0), pltpu.roll(z, 2, 0))
        ztail_ref[...] = z
    cw = cw_ref[...]
    cv = z2 * cw[0:1] + z1 * cw[1:2] + z * cw[2:3]
    c_ref[...] = (b_gate * cv).astype(c_ref.dtype)
    for gi in reversed(range(n_g)):
        off = 3 * D_C + 3 * gi * D_DG
        res = _dot(h, w_ref[:, off:off + 3 * D_DG])
        q = res[:, 0:D_DG] * QK_SCALE_D
        if tiles_per_seq:
            q_ref, k_ref, v_ref, kvt_ref = group_refs[4 * gi:4 * gi + 4]
            dil = D_WINDOWS[gi][1]
            _store_streams(q_ref, q, dil, stream_scr)
            _store_streams(k_ref, res[:, D_DG:2 * D_DG], dil, stream_scr)
            _store_streams(v_ref, res[:, 2 * D_DG:3 * D_DG], dil, stream_scr)
            kvt_ref[...] = res[tm - tails[gi]:tm, D_DG:3 * D_DG].T
        else:
            q_ref, kvt_ref = group_refs[2 * gi:2 * gi + 2]
            q_ref[...] = q
            kvt_ref[...] = res[:, D_DG:3 * D_DG].T


def _proj_cd_prompt(x, g, w, cw, n_batch, seq, tm):
    m = x.shape[0]
    tps = seq // tm
    row = lambda n: pl.BlockSpec((tm, n), lambda i: (i, 0))
    hist = jnp.zeros((n_batch * SUBLANES, D_C), F32)
    out_shape = [jax.ShapeDtypeStruct((m, D_C), BF16), jax.ShapeDtypeStruct((m // tm * SUBLANES, D_C), F32)]
    out_specs = [row(D_C), pl.BlockSpec((SUBLANES, D_C), lambda i: (i, 0))]
    tails = []
    for win, dil in D_WINDOWS:
        keep = min(win, seq)
        tail = min(keep, tm)
        first_kept = tps - keep // tail
        tails.append(tail)
        out_shape += [jax.ShapeDtypeStruct((n_batch, seq // dil, dil * D_DG), BF16)] * 3
        out_shape += [jax.ShapeDtypeStruct((n_batch, 2 * D_DG, keep), F32)]
        out_specs += [pl.BlockSpec((None, tm // dil, dil * D_DG), lambda i: (i // tps, i % tps, 0))] * 3
        out_specs += [pl.BlockSpec((None, 2 * D_DG, tail), functools.partial(
            lambda i, fk: (i // tps, 0, jnp.maximum(i % tps - fk, 0)), fk=first_kept))]
    return pl.pallas_call(
        functools.partial(_proj_cd_kernel, tps, tuple(tails)),
        grid=(m // tm,),
        in_specs=[row(D_MODEL), _resident((1, D_MODEL)), _resident(w.shape),
                  _resident(cw.shape), pl.BlockSpec((SUBLANES, D_C), lambda i: (i // tps, 0))],
        out_specs=out_specs,
        out_shape=out_shape,
        scratch_shapes=[pltpu.VMEM((SUBLANES, D_C), F32), pltpu.VMEM((D_DG // LANES, tm, LANES), F32)],
        compiler_params=_params("arbitrary"),
    )(x, g, w, cw, hist)


def _proj_cd_sample(x, g, w, cw, hist_rows):
    m = x.shape[0]
    full = lambda r, n: pl.BlockSpec((r, n), lambda i: (0, 0))
    out_shape = [jax.ShapeDtypeStruct((m, D_C), F32), jax.ShapeDtypeStruct((m, D_C), F32)]
    out_specs = [full(m, D_C), full(m, D_C)]
    for _ in D_WINDOWS:
        out_shape += [jax.ShapeDtypeStruct((m, D_DG), F32), jax.ShapeDtypeStruct((2 * D_DG, m), F32)]
        out_specs += [full(m, D_DG), full(2 * D_DG, m)]
    return pl.pallas_call(
        functools.partial(_proj_cd_kernel, 0, ()),
        grid=(1,),
        in_specs=[full(m, D_MODEL), _resident((1, D_MODEL)), _resident(w.shape),
                  _resident(cw.shape), full(m, D_C)],
        out_specs=out_specs,
        out_shape=out_shape,
        scratch_shapes=[pltpu.VMEM((SUBLANES, D_C), F32), pltpu.VMEM((D_DG // LANES, SUBLANES, LANES), F32)],
        compiler_params=_params("arbitrary"),
    )(x, g, w, cw, hist_rows)


def _dilated_prompt_kernel(n_streams, q_ref, kc_ref, kp_ref, vc_ref, vp_ref, o_ref, lse_ref):
    bl = N_BACK
    n_qblk = q_ref.shape[0] // bl
    first_blk = pl.program_id(2) * n_qblk
    shape = (bl, 2 * bl)
    row = lax.broadcasted_iota(jnp.int32, shape, 0)
    col = lax.broadcasted_iota(jnp.int32, shape, 1)
    dist = row - col + bl
    band = (dist >= 0) & (dist <= N_BACK)
    hw = 2 * HD_D
    lane = lax.broadcasted_iota(jnp.int32, (bl, hw), 1)
    for r in range(n_streams):
        for qb in range(n_qblk):
            rows = slice(qb * bl, (qb + 1) * bl)
            prev_rows = slice((qb - 1) * bl, qb * bl)
            valid = (band & ((first_blk * bl + col - bl) >= 0)) if qb == 0 else band
            pairs = [slice(r * D_DG + p * hw, r * D_DG + (p + 1) * hw) for p in range(H_DG // 2)]
            scores = []
            for sl in pairs:
                qq = _split_heads_stack(q_ref[rows, sl], HD_D)
                k_prev = kp_ref[:, sl] if qb == 0 else kc_ref[prev_rows, sl]
                kcat = jnp.concatenate([k_prev, kc_ref[rows, sl]], axis=0)
                scores.append([_dot_nt(qq[e * bl:(e + 1) * bl], kcat) for e in range(2)])
            probs = []
            for pair_scores in scores:
                stats = []
                for s in pair_scores:
                    s = jnp.where(valid, s, -jnp.inf)
                    m = jnp.max(s, axis=-1, keepdims=True)
                    pr = jnp.exp2(s - m)
                    l = jnp.sum(pr, axis=-1, keepdims=True)
                    stats.append((pr.astype(BF16), l, (m + jnp.log2(l)) * LN_2))
                probs.append(stats)
            for sl, stats in zip(pairs, probs):
                v_prev = vp_ref[:, sl] if qb == 0 else vc_ref[prev_rows, sl]
                vcat = jnp.concatenate([v_prev, vc_ref[rows, sl]], axis=0)
                outs = [_dot(pr, vcat) / l for pr, l, _ in stats]
                o_ref[rows, sl] = jnp.where(lane < HD_D, outs[0], outs[1]).astype(o_ref.dtype)
                lse_ref[rows, sl] = jnp.where(lane < HD_D, stats[0][2], stats[1][2])


def _dilated_prompt(q, k, v, dil, blocks_per_step):
    n_batch, length, _ = q.shape
    bl = N_BACK
    n_qblk = min(blocks_per_step, length // bl)
    n_streams = min(blocks_per_step // n_qblk, dil)
    rows, width = n_qblk * bl, n_streams * D_DG
    cur = pl.BlockSpec((None, rows, width), lambda b, r, j: (b, j, r))
    prev = pl.BlockSpec((None, bl, width), lambda b, r, j: (b, jnp.maximum(j * n_qblk - 1, 0), r))
    return pl.pallas_call(
        functools.partial(_dilated_prompt_kernel, n_streams),
        grid=(n_batch, dil // n_streams, length // rows),
        in_specs=[cur, cur, prev, cur, prev],
        out_specs=[cur, cur],
        out_shape=[jax.ShapeDtypeStruct(q.shape, BF16), jax.ShapeDtypeStruct(q.shape, F32)],
        compiler_params=_params("arbitrary", "arbitrary", "arbitrary"),
    )(q, k, k, v, v)


def _dilated_sample_kernel(dil, q_ref, cache_ref, kvnew_ref, newc_ref, o_ref, lse_ref):
    n_seq = q_ref.shape[0]
    for s in range(n_seq):
        _dilated_sample_one(dil, pl.program_id(0) * n_seq + s, q_ref.at[s], cache_ref.at[s], kvnew_ref,
                            newc_ref.at[s], o_ref.at[s], lse_ref.at[s])


def _dilated_sample_one(dil, seq, q_ref, cache_ref, kvnew_ref, newc_ref, o_ref, lse_ref):
    buf_len = cache_ref.shape[1]
    t_new = q_ref.shape[0]
    hw = 2 * HD_D
    n_blk = buf_len // LANES

    seq_in_block = seq % (LANES // t_new)
    new_tile = pltpu.roll(kvnew_ref[...], (LANES - seq_in_block * t_new) % LANES, 1)

    lane = lax.broadcasted_iota(jnp.int32, (cache_ref.shape[0], LANES), 1)
    cur = pltpu.roll(cache_ref[:, 0:LANES], LANES - t_new, 1)
    for c in range(n_blk):
        following = cache_ref[:, (c + 1) * LANES:(c + 2) * LANES] if c + 1 < n_blk else new_tile
        nxt = pltpu.roll(following, LANES - t_new, 1)
        newc_ref[:, c * LANES:(c + 1) * LANES] = jnp.where(lane < LANES - t_new, cur, nxt)
        cur = nxt

    def iotas(n_keys):
        shape = (2 * t_new, n_keys)
        row = lax.broadcasted_iota(jnp.int32, shape, 0)
        return row & (t_new - 1), lax.broadcasted_iota(jnp.int32, shape, 1)

    qi, col = iotas(buf_len)
    d = buf_len + qi - col
    valid_buf = ((d & (dil - 1)) == 0) & (d <= N_BACK * dil)
    qi, col = iotas(LANES)
    d = qi - col
    valid_new = (col < t_new) & (d >= 0) & ((d & (dil - 1)) == 0)

    q = q_ref[...].astype(BF16)
    lane = lax.broadcasted_iota(jnp.int32, (t_new, hw), 1)
    for p in range(H_DG // 2):
        k_rows, v_rows = slice(p * hw, (p + 1) * hw), slice(D_DG + p * hw, D_DG + (p + 1) * hw)
        qq = _split_heads_stack(q[:, k_rows], HD_D)
        s_buf = jnp.where(valid_buf, _dot(qq, cache_ref[k_rows, :].astype(BF16)), NEG_BIG)
        s_new = jnp.where(valid_new, _dot(qq, new_tile[k_rows].astype(BF16)), NEG_BIG)
        m = jnp.maximum(jnp.max(s_buf, axis=-1, keepdims=True), jnp.max(s_new, axis=-1, keepdims=True))
        p_buf = jnp.where(valid_buf, jnp.exp2(s_buf - m), 0.0)
        p_new = jnp.where(valid_new, jnp.exp2(s_new - m), 0.0)
        l = jnp.sum(p_buf, axis=-1, keepdims=True) + jnp.sum(p_new, axis=-1, keepdims=True)
        acc = (_dot_nt(p_buf.astype(BF16), cache_ref[v_rows, :].astype(BF16))
               + _dot_nt(p_new.astype(BF16), new_tile[v_rows].astype(BF16)))
        o = acc / l
        lse = (m + jnp.log2(l)) * LN_2
        o_ref[:, k_rows] = jnp.where(lane < HD_D, o[0:t_new], o[t_new:2 * t_new])
        lse_ref[:, k_rows] = jnp.where(lane < HD_D, lse[0:t_new], lse[t_new:2 * t_new])


def _dilated_sample(q, cache_t, kvnew_t, dil, n_batch, t_new, seqs_per_step):
    buf_len = cache_t.shape[2]
    steps_per_lane_block = LANES // t_new // seqs_per_step
    small = lambda n: pl.BlockSpec((seqs_per_step, t_new, n), lambda b: (b, 0, 0))
    whole = pl.BlockSpec((seqs_per_step, 2 * D_DG, buf_len), lambda b: (b, 0, 0))
    newc, o, lse = pl.pallas_call(
        functools.partial(_dilated_sample_kernel, dil),
        grid=(n_batch // seqs_per_step,),
        in_specs=[small(D_DG), whole, pl.BlockSpec((2 * D_DG, LANES), lambda b: (0, b // steps_per_lane_block))],
        out_specs=[whole, small(D_DG), small(D_DG)],
        out_shape=[jax.ShapeDtypeStruct(cache_t.shape, F32),
                   jax.ShapeDtypeStruct((n_batch, t_new, D_DG), F32),
                   jax.ShapeDtypeStruct((n_batch, t_new, D_DG), F32)],
        compiler_params=_params("arbitrary"),
    )(q.reshape(n_batch, t_new, D_DG), cache_t, kvnew_t)
    return newc, o.reshape(n_batch * t_new, D_DG), lse.reshape(n_batch * t_new, D_DG)


def kernel(x_prompt, x_sample, state_a_pool, cache_b_kv, state_c_conv, cache_d0_kv, cache_d1_kv, cache_d2_kv, page_table, norm_mix_g, norm_mlp_g, norm_out_g, w_in_ab, w_out_ab, a_mix, a_scale, b_lam, b_subln_g, w_in_cd, w_out_cd, c_conv_w, w_up, w_down):
    bp, seq, _ = x_prompt.shape
    bs, t_new, _ = x_sample.shape
    n_pages = page_table.shape[1]
    past_len = n_pages * cache_b_kv.shape[2]
    mp, ms = bp * seq, bs * t_new
    tm_p = ROW_TILE
    xp = x_prompt.reshape(mp, D_MODEL)
    xs = x_sample.reshape(ms, D_MODEL)
    g2 = lambda v: v.reshape(1, -1)
    d_caches = (cache_d0_kv, cache_d1_kv, cache_d2_kv)

    lam_init = 0.8 - 0.6 * math.exp(-0.3 * 0)
    w_in = w_in_ab[0].astype(BF16)
    w_out = w_out_ab[0].astype(BF16)
    wu, wd = w_up[0].astype(BF16), w_down[0].astype(BF16)
    amix = a_mix[0].astype(BF16)
    gm, gl = g2(norm_mix_g[0]), g2(norm_mlp_g[0])
    ascale, sg = g2(a_scale[0]), g2(b_subln_g[0])

    u_p, q_p, kv_p, k_p, vt_p = _proj_ab(xp, gm, w_in, tm_p, True, ATTN_KEY_BLOCK)
    hist_p = jnp.zeros((bp, POOL_HIST + 1, D_A), F32)
    a_p = _pool_mix(u_p, hist_p, amix, ascale, 0, bp, seq)
    ob_p = _diffattn_prompt(q_p, k_p, vt_p, b_lam[0], sg, lam_init, bp, seq, ATTN_QUERY_CHUNK)

    u_s, q_s, kv_s, _, _ = _proj_ab(xs, gm, w_in, ms, False, ATTN_KEY_BLOCK)
    hist_s = jnp.pad(state_a_pool[0], ((0, 0), (1, 0), (0, 0)))
    a_s = _pool_mix(u_s, hist_s, amix, ascale, past_len, bs, t_new)
    pool_pages = cache_b_kv[0].reshape(cache_b_kv.shape[1], PAGE_SIZE * 2 * H_B, 2 * HD_B)
    xp = _post_mixer(xp, a_p, (ob_p,), (), (), w_out, gl, wu, wd, None, tm_p)
    ob_s = _diffattn_decode(page_table, pool_pages, q_s, kv_s, b_lam[0], sg, lam_init, bs, t_new,
                            min(PAGES_PER_STEP, n_pages))
    xs = _post_mixer(xs, a_s, (ob_s,), (), (), w_out, gl, wu, wd, None, ms)

    u_p3 = u_p.reshape(bp, seq, D_A)
    new_pool_p = u_p3[:, seq - POOL_HIST:][None]
    new_pool_s = jnp.concatenate([state_a_pool[0], u_s.reshape(bs, t_new, D_A)], axis=1)[:, -POOL_HIST:][None]
    new_bkv_p = kv_p.reshape(1, bp, seq, 2, H_B, 2 * HD_B)
    new_bkv_s = kv_s.reshape(1, bs, t_new, 2, H_B, 2 * HD_B)

    w_in = w_in_cd[0].astype(BF16)
    w_out = w_out_cd[0].astype(BF16)
    wu, wd = w_up[1].astype(BF16), w_down[1].astype(BF16)
    gm, gl, gf = g2(norm_mix_g[1]), g2(norm_mlp_g[1]), g2(norm_out_g)
    cw = c_conv_w[0]

    def from_position_minor(a):
        a = a.reshape(a.shape[0], 2, H_DG, HD_D, a.shape[2])
        return jnp.transpose(a, (0, 4, 1, 2, 3))[None]

    outs = _proj_cd_prompt(xp, gm, w_in, cw, bp, seq, tm_p)
    c_p, ztail_p = outs[0], outs[1]
    o_parts, lse_parts, new_d_p = [], [], []
    for gi, (win, dil) in enumerate(D_WINDOWS):
        q_g, k_g, v_g, kvt_g = outs[2 + 4 * gi:6 + 4 * gi]
        o_g, lse_g = _dilated_prompt(q_g, k_g, v_g, dil, DILATED_BLOCKS_PER_STEP)
        if dil == 1:
            o_g, lse_g = o_g.reshape(mp, D_DG), lse_g.reshape(mp, D_DG)
        o_parts.append(o_g)
        lse_parts.append(lse_g)
        new_d_p.append(from_position_minor(kvt_g))
    dils = tuple(dil for _, dil in D_WINDOWS)
    y_p = _post_mixer(xp, c_p, o_parts, lse_parts, dils, w_out, gl, wu, wd, gf, tm_p)
    tiles_per_seq = seq // tm_p
    new_conv_p = ztail_p.reshape(bp, tiles_per_seq, SUBLANES, D_C)[:, -1, SUBLANES - (CONV_W - 1):][None]

    hist_rows = jnp.pad(state_c_conv[0], ((0, 0), (t_new - (CONV_W - 1), 0), (0, 0))).reshape(ms, D_C)
    outs = _proj_cd_sample(xs, gm, w_in, cw, hist_rows)
    c_s, z_s = outs[0], outs[1]
    o_parts, lse_parts, new_d_s = [], [], []
    for gi, (win, dil) in enumerate(D_WINDOWS):
        q_g, kvt_g = outs[2 + 2 * gi:4 + 2 * gi]
        cache = d_caches[gi][0]
        buf_len = cache.shape[1]
        cache_t = jnp.transpose(cache, (0, 2, 3, 4, 1)).reshape(bs, 2 * D_DG, buf_len)
        seqs_per_step = max(1, min(SUBLANES, D_WINDOWS[-1][0] // (2 * buf_len)))
        newc, o_g, lse_g = _dilated_sample(q_g, cache_t, kvt_g, dil, bs, t_new, seqs_per_step)
        o_parts.append(o_g)
        lse_parts.append(lse_g)
        new_d_s.append(from_position_minor(newc))
    y_s = _post_mixer(xs, c_s, o_parts, lse_parts, (1,) * len(D_WINDOWS), w_out, gl, wu, wd, gf, ms)
    new_conv_s = z_s.reshape(bs, t_new, D_C)[:, t_new - (CONV_W - 1):][None]

    return (y_p.reshape(bp, seq, D_MODEL), y_s.reshape(bs, t_new, D_MODEL),
            new_pool_p, new_pool_s, new_bkv_p, new_bkv_s, new_conv_p, new_conv_s,
            new_d_p[0], new_d_s[0], new_d_p[1], new_d_s[1], new_d_p[2], new_d_s[2])
```

```python
import functools
import math

import jax
import jax.numpy as jnp
from jax import lax
from jax.experimental import pallas as pl
from jax.experimental.pallas import tpu as pltpu

F32 = jnp.float32
BF16 = jnp.bfloat16

D_MODEL = 1024
RMS_EPS = 1e-6
D_A = 512
POOL_WINDOWS = (2, 4, 8, 16)
POOL_GROUP = 128
POOL_HIST = 15
H_B = 4
HD_B = 64
D_B = 512
D_C = 512
CONV_W = 3
D_WINDOWS = ((128, 1), (512, 4), (2048, 16))
N_BACK = 128
H_DG = 8
HD_D = 64
D_DG = 512
D_FF = 4096
PAGE_SIZE = 128

LANES = 128
SUBLANES = 8
VMEM_LIMIT_BYTES = 56 * 1024 * 1024
NEG_BIG = -1e30

ROW_TILE = 512
FF_CHUNK = 1024
ATTN_QUERY_CHUNK = 512
ATTN_KEY_BLOCK = 512
PAGES_PER_STEP = 32
DILATED_BLOCKS_PER_STEP = 8

LOG2_E = math.log2(math.e)
LN_2 = math.log(2.0)
QK_SCALE_B = HD_B ** -0.5 * LOG2_E
QK_SCALE_D = HD_D ** -0.5 * LOG2_E


def _params(*sem):
    return pltpu.CompilerParams(dimension_semantics=sem, vmem_limit_bytes=VMEM_LIMIT_BYTES)


def _resident(shape):
    nd = len(shape)
    return pl.BlockSpec(shape, lambda *_: (0,) * nd, pipeline_mode=pl.Buffered(1))


def _rmsnorm(x, g):
    return x * lax.rsqrt(jnp.mean(x * x, axis=-1, keepdims=True) + RMS_EPS) * g


def _dot(a, b):
    return jnp.dot(a, b, preferred_element_type=F32)


def _dot_nt(a, b):
    return lax.dot_general(a, b, (((1,), (1,)), ((), ())), preferred_element_type=F32)


def _split_heads_stack(q, half):
    lane = lax.broadcasted_iota(jnp.int32, q.shape, 1)
    zero = jnp.zeros_like(q)
    return jnp.concatenate([jnp.where(lane < half, q, zero), jnp.where(lane >= half, q, zero)], axis=0)


def _lane_blocks(n):
    return [slice(c * LANES, (c + 1) * LANES) for c in range(n // LANES)]


def _store_streams(ref, val, dil, scr):
    if dil == 1:
        ref[...] = val.astype(ref.dtype)
        return
    width = val.shape[1]
    for c, sl in enumerate(_lane_blocks(width)):
        scr[c] = val[:, sl]
    for r in range(dil):
        for c, sl in enumerate(_lane_blocks(width)):
            ref[:, r * width + sl.start:r * width + sl.stop] = (
                scr[c, pl.ds(r, ref.shape[0], stride=dil), :].astype(ref.dtype))


def _rows_from_streams(ref, dil, scr):
    width = ref.shape[1] // dil
    if dil == 1:
        return [ref[:, sl].astype(F32) for sl in _lane_blocks(width)]
    for r in range(dil):
        for c, sl in enumerate(_lane_blocks(width)):
            scr[c, pl.ds(r, ref.shape[0], stride=dil), :] = ref[:, r * width + sl.start:r * width + sl.stop].astype(F32)
    return [scr[c] for c in range(width // LANES)]


def _diff_lambda(blam, lam_init):
    a = jnp.sum(blam[0:1] * blam[1:2], axis=-1, keepdims=True)
    b = jnp.sum(blam[2:3] * blam[3:4], axis=-1, keepdims=True)
    return jnp.exp(a) - jnp.exp(b) + lam_init


def _proj_ab_kernel(prompt, x_ref, g_ref, w_ref, u_ref, q_ref, kv_ref, k_ref, v_ref):
    h = _rmsnorm(x_ref[...], g_ref[...]).astype(BF16)
    kv = _dot(h, w_ref[:, D_A + D_B:D_A + 3 * D_B])
    u_ref[...] = _dot(h, w_ref[:, 0:D_A])
    q_ref[...] = (_dot(h, w_ref[:, D_A:D_A + D_B]) * QK_SCALE_B).astype(q_ref.dtype)
    k_ref[...] = kv[:, :D_B].astype(BF16)
    if not prompt:
        kv_ref[...] = kv
        v_ref[...] = kv[:, D_B:].astype(BF16)
        return
    n_blk = kv.shape[1] // LANES
    for c, sl in enumerate(_lane_blocks(kv.shape[1])):
        kv_ref[pl.ds(c, kv.shape[0], stride=n_blk), :] = kv[:, sl]
    vt = kv[:, D_B:].T.astype(BF16)
    tk = v_ref.shape[2]
    for j in range(v_ref.shape[0]):
        v_ref[j] = vt[:, j * tk:(j + 1) * tk]


def _proj_ab(x, g, w, tm, prompt, tk):
    m = x.shape[0]
    n_in = w.shape[1]
    row = lambda n: pl.BlockSpec((tm, n), lambda i: (i, 0))
    n_blk = 2 * D_B // LANES
    if prompt:
        kv_shape, kv_spec = (m * n_blk, LANES), pl.BlockSpec((tm * n_blk, LANES), lambda i: (i, 0))
        v_shape, v_spec = (m // tk, D_B, tk), pl.BlockSpec((tm // tk, D_B, tk), lambda i: (i, 0, 0))
    else:
        kv_shape, kv_spec = (m, 2 * D_B), row(2 * D_B)
        v_shape, v_spec = (m, D_B), row(D_B)
    return pl.pallas_call(
        functools.partial(_proj_ab_kernel, prompt),
        grid=(m // tm,),
        in_specs=[row(D_MODEL), _resident((1, D_MODEL)), _resident((D_MODEL, n_in))],
        out_specs=[row(D_A), row(D_B), kv_spec, row(D_B), v_spec],
        out_shape=[jax.ShapeDtypeStruct((m, D_A), F32), jax.ShapeDtypeStruct((m, D_B), BF16 if prompt else F32),
                   jax.ShapeDtypeStruct(kv_shape, F32), jax.ShapeDtypeStruct((m, D_B), BF16),
                   jax.ShapeDtypeStruct(v_shape, BF16)],
        compiler_params=_params("arbitrary"),
    )(x, g, w)


def _pool_kernel(pos0, tchunk, u_ref, hist_ref, amix_ref, ascale_ref, o_ref, ext_ref):
    t_len = u_ref.shape[0]
    hpad = hist_ref.shape[0]
    ext_ref[0:hpad, :] = hist_ref[...]
    ext_ref[hpad:hpad + t_len, :] = u_ref[...]
    for t0 in range(0, t_len, tchunk):
        pos = pos0 + t0 + lax.broadcasted_iota(jnp.int32, (tchunk, 1), 0)
        for gi, w in enumerate(POOL_WINDOWS):
            sl = slice(gi * POOL_GROUP, (gi + 1) * POOL_GROUP)
            u = u_ref[t0:t0 + tchunk, sl]
            acc = u
            for j in range(1, w):
                acc = acc + ext_ref[hpad + t0 - j:hpad + t0 - j + tchunk, sl]
            cnt = jnp.minimum(pos + 1, w).astype(F32)
            p = (acc / cnt - u).astype(BF16)
            y = _dot(p, amix_ref[gi]) * ascale_ref[:, sl]
            o_ref[t0:t0 + tchunk, sl] = y.astype(o_ref.dtype)


def _pool_mix(u, hist16, amix, ascale, pos0, n_seq, t_len):
    hpad = hist16.shape[1]
    tchunk = min(t_len, 256)
    out = pl.pallas_call(
        functools.partial(_pool_kernel, pos0, tchunk),
        grid=(n_seq,),
        in_specs=[pl.BlockSpec((None, t_len, D_A), lambda s: (s, 0, 0)),
                  pl.BlockSpec((None, hpad, D_A), lambda s: (s, 0, 0)),
                  _resident(amix.shape), _resident((1, D_A))],
        out_specs=pl.BlockSpec((None, t_len, D_A), lambda s: (s, 0, 0)),
        out_shape=jax.ShapeDtypeStruct((n_seq, t_len, D_A), BF16 if t_len % 16 == 0 else F32),
        scratch_shapes=[pltpu.VMEM((hpad + t_len, D_A), F32)],
        compiler_params=_params("arbitrary"),
    )(u.reshape(n_seq, t_len, D_A), hist16, amix, ascale)
    return out.reshape(n_seq * t_len, D_A)


def _diff_finalize(acc, l, lam, sg, lam_init, t):
    o = acc[0:t] / l[0:t] - lam * (acc[t:2 * t] / l[t:2 * t])
    o = o * lax.rsqrt(jnp.mean(o * o, axis=-1, keepdims=True) + RMS_EPS) * sg
    return o * (1.0 - lam_init)


def _diffattn_prompt_kernel(lam_init, tc, q_ref, k_ref, vt_ref, blam_ref, sgt_ref, o_ref, m_scr, l_scr, acc_scr):
    seq = q_ref.shape[0]
    tk = vt_ref.shape[2]
    hw = 2 * HD_B
    heads = [slice(h * hw, (h + 1) * hw) for h in range(H_B)]
    lam = _diff_lambda(blam_ref[...], lam_init)
    sgt = sgt_ref[...]
    shape = (tk, 2 * tc)
    key = lax.broadcasted_iota(jnp.int32, shape, 0)
    col = lax.broadcasted_iota(jnp.int32, shape, 1)
    qcol = jnp.where(col >= tc, col - tc, col)

    def scores(j, h, qq):
        off = pl.multiple_of(j * tk, tk)
        return _dot_nt(k_ref[pl.ds(off, tk), heads[h]], qq)

    def chunk(c, carry):
        q0 = pl.multiple_of(c * tc, tc)
        jd = q0 // tk
        qqs = [_split_heads_stack(q_ref[pl.ds(q0, tc), sl], HD_B) for sl in heads]

        causal = (jd * tk + key) <= (q0 + qcol)
        ss = [scores(jd, h, qqs[h]) for h in range(H_B)]
        ps = []
        for h in range(H_B):
            s = jnp.where(causal, ss[h], -jnp.inf)
            m = jnp.max(s, axis=0, keepdims=True)
            p = jnp.exp2(s - m)
            m_scr[h] = m
            l_scr[h] = jnp.sum(p, axis=0, keepdims=True)
            ps.append(p.astype(BF16))
        for h, sl in enumerate(heads):
            acc_scr[h] = _dot(vt_ref[jd, sl, :], ps[h])

        def body(j, inner):
            ss = [scores(j, h, qqs[h]) for h in range(H_B)]
            ps = []
            for h in range(H_B):
                m_prev = m_scr[h]
                m_new = jnp.maximum(m_prev, jnp.max(ss[h], axis=0, keepdims=True))
                alpha = jnp.exp2(m_prev - m_new)
                p = jnp.exp2(ss[h] - m_new)
                l_scr[h] = alpha * l_scr[h] + jnp.sum(p, axis=0, keepdims=True)
                m_scr[h] = m_new
                ps.append((alpha, p.astype(BF16)))
            for h, sl in enumerate(heads):
                alpha, p = ps[h]
                acc_scr[h] = alpha * acc_scr[h] + _dot(vt_ref[j, sl, :], p)
            return inner

        lax.fori_loop(0, jd, body, 0)
        for h, sl in enumerate(heads):
            o = acc_scr[h] / l_scr[h]
            o = o[:, 0:tc] - lam * o[:, tc:2 * tc]
            o = o * lax.rsqrt(jnp.mean(o * o, axis=0, keepdims=True) + RMS_EPS) * sgt * (1.0 - lam_init)
            o_ref[pl.ds(q0, tc), sl] = o.T.astype(o_ref.dtype)
        return carry

    lax.fori_loop(0, seq // tc, chunk, 0)


def _diffattn_prompt(q, k, vt, blam, sg, lam_init, n_batch, seq, tc):
    hw = 2 * HD_B
    tk = vt.shape[2]
    whole = pl.BlockSpec((seq, D_B), lambda b: (b, 0))
    return pl.pallas_call(
        functools.partial(_diffattn_prompt_kernel, lam_init, tc),
        grid=(n_batch,),
        in_specs=[whole, whole, pl.BlockSpec((seq // tk, D_B, tk), lambda b: (b, 0, 0)),
                  _resident(blam.shape), _resident((hw, 1))],
        out_specs=whole,
        out_shape=jax.ShapeDtypeStruct((n_batch * seq, D_B), BF16),
        scratch_shapes=[pltpu.VMEM((H_B, 1, 2 * tc), F32), pltpu.VMEM((H_B, 1, 2 * tc), F32),
                        pltpu.VMEM((H_B, hw, 2 * tc), F32)],
        compiler_params=_params("arbitrary"),
    )(q, k, vt, blam, sg.reshape(hw, 1))


def _online_softmax(rows, scores, valid, m_scr, l_scr):
    steps = []
    for r, s in zip(rows, scores):
        m_prev = m_scr[r]
        m_new = jnp.maximum(m_prev, jnp.max(s, axis=-1, keepdims=True))
        alpha = jnp.exp2(m_prev - m_new)
        p = jnp.exp2(s - m_new)
        if valid is not None:
            p = jnp.where(valid, p, 0.0)
        l_scr[r] = alpha * l_scr[r] + jnp.sum(p, axis=-1, keepdims=True)
        m_scr[r] = m_new
        steps.append((alpha, p.astype(BF16)))
    return steps


def _online_accumulate(rows, steps, values, acc_scr):
    for r, v, (alpha, p) in zip(rows, values, steps):
        acc_scr[r] = alpha * acc_scr[r] + _dot(p, v)


def _online_update(rows, scores, values, valid, m_scr, l_scr, acc_scr):
    _online_accumulate(rows, _online_softmax(rows, scores, valid, m_scr, l_scr), values, acc_scr)


def _init_online(m_scr, l_scr, acc_scr):
    m_scr[...] = jnp.full(m_scr.shape, NEG_BIG, F32)
    l_scr[...] = jnp.zeros(l_scr.shape, F32)
    acc_scr[...] = jnp.zeros(acc_scr.shape, F32)


def _decode_queries(q_ref):
    t_new = q_ref.shape[0]
    hw = 2 * HD_B
    q = q_ref[...].astype(BF16)
    qs = [_split_heads_stack(q[:, h * hw:(h + 1) * hw], HD_B) for h in range(H_B)]
    return qs, [slice(2 * t_new * h, 2 * t_new * (h + 1)) for h in range(H_B)]


def _decode_page_heads(page_refs, first_row):
    return jnp.concatenate([r[pl.ds(first_row, PAGE_SIZE, stride=2 * H_B), :] for r in page_refs],
                           axis=0).astype(BF16)


def _decode_finish(lam_init, qs, head_rows, kvnew_ref, blam_ref, sg_ref, o_ref, m_scr, l_scr, acc_scr, pad_scr):
    t_new = kvnew_ref.shape[0]
    hw = 2 * HD_B
    pad_scr[...] = jnp.zeros(pad_scr.shape, F32)
    pad_scr[0:t_new, :] = kvnew_ref[...]
    shape = (2 * t_new, pad_scr.shape[0])
    row = lax.broadcasted_iota(jnp.int32, shape, 0)
    col = lax.broadcasted_iota(jnp.int32, shape, 1)
    valid = col <= jnp.where(row >= t_new, row - t_new, row)
    scores = [jnp.where(valid, _dot_nt(qs[h], pad_scr[:, h * hw:(h + 1) * hw].astype(BF16)), NEG_BIG)
              for h in range(H_B)]
    values = [pad_scr[:, D_B + h * hw:D_B + (h + 1) * hw].astype(BF16) for h in range(H_B)]
    _online_update(head_rows, scores, values, valid, m_scr, l_scr, acc_scr)
    lam = _diff_lambda(blam_ref[...], lam_init)
    for h, rows in enumerate(head_rows):
        o = _diff_finalize(acc_scr[rows], l_scr[rows], lam, sg_ref[...], lam_init, t_new)
        o_ref[:, h * hw:(h + 1) * hw] = o.astype(o_ref.dtype)


def _diffattn_decode_kernel(pages_per_step, lam_init, pt_ref, q_ref, kvnew_ref, blam_ref, sg_ref, *rest):
    page_refs = rest[:pages_per_step]
    o_ref, m_scr, l_scr, acc_scr, pad_scr = rest[pages_per_step:]
    j = pl.program_id(1)
    qs, head_rows = _decode_queries(q_ref)

    @pl.when(j == 0)
    def _():
        _init_online(m_scr, l_scr, acc_scr)

    scores = [_dot_nt(qs[h], _decode_page_heads(page_refs, h)) for h in range(H_B)]
    values = [_decode_page_heads(page_refs, H_B + h) for h in range(H_B)]
    _online_update(head_rows, scores, values, None, m_scr, l_scr, acc_scr)

    @pl.when(j == pl.num_programs(1) - 1)
    def _():
        _decode_finish(lam_init, qs, head_rows, kvnew_ref, blam_ref, sg_ref, o_ref, m_scr, l_scr, acc_scr, pad_scr)


def _diffattn_decode(page_table, cache, q, kvnew, blam, sg, lam_init, n_batch, t_new, pages_per_step):
    n_pages = page_table.shape[1]
    hw = 2 * HD_B
    page_specs = [
        pl.BlockSpec((None, PAGE_SIZE * 2 * H_B, hw), functools.partial(
            lambda b, j, pt, t: (pt[b, j * pages_per_step + t], 0, 0), t=t))
        for t in range(pages_per_step)]
    grid_spec = pltpu.PrefetchScalarGridSpec(
        num_scalar_prefetch=1,
        grid=(n_batch, n_pages // pages_per_step),
        in_specs=[pl.BlockSpec((None, t_new, D_B), lambda b, j, pt: (b, 0, 0)),
                  pl.BlockSpec((None, t_new, 2 * D_B), lambda b, j, pt: (b, 0, 0)),
                  pl.BlockSpec(blam.shape, lambda b, j, pt: (0, 0)),
                  pl.BlockSpec((1, hw), lambda b, j, pt: (0, 0))] + page_specs,
        out_specs=pl.BlockSpec((None, t_new, D_B), lambda b, j, pt: (b, 0, 0)),
        scratch_shapes=[pltpu.VMEM((2 * t_new * H_B, 1), F32), pltpu.VMEM((2 * t_new * H_B, 1), F32),
                        pltpu.VMEM((2 * t_new * H_B, hw), F32), pltpu.VMEM((PAGE_SIZE, 2 * D_B), F32)])
    out = pl.pallas_call(
        functools.partial(_diffattn_decode_kernel, pages_per_step, lam_init),
        grid_spec=grid_spec,
        out_shape=jax.ShapeDtypeStruct((n_batch, t_new, D_B), F32),
        compiler_params=_params("arbitrary", "arbitrary"),
    )(page_table, q.reshape(n_batch, t_new, D_B), kvnew.reshape(n_batch, t_new, 2 * D_B), blam, sg,
      *([cache] * pages_per_step))
    return out.reshape(n_batch * t_new, D_B)


def _post_kernel(dils, ff_chunk, has_final, x_ref, first_ref, *rest):
    n_g = max(len(dils), 1)
    o_refs = rest[:n_g]
    lse_refs = rest[n_g:n_g + len(dils)]
    rest = rest[n_g + len(dils):]
    wout_ref, g_ref, wup_ref, wdn_ref = rest[:4]
    rest = rest[4:]
    gf_ref = rest[0] if has_final else None
    o_ref = rest[1] if has_final else rest[0]
    scratch = list(rest[2 if has_final else 1:])
    if not dils:
        second = o_refs[0][...].astype(BF16)
    else:
        scr_of = lambda d: scratch.pop(0) if d > 1 else None
        os_ = [_rows_from_streams(r, d, scr_of(d)) for r, d in zip(o_refs, dils)]
        ls_ = [_rows_from_streams(r, d, scr_of(d)) for r, d in zip(lse_refs, dils)]
        merged = []
        for c in range(len(os_[0])):
            lses = [l[c] for l in ls_]
            mx = functools.reduce(jnp.maximum, lses)
            es = [jnp.exp(l - mx) for l in lses]
            num = functools.reduce(lambda a, b: a + b, [e * o[c] for e, o in zip(es, os_)])
            merged.append((num / functools.reduce(lambda a, b: a + b, es)).astype(BF16))
        second = jnp.concatenate(merged, axis=-1)
    mix = jnp.concatenate([first_ref[...].astype(BF16), second], axis=-1)
    x1 = x_ref[...] + _dot(mix, wout_ref[...])
    h = _rmsnorm(x1, g_ref[...]).astype(BF16)
    acc = x1
    up = _dot(h, wup_ref[:, 0:ff_chunk])
    for c in range(0, D_FF, ff_chunk):
        nxt = _dot(h, wup_ref[:, c + ff_chunk:c + 2 * ff_chunk]) if c + ff_chunk < D_FF else None
        act = jnp.square(jnp.maximum(up, 0.0)).astype(BF16)
        acc = acc + _dot(act, wdn_ref[c:c + ff_chunk, :])
        up = nxt
    if has_final:
        acc = _rmsnorm(acc, gf_ref[...])
    o_ref[...] = acc


def _post_mixer(x, first, o_parts, lse_parts, dils, wout, g, wup, wdn, gf, tm):
    m = x.shape[0]
    row = lambda n: pl.BlockSpec((tm, n), lambda i: (i, 0))

    def part_spec(a, dil):
        if dil == 1:
            return row(a.shape[1])
        tiles_per_seq = a.shape[1] * dil // tm
        return pl.BlockSpec((None, tm // dil, a.shape[2]), lambda i: (i // tiles_per_seq, i % tiles_per_seq, 0))

    has_final = gf is not None
    part_dils = tuple(dils) if dils else (1,)
    args = [x, first, *o_parts, *lse_parts, wout, g, wup, wdn]
    specs = [row(D_MODEL), row(first.shape[1])]
    specs += [part_spec(a, d) for a, d in zip(o_parts, part_dils)] + [part_spec(a, d) for a, d in zip(lse_parts, dils)]
    specs += [_resident(wout.shape), _resident((1, D_MODEL)), _resident(wup.shape), _resident(wdn.shape)]
    if has_final:
        args.append(gf)
        specs.append(_resident((1, D_MODEL)))
    n_scratch = 2 * sum(d > 1 for d in dils)
    return pl.pallas_call(
        functools.partial(_post_kernel, tuple(dils), FF_CHUNK, has_final),
        grid=(m // tm,),
        in_specs=specs,
        out_specs=row(D_MODEL),
        out_shape=jax.ShapeDtypeStruct((m, D_MODEL), F32),
        scratch_shapes=[pltpu.VMEM((D_DG // LANES, tm, LANES), F32)] * n_scratch,
        compiler_params=_params("arbitrary"),
    )(*args)


def _proj_cd_kernel(tiles_per_seq, tails, x_ref, g_ref, w_ref, cw_ref, hist_ref, *rest):
    n_g = len(D_WINDOWS)
    c_ref, ztail_ref = rest[0], rest[1]
    group_refs = rest[2:2 + 4 * n_g] if tiles_per_seq else rest[2:2 + 2 * n_g]
    carry_ref, stream_scr = rest[-2], rest[-1]
    tm = x_ref.shape[0]
    i = pl.program_id(0)
    h = _rmsnorm(x_ref[...], g_ref[...]).astype(BF16)
    gates = _dot(h, w_ref[:, 0:3 * D_C])
    b_gate = gates[:, 0:D_C]
    z = gates[:, D_C:2 * D_C] * gates[:, 2 * D_C:3 * D_C]
    row = lax.broadcasted_iota(jnp.int32, z.shape, 0)
    if tiles_per_seq:
        first = (i % tiles_per_seq) == 0
        prev = jnp.where(first, hist_ref[...], carry_ref[...])
        z1 = jnp.where(row == 0, prev[7:8], pltpu.roll(z, 1, 0))
        z2 = jnp.where(row == 0, prev[6:7], jnp.where(row == 1, prev[7:8], pltpu.roll(z, 2, 0)))
        carry_ref[...] = z[tm - SUBLANES:tm]
        ztail_ref[...] = z[tm - SUBLANES:tm]
    else:
        e = hist_ref[...]
        t = row & (SUBLANES - 1)
        z1 = jnp.where(t == 0, pltpu.roll(e, tm - 7, 0), pltpu.roll(z, 1, 0))
        z2 = jnp.where(t < 2, pltpu.roll(e, tm - 6, 0), pltpu.roll(z, 2, 0))
        ztail_ref[...] = z
    cw = cw_ref[...]
    cv = z2 * cw[0:1] + z1 * cw[1:2] + z * cw[2:3]
    c_ref[...] = (b_gate * cv).astype(c_ref.dtype)
    for gi in reversed(range(n_g)):
        off = 3 * D_C + 3 * gi * D_DG
        res = _dot(h, w_ref[:, off:off + 3 * D_DG])
        q = res[:, 0:D_DG] * QK_SCALE_D
        if tiles_per_seq:
            q_ref, k_ref, v_ref, kvt_ref = group_refs[4 * gi:4 * gi + 4]
            dil = D_WINDOWS[gi][1]
            _store_streams(q_ref, q, dil, stream_scr)
            _store_streams(k_ref, res[:, D_DG:2 * D_DG], dil, stream_scr)
            _store_streams(v_ref, res[:, 2 * D_DG:3 * D_DG], dil, stream_scr)
            kvt_ref[...] = res[tm - tails[gi]:tm, D_DG:3 * D_DG].T
        else:
            q_ref, kvt_ref = group_refs[2 * gi:2 * gi + 2]
            q_ref[...] = q
            kvt_ref[...] = res[:, D_DG:3 * D_DG].T


def _proj_cd_prompt(x, g, w, cw, n_batch, seq, tm):
    m = x.shape[0]
    tps = seq // tm
    row = lambda n: pl.BlockSpec((tm, n), lambda i: (i, 0))
    hist = jnp.zeros((n_batch * SUBLANES, D_C), F32)
    out_shape = [jax.ShapeDtypeStruct((m, D_C), BF16), jax.ShapeDtypeStruct((m // tm * SUBLANES, D_C), F32)]
    out_specs = [row(D_C), pl.BlockSpec((SUBLANES, D_C), lambda i: (i, 0))]
    tails = []
    for win, dil in D_WINDOWS:
        keep = min(win, seq)
        tail = min(keep, tm)
        first_kept = tps - keep // tail
        tails.append(tail)
        out_shape += [jax.ShapeDtypeStruct((n_batch, seq // dil, dil * D_DG), BF16)] * 3
        out_shape += [jax.ShapeDtypeStruct((n_batch, 2 * D_DG, keep), F32)]
        out_specs += [pl.BlockSpec((None, tm // dil, dil * D_DG), lambda i: (i // tps, i % tps, 0))] * 3
        out_specs += [pl.BlockSpec((None, 2 * D_DG, tail), functools.partial(
            lambda i, fk: (i // tps, 0, jnp.maximum(i % tps - fk, 0)), fk=first_kept))]
    return pl.pallas_call(
        functools.partial(_proj_cd_kernel, tps, tuple(tails)),
        grid=(m // tm,),
        in_specs=[row(D_MODEL), _resident((1, D_MODEL)), _resident(w.shape),
                  _resident(cw.shape), pl.BlockSpec((SUBLANES, D_C), lambda i: (i // tps, 0))],
        out_specs=out_specs,
        out_shape=out_shape,
        scratch_shapes=[pltpu.VMEM((SUBLANES, D_C), F32), pltpu.VMEM((D_DG // LANES, tm, LANES), F32)],
        compiler_params=_params("arbitrary"),
    )(x, g, w, cw, hist)


def _proj_cd_sample(x, g, w, cw, hist_rows):
    m = x.shape[0]
    full = lambda r, n: pl.BlockSpec((r, n), lambda i: (0, 0))
    out_shape = [jax.ShapeDtypeStruct((m, D_C), F32), jax.ShapeDtypeStruct((m, D_C), F32)]
    out_specs = [full(m, D_C), full(m, D_C)]
    for _ in D_WINDOWS:
        out_shape += [jax.ShapeDtypeStruct((m, D_DG), F32), jax.ShapeDtypeStruct((2 * D_DG, m), F32)]
        out_specs += [full(m, D_DG), full(2 * D_DG, m)]
    return pl.pallas_call(
        functools.partial(_proj_cd_kernel, 0, ()),
        grid=(1,),
        in_specs=[full(m, D_MODEL), _resident((1, D_MODEL)), _resident(w.shape),
                  _resident(cw.shape), full(m, D_C)],
        out_specs=out_specs,
        out_shape=out_shape,
        scratch_shapes=[pltpu.VMEM((SUBLANES, D_C), F32), pltpu.VMEM((D_DG // LANES, SUBLANES, LANES), F32)],
        compiler_params=_params("arbitrary"),
    )(x, g, w, cw, hist_rows)


def _dilated_prompt_kernel(n_streams, q_ref, kc_ref, kp_ref, vc_ref, vp_ref, o_ref, lse_ref):
    bl = N_BACK
    n_qblk = q_ref.shape[0] // bl
    first_blk = pl.program_id(2) * n_qblk
    shape = (bl, 2 * bl)
    row = lax.broadcasted_iota(jnp.int32, shape, 0)
    col = lax.broadcasted_iota(jnp.int32, shape, 1)
    dist = row - col + bl
    band = (dist >= 0) & (dist <= N_BACK)
    hw = 2 * HD_D
    lane = lax.broadcasted_iota(jnp.int32, (bl, hw), 1)
    for r in range(n_streams):
        for qb in range(n_qblk):
            rows = slice(qb * bl, (qb + 1) * bl)
            prev_rows = slice((qb - 1) * bl, qb * bl)
            valid = (band & ((first_blk * bl + col - bl) >= 0)) if qb == 0 else band
            pairs = [slice(r * D_DG + p * hw, r * D_DG + (p + 1) * hw) for p in range(H_DG // 2)]
            scores = []
            for sl in pairs:
                qq = _split_heads_stack(q_ref[rows, sl], HD_D)
                k_prev = kp_ref[:, sl] if qb == 0 else kc_ref[prev_rows, sl]
                kcat = jnp.concatenate([k_prev, kc_ref[rows, sl]], axis=0)
                scores.append([_dot_nt(qq[e * bl:(e + 1) * bl], kcat) for e in range(2)])
            probs = []
            for pair_scores in scores:
                stats = []
                for s in pair_scores:
                    s = jnp.where(valid, s, -jnp.inf)
                    m = jnp.max(s, axis=-1, keepdims=True)
                    pr = jnp.exp2(s - m)
                    l = jnp.sum(pr, axis=-1, keepdims=True)
                    stats.append((pr.astype(BF16), l, (m + jnp.log2(l)) * LN_2))
                probs.append(stats)
            for sl, stats in zip(pairs, probs):
                v_prev = vp_ref[:, sl] if qb == 0 else vc_ref[prev_rows, sl]
                vcat = jnp.concatenate([v_prev, vc_ref[rows, sl]], axis=0)
                outs = [_dot(pr, vcat) / l for pr, l, _ in stats]
                o_ref[rows, sl] = jnp.where(lane < HD_D, outs[0], outs[1]).astype(o_ref.dtype)
                lse_ref[rows, sl] = jnp.where(lane < HD_D, stats[0][2], stats[1][2])


def _dilated_prompt(q, k, v, dil, blocks_per_step):
    n_batch, length, _ = q.shape
    bl = N_BACK
    n_qblk = min(blocks_per_step, length // bl)
    n_streams = min(blocks_per_step // n_qblk, dil)
    rows, width = n_qblk * bl, n_streams * D_DG
    cur = pl.BlockSpec((None, rows, width), lambda b, r, j: (b, j, r))
    prev = pl.BlockSpec((None, bl, width), lambda b, r, j: (b, jnp.maximum(j * n_qblk - 1, 0), r))
    return pl.pallas_call(
        functools.partial(_dilated_prompt_kernel, n_streams),
        grid=(n_batch, dil // n_streams, length // rows),
        in_specs=[cur, cur, prev, cur, prev],
        out_specs=[cur, cur],
        out_shape=[jax.ShapeDtypeStruct(q.shape, BF16), jax.ShapeDtypeStruct(q.shape, F32)],
        compiler_params=_params("arbitrary", "arbitrary", "arbitrary"),
    )(q, k, k, v, v)


def _dilated_sample_kernel(dil, q_ref, cache_ref, kvnew_ref, newc_ref, o_ref, lse_ref):
    n_seq = q_ref.shape[0]
    for s in range(n_seq):
        _dilated_sample_one(dil, pl.program_id(0) * n_seq + s, q_ref.at[s], cache_ref.at[s], kvnew_ref,
                            newc_ref.at[s], o_ref.at[s], lse_ref.at[s])


def _dilated_sample_one(dil, seq, q_ref, cache_ref, kvnew_ref, newc_ref, o_ref, lse_ref):
    buf_len = cache_ref.shape[1]
    t_new = q_ref.shape[0]
    hw = 2 * HD_D
    n_blk = buf_len // LANES

    seq_in_block = seq % (LANES // t_new)
    new_tile = pltpu.roll(kvnew_ref[...], (LANES - seq_in_block * t_new) % LANES, 1)

    lane = lax.broadcasted_iota(jnp.int32, (cache_ref.shape[0], LANES), 1)
    cur = pltpu.roll(cache_ref[:, 0:LANES], LANES - t_new, 1)
    for c in range(n_blk):
        following = cache_ref[:, (c + 1) * LANES:(c + 2) * LANES] if c + 1 < n_blk else new_tile
        nxt = pltpu.roll(following, LANES - t_new, 1)
        newc_ref[:, c * LANES:(c + 1) * LANES] = jnp.where(lane < LANES - t_new, cur, nxt)
        cur = nxt

    def iotas(n_keys):
        shape = (2 * t_new, n_keys)
        row = lax.broadcasted_iota(jnp.int32, shape, 0)
        return row & (t_new - 1), lax.broadcasted_iota(jnp.int32, shape, 1)

    qi, col = iotas(buf_len)
    d = buf_len + qi - col
    valid_buf = ((d & (dil - 1)) == 0) & (d <= N_BACK * dil)
    qi, col = iotas(LANES)
    d = qi - col
    valid_new = (col < t_new) & (d >= 0) & ((d & (dil - 1)) == 0)

    q = q_ref[...].astype(BF16)
    lane = lax.broadcasted_iota(jnp.int32, (t_new, hw), 1)
    for p in range(H_DG // 2):
        k_rows, v_rows = slice(p * hw, (p + 1) * hw), slice(D_DG + p * hw, D_DG + (p + 1) * hw)
        qq = _split_heads_stack(q[:, k_rows], HD_D)
        s_buf = jnp.where(valid_buf, _dot(qq, cache_ref[k_rows, :].astype(BF16)), NEG_BIG)
        s_new = jnp.where(valid_new, _dot(qq, new_tile[k_rows].astype(BF16)), NEG_BIG)
        m = jnp.maximum(jnp.max(s_buf, axis=-1, keepdims=True), jnp.max(s_new, axis=-1, keepdims=True))
        p_buf = jnp.where(valid_buf, jnp.exp2(s_buf - m), 0.0)
        p_new = jnp.where(valid_new, jnp.exp2(s_new - m), 0.0)
        l = jnp.sum(p_buf, axis=-1, keepdims=True) + jnp.sum(p_new, axis=-1, keepdims=True)
        acc = (_dot_nt(p_buf.astype(BF16), cache_ref[v_rows, :].astype(BF16))
               + _dot_nt(p_new.astype(BF16), new_tile[v_rows].astype(BF16)))
        o = acc / l
        lse = (m + jnp.log2(l)) * LN_2
        o_ref[:, k_rows] = jnp.where(lane < HD_D, o[0:t_new], o[t_new:2 * t_new])
        lse_ref[:, k_rows] = jnp.where(lane < HD_D, lse[0:t_new], lse[t_new:2 * t_new])


def _dilated_sample(q, cache_t, kvnew_t, dil, n_batch, t_new, seqs_per_step):
    buf_len = cache_t.shape[2]
    steps_per_lane_block = LANES // t_new // seqs_per_step
    small = lambda n: pl.BlockSpec((seqs_per_step, t_new, n), lambda b: (b, 0, 0))
    whole = pl.BlockSpec((seqs_per_step, 2 * D_DG, buf_len), lambda b: (b, 0, 0))
    newc, o, lse = pl.pallas_call(
        functools.partial(_dilated_sample_kernel, dil),
        grid=(n_batch // seqs_per_step,),
        in_specs=[small(D_DG), whole, pl.BlockSpec((2 * D_DG, LANES), lambda b: (0, b // steps_per_lane_block))],
        out_specs=[whole, small(D_DG), small(D_DG)],
        out_shape=[jax.ShapeDtypeStruct(cache_t.shape, F32),
                   jax.ShapeDtypeStruct((n_batch, t_new, D_DG), F32),
                   jax.ShapeDtypeStruct((n_batch, t_new, D_DG), F32)],
        compiler_params=_params("arbitrary"),
    )(q.reshape(n_batch, t_new, D_DG), cache_t, kvnew_t)
    return newc, o.reshape(n_batch * t_new, D_DG), lse.reshape(n_batch * t_new, D_DG)


def kernel(x_prompt, x_sample, state_a_pool, cache_b_kv, state_c_conv, cache_d0_kv, cache_d1_kv, cache_d2_kv, page_table, norm_mix_g, norm_mlp_g, norm_out_g, w_in_ab, w_out_ab, a_mix, a_scale, b_lam, b_subln_g, w_in_cd, w_out_cd, c_conv_w, w_up, w_down):
    bp, seq, _ = x_prompt.shape
    bs, t_new, _ = x_sample.shape
    n_pages = page_table.shape[1]
    past_len = n_pages * cache_b_kv.shape[2]
    mp, ms = bp * seq, bs * t_new
    tm_p = ROW_TILE
    xp = x_prompt.reshape(mp, D_MODEL)
    xs = x_sample.reshape(ms, D_MODEL)
    g2 = lambda v: v.reshape(1, -1)
    d_caches = (cache_d0_kv, cache_d1_kv, cache_d2_kv)

    lam_init = 0.8 - 0.6 * math.exp(-0.3 * 0)
    w_in = w_in_ab[0].astype(BF16)
    w_out = w_out_ab[0].astype(BF16)
    wu, wd = w_up[0].astype(BF16), w_down[0].astype(BF16)
    amix = a_mix[0].astype(BF16)
    gm, gl = g2(norm_mix_g[0]), g2(norm_mlp_g[0])
    ascale, sg = g2(a_scale[0]), g2(b_subln_g[0])

    u_p, q_p, kv_p, k_p, vt_p = _proj_ab(xp, gm, w_in, tm_p, True, ATTN_KEY_BLOCK)
    hist_p = jnp.zeros((bp, POOL_HIST + 1, D_A), F32)
    a_p = _pool_mix(u_p, hist_p, amix, ascale, 0, bp, seq)
    ob_p = _diffattn_prompt(q_p, k_p, vt_p, b_lam[0], sg, lam_init, bp, seq, ATTN_QUERY_CHUNK)

    u_s, q_s, kv_s, _, _ = _proj_ab(xs, gm, w_in, ms, False, ATTN_KEY_BLOCK)
    hist_s = jnp.pad(state_a_pool[0], ((0, 0), (1, 0), (0, 0)))
    a_s = _pool_mix(u_s, hist_s, amix, ascale, past_len, bs, t_new)
    pool_pages = cache_b_kv[0].reshape(cache_b_kv.shape[1], PAGE_SIZE * 2 * H_B, 2 * HD_B)
    xp = _post_mixer(xp, a_p, (ob_p,), (), (), w_out, gl, wu, wd, None, tm_p)
    ob_s = _diffattn_decode(page_table, pool_pages, q_s, kv_s, b_lam[0], sg, lam_init, bs, t_new,
                            min(PAGES_PER_STEP, n_pages))
    xs = _post_mixer(xs, a_s, (ob_s,), (), (), w_out, gl, wu, wd, None, ms)

    u_p3 = u_p.reshape(bp, seq, D_A)
    new_pool_p = u_p3[:, seq - POOL_HIST:][None]
    new_pool_s = jnp.concatenate([state_a_pool[0], u_s.reshape(bs, t_new, D_A)], axis=1)[:, -POOL_HIST:][None]
    new_bkv_p = kv_p.reshape(1, bp, seq, 2, H_B, 2 * HD_B)
    new_bkv_s = kv_s.reshape(1, bs, t_new, 2, H_B, 2 * HD_B)

    w_in = w_in_cd[0].astype(BF16)
    w_out = w_out_cd[0].astype(BF16)
    wu, wd = w_up[1].astype(BF16), w_down[1].astype(BF16)
    gm, gl, gf = g2(norm_mix_g[1]), g2(norm_mlp_g[1]), g2(norm_out_g)
    cw = c_conv_w[0]

    def from_position_minor(a):
        a = a.reshape(a.shape[0], 2, H_DG, HD_D, a.shape[2])
        return jnp.transpose(a, (0, 4, 1, 2, 3))[None]

    outs = _proj_cd_prompt(xp, gm, w_in, cw, bp, seq, tm_p)
    c_p, ztail_p = outs[0], outs[1]
    o_parts, lse_parts, new_d_p = [], [], []
    for gi, (win, dil) in enumerate(D_WINDOWS):
        q_g, k_g, v_g, kvt_g = outs[2 + 4 * gi:6 + 4 * gi]
        o_g, lse_g = _dilated_prompt(q_g, k_g, v_g, dil, DILATED_BLOCKS_PER_STEP)
        if dil == 1:
            o_g, lse_g = o_g.reshape(mp, D_DG), lse_g.reshape(mp, D_DG)
        o_parts.append(o_g)
        lse_parts.append(lse_g)
        new_d_p.append(from_position_minor(kvt_g))
    dils = tuple(dil for _, dil in D_WINDOWS)
    y_p = _post_mixer(xp, c_p, o_parts, lse_parts, dils, w_out, gl, wu, wd, gf, tm_p)
    tiles_per_seq = seq // tm_p
    new_conv_p = ztail_p.reshape(bp, tiles_per_seq, SUBLANES, D_C)[:, -1, SUBLANES - (CONV_W - 1):][None]

    hist_rows = jnp.pad(state_c_conv[0], ((0, 0), (t_new - (CONV_W - 1), 0), (0, 0))).reshape(ms, D_C)
    outs = _proj_cd_sample(xs, gm, w_in, cw, hist_rows)
    c_s, z_s = outs[0], outs[1]
    o_parts, lse_parts, new_d_s = [], [], []
    for gi, (win, dil) in enumerate(D_WINDOWS):
        q_g, kvt_g = outs[2 + 2 * gi:4 + 2 * gi]
        cache = d_caches[gi][0]
        buf_len = cache.shape[1]
        cache_t = jnp.transpose(cache, (0, 2, 3, 4, 1)).reshape(bs, 2 * D_DG, buf_len)
        seqs_per_step = max(1, min(SUBLANES, D_WINDOWS[-1][0] // (2 * buf_len)))
        newc, o_g, lse_g = _dilated_sample(q_g, cache_t, kvt_g, dil, bs, t_new, seqs_per_step)
        o_parts.append(o_g)
        lse_parts.append(lse_g)
        new_d_s.append(from_position_minor(newc))
    y_s = _post_mixer(xs, c_s, o_parts, lse_parts, (1,) * len(D_WINDOWS), w_out, gl, wu, wd, gf, ms)
    new_conv_s = z_s.reshape(bs, t_new, D_C)[:, t_new - (CONV_W - 1):][None]

    return (y_p.reshape(bp, seq, D_MODEL), y_s.reshape(bs, t_new, D_MODEL),
            new_pool_p, new_pool_s, new_bkv_p, new_bkv_s, new_conv_p, new_conv_s,
            new_d_p[0], new_d_s[0], new_d_p[1], new_d_s[1], new_d_p[2], new_d_s[2])
```

```python
import functools
import math

import jax
import jax.numpy as jnp
from jax import lax
from jax.experimental import pallas as pl
from jax.experimental.pallas import tpu as pltpu

F32 = jnp.float32
BF16 = jnp.bfloat16

D_MODEL = 1024
RMS_EPS = 1e-6
D_A = 512
POOL_WINDOWS = (2, 4, 8, 16)
POOL_GROUP = 128
POOL_HIST = 15
H_B = 4
HD_B = 64
D_B = 512
D_C = 512
CONV_W = 3
D_WINDOWS = ((128, 1), (512, 4), (2048, 16))
N_BACK = 128
H_DG = 8
HD_D = 64
D_DG = 512
D_FF = 4096
PAGE_SIZE = 128

LANES = 128
SUBLANES = 8
VMEM_LIMIT_BYTES = 56 * 1024 * 1024
NEG_BIG = -1e30

ROW_TILE = 512
FF_CHUNK = 1024
ATTN_QUERY_CHUNK = 512
ATTN_KEY_BLOCK = 512
PAGES_PER_STEP = 32
DILATED_BLOCKS_PER_STEP = 16

LOG2_E = math.log2(math.e)
LN_2 = math.log(2.0)
QK_SCALE_B = HD_B ** -0.5 * LOG2_E
QK_SCALE_D = HD_D ** -0.5 * LOG2_E


def _params(*sem):
    return pltpu.CompilerParams(dimension_semantics=sem, vmem_limit_bytes=VMEM_LIMIT_BYTES)


def _resident(shape):
    nd = len(shape)
    return pl.BlockSpec(shape, lambda *_: (0,) * nd, pipeline_mode=pl.Buffered(1))


def _rmsnorm(x, g):
    return x * lax.rsqrt(jnp.mean(x * x, axis=-1, keepdims=True) + RMS_EPS) * g


def _dot(a, b):
    return jnp.dot(a, b, preferred_element_type=F32)


def _dot_nt(a, b):
    return lax.dot_general(a, b, (((1,), (1,)), ((), ())), preferred_element_type=F32)


def _split_heads_stack(q, half):
    lane = lax.broadcasted_iota(jnp.int32, q.shape, 1)
    zero = jnp.zeros_like(q)
    return jnp.concatenate([jnp.where(lane < half, q, zero), jnp.where(lane >= half, q, zero)], axis=0)


def _lane_blocks(n):
    return [slice(c * LANES, (c + 1) * LANES) for c in range(n // LANES)]


def _store_streams(ref, val, dil, scr):
    if dil == 1:
        ref[...] = val.astype(ref.dtype)
        return
    width = val.shape[1]
    for c, sl in enumerate(_lane_blocks(width)):
        scr[c] = val[:, sl]
    for r in range(dil):
        for c, sl in enumerate(_lane_blocks(width)):
            ref[:, r * width + sl.start:r * width + sl.stop] = (
                scr[c, pl.ds(r, ref.shape[0], stride=dil), :].astype(ref.dtype))


def _rows_from_streams(ref, dil, scr):
    width = ref.shape[1] // dil
    if dil == 1:
        return [ref[:, sl].astype(F32) for sl in _lane_blocks(width)]
    for r in range(dil):
        for c, sl in enumerate(_lane_blocks(width)):
            scr[c, pl.ds(r, ref.shape[0], stride=dil), :] = ref[:, r * width + sl.start:r * width + sl.stop].astype(F32)
    return [scr[c] for c in range(width // LANES)]


def _diff_lambda(blam, lam_init):
    a = jnp.sum(blam[0:1] * blam[1:2], axis=-1, keepdims=True)
    b = jnp.sum(blam[2:3] * blam[3:4], axis=-1, keepdims=True)
    return jnp.exp(a) - jnp.exp(b) + lam_init


def _proj_ab_kernel(prompt, x_ref, g_ref, w_ref, u_ref, q_ref, kv_ref, k_ref, v_ref):
    h = _rmsnorm(x_ref[...], g_ref[...]).astype(BF16)
    kv = _dot(h, w_ref[:, D_A + D_B:D_A + 3 * D_B])
    u_ref[...] = _dot(h, w_ref[:, 0:D_A])
    q_ref[...] = (_dot(h, w_ref[:, D_A:D_A + D_B]) * QK_SCALE_B).astype(q_ref.dtype)
    k_ref[...] = kv[:, :D_B].astype(BF16)
    if not prompt:
        kv_ref[...] = kv
        v_ref[...] = kv[:, D_B:].astype(BF16)
        return
    n_blk = kv.shape[1] // LANES
    for c, sl in enumerate(_lane_blocks(kv.shape[1])):
        kv_ref[pl.ds(c, kv.shape[0], stride=n_blk), :] = kv[:, sl]
    vt = kv[:, D_B:].T.astype(BF16)
    tk = v_ref.shape[2]
    for j in range(v_ref.shape[0]):
        v_ref[j] = vt[:, j * tk:(j + 1) * tk]


def _proj_ab(x, g, w, tm, prompt, tk):
    m = x.shape[0]
    n_in = w.shape[1]
    row = lambda n: pl.BlockSpec((tm, n), lambda i: (i, 0))
    n_blk = 2 * D_B // LANES
    if prompt:
        kv_shape, kv_spec = (m * n_blk, LANES), pl.BlockSpec((tm * n_blk, LANES), lambda i: (i, 0))
        v_shape, v_spec = (m // tk, D_B, tk), pl.BlockSpec((tm // tk, D_B, tk), lambda i: (i, 0, 0))
    else:
        kv_shape, kv_spec = (m, 2 * D_B), row(2 * D_B)
        v_shape, v_spec = (m, D_B), row(D_B)
    return pl.pallas_call(
        functools.partial(_proj_ab_kernel, prompt),
        grid=(m // tm,),
        in_specs=[row(D_MODEL), _resident((1, D_MODEL)), _resident((D_MODEL, n_in))],
        out_specs=[row(D_A), row(D_B), kv_spec, row(D_B), v_spec],
        out_shape=[jax.ShapeDtypeStruct((m, D_A), F32), jax.ShapeDtypeStruct((m, D_B), BF16 if prompt else F32),
                   jax.ShapeDtypeStruct(kv_shape, F32), jax.ShapeDtypeStruct((m, D_B), BF16),
                   jax.ShapeDtypeStruct(v_shape, BF16)],
        compiler_params=_params("arbitrary"),
    )(x, g, w)


def _pool_kernel(pos0, tchunk, u_ref, hist_ref, amix_ref, ascale_ref, o_ref, ext_ref):
    t_len = u_ref.shape[0]
    hpad = hist_ref.shape[0]
    ext_ref[0:hpad, :] = hist_ref[...]
    ext_ref[hpad:hpad + t_len, :] = u_ref[...]
    for t0 in range(0, t_len, tchunk):
        pos = pos0 + t0 + lax.broadcasted_iota(jnp.int32, (tchunk, 1), 0)
        for gi, w in enumerate(POOL_WINDOWS):
            sl = slice(gi * POOL_GROUP, (gi + 1) * POOL_GROUP)
            u = u_ref[t0:t0 + tchunk, sl]
            acc = u
            for j in range(1, w):
                acc = acc + ext_ref[hpad + t0 - j:hpad + t0 - j + tchunk, sl]
            cnt = jnp.minimum(pos + 1, w).astype(F32)
            p = (acc / cnt - u).astype(BF16)
            y = _dot(p, amix_ref[gi]) * ascale_ref[:, sl]
            o_ref[t0:t0 + tchunk, sl] = y.astype(o_ref.dtype)


def _pool_mix(u, hist16, amix, ascale, pos0, n_seq, t_len):
    hpad = hist16.shape[1]
    tchunk = min(t_len, 256)
    out = pl.pallas_call(
        functools.partial(_pool_kernel, pos0, tchunk),
        grid=(n_seq,),
        in_specs=[pl.BlockSpec((None, t_len, D_A), lambda s: (s, 0, 0)),
                  pl.BlockSpec((None, hpad, D_A), lambda s: (s, 0, 0)),
                  _resident(amix.shape), _resident((1, D_A))],
        out_specs=pl.BlockSpec((None, t_len, D_A), lambda s: (s, 0, 0)),
        out_shape=jax.ShapeDtypeStruct((n_seq, t_len, D_A), BF16 if t_len % 16 == 0 else F32),
        scratch_shapes=[pltpu.VMEM((hpad + t_len, D_A), F32)],
        compiler_params=_params("arbitrary"),
    )(u.reshape(n_seq, t_len, D_A), hist16, amix, ascale)
    return out.reshape(n_seq * t_len, D_A)


def _diff_finalize(acc, l, lam, sg, lam_init, t):
    o = acc[0:t] / l[0:t] - lam * (acc[t:2 * t] / l[t:2 * t])
    o = o * lax.rsqrt(jnp.mean(o * o, axis=-1, keepdims=True) + RMS_EPS) * sg
    return o * (1.0 - lam_init)


def _diffattn_prompt_kernel(lam_init, tc, q_ref, k_ref, vt_ref, blam_ref, sgt_ref, o_ref, m_scr, l_scr, acc_scr):
    seq = q_ref.shape[0]
    tk = vt_ref.shape[2]
    hw = 2 * HD_B
    heads = [slice(h * hw, (h + 1) * hw) for h in range(H_B)]
    lam = _diff_lambda(blam_ref[...], lam_init)
    sgt = sgt_ref[...]
    shape = (tk, 2 * tc)
    key = lax.broadcasted_iota(jnp.int32, shape, 0)
    col = lax.broadcasted_iota(jnp.int32, shape, 1)
    qcol = jnp.where(col >= tc, col - tc, col)

    def scores(j, h, qq):
        off = pl.multiple_of(j * tk, tk)
        return _dot_nt(k_ref[pl.ds(off, tk), heads[h]], qq)

    def chunk(c, carry):
        q0 = pl.multiple_of(c * tc, tc)
        jd = q0 // tk
        qqs = [_split_heads_stack(q_ref[pl.ds(q0, tc), sl], HD_B) for sl in heads]

        causal = (jd * tk + key) <= (q0 + qcol)
        ss = [scores(jd, h, qqs[h]) for h in range(H_B)]
        ps = []
        for h in range(H_B):
            s = jnp.where(causal, ss[h], -jnp.inf)
            m = jnp.max(s, axis=0, keepdims=True)
            p = jnp.exp2(s - m)
            m_scr[h] = m
            l_scr[h] = jnp.sum(p, axis=0, keepdims=True)
            ps.append(p.astype(BF16))
        for h, sl in enumerate(heads):
            acc_scr[h] = _dot(vt_ref[jd, sl, :], ps[h])

        def body(j, inner):
            ss = [scores(j, h, qqs[h]) for h in range(H_B)]
            ps = []
            for h in range(H_B):
                m_prev = m_scr[h]
                m_new = jnp.maximum(m_prev, jnp.max(ss[h], axis=0, keepdims=True))
                alpha = jnp.exp2(m_prev - m_new)
                p = jnp.exp2(ss[h] - m_new)
                l_scr[h] = alpha * l_scr[h] + jnp.sum(p, axis=0, keepdims=True)
                m_scr[h] = m_new
                ps.append((alpha, p.astype(BF16)))
            for h, sl in enumerate(heads):
                alpha, p = ps[h]
                acc_scr[h] = alpha * acc_scr[h] + _dot(vt_ref[j, sl, :], p)
            return inner

        lax.fori_loop(0, jd, body, 0)
        for h, sl in enumerate(heads):
            o = acc_scr[h] / l_scr[h]
            o = o[:, 0:tc] - lam * o[:, tc:2 * tc]
            o = o * lax.rsqrt(jnp.mean(o * o, axis=0, keepdims=True) + RMS_EPS) * sgt * (1.0 - lam_init)
            o_ref[pl.ds(q0, tc), sl] = o.T.astype(o_ref.dtype)
        return carry

    lax.fori_loop(0, seq // tc, chunk, 0)


def _diffattn_prompt(q, k, vt, blam, sg, lam_init, n_batch, seq, tc):
    hw = 2 * HD_B
    tk = vt.shape[2]
    whole = pl.BlockSpec((seq, D_B), lambda b: (b, 0))
    return pl.pallas_call(
        functools.partial(_diffattn_prompt_kernel, lam_init, tc),
        grid=(n_batch,),
        in_specs=[whole, whole, pl.BlockSpec((seq // tk, D_B, tk), lambda b: (b, 0, 0)),
                  _resident(blam.shape), _resident((hw, 1))],
        out_specs=whole,
        out_shape=jax.ShapeDtypeStruct((n_batch * seq, D_B), BF16),
        scratch_shapes=[pltpu.VMEM((H_B, 1, 2 * tc), F32), pltpu.VMEM((H_B, 1, 2 * tc), F32),
                        pltpu.VMEM((H_B, hw, 2 * tc), F32)],
        compiler_params=_params("arbitrary"),
    )(q, k, vt, blam, sg.reshape(hw, 1))


def _online_softmax(rows, scores, valid, m_scr, l_scr):
    steps = []
    for r, s in zip(rows, scores):
        m_prev = m_scr[r]
        m_new = jnp.maximum(m_prev, jnp.max(s, axis=-1, keepdims=True))
        alpha = jnp.exp2(m_prev - m_new)
        p = jnp.exp2(s - m_new)
        if valid is not None:
            p = jnp.where(valid, p, 0.0)
        l_scr[r] = alpha * l_scr[r] + jnp.sum(p, axis=-1, keepdims=True)
        m_scr[r] = m_new
        steps.append((alpha, p.astype(BF16)))
    return steps


def _online_accumulate(rows, steps, values, acc_scr):
    for r, v, (alpha, p) in zip(rows, values, steps):
        acc_scr[r] = alpha * acc_scr[r] + _dot(p, v)


def _online_update(rows, scores, values, valid, m_scr, l_scr, acc_scr):
    _online_accumulate(rows, _online_softmax(rows, scores, valid, m_scr, l_scr), values, acc_scr)


def _init_online(m_scr, l_scr, acc_scr):
    m_scr[...] = jnp.full(m_scr.shape, NEG_BIG, F32)
    l_scr[...] = jnp.zeros(l_scr.shape, F32)
    acc_scr[...] = jnp.zeros(acc_scr.shape, F32)


def _decode_queries(q_ref):
    t_new = q_ref.shape[0]
    hw = 2 * HD_B
    q = q_ref[...].astype(BF16)
    qs = [_split_heads_stack(q[:, h * hw:(h + 1) * hw], HD_B) for h in range(H_B)]
    return qs, [slice(2 * t_new * h, 2 * t_new * (h + 1)) for h in range(H_B)]


def _decode_page_heads(page_refs, first_row):
    return jnp.concatenate([r[pl.ds(first_row, PAGE_SIZE, stride=2 * H_B), :] for r in page_refs],
                           axis=0).astype(BF16)


def _decode_finish(lam_init, qs, head_rows, kvnew_ref, blam_ref, sg_ref, o_ref, m_scr, l_scr, acc_scr, pad_scr):
    t_new = kvnew_ref.shape[0]
    hw = 2 * HD_B
    pad_scr[...] = jnp.zeros(pad_scr.shape, F32)
    pad_scr[0:t_new, :] = kvnew_ref[...]
    shape = (2 * t_new, pad_scr.shape[0])
    row = lax.broadcasted_iota(jnp.int32, shape, 0)
    col = lax.broadcasted_iota(jnp.int32, shape, 1)
    valid = col <= jnp.where(row >= t_new, row - t_new, row)
    scores = [jnp.where(valid, _dot_nt(qs[h], pad_scr[:, h * hw:(h + 1) * hw].astype(BF16)), NEG_BIG)
              for h in range(H_B)]
    values = [pad_scr[:, D_B + h * hw:D_B + (h + 1) * hw].astype(BF16) for h in range(H_B)]
    _online_update(head_rows, scores, values, valid, m_scr, l_scr, acc_scr)
    lam = _diff_lambda(blam_ref[...], lam_init)
    for h, rows in enumerate(head_rows):
        o = _diff_finalize(acc_scr[rows], l_scr[rows], lam, sg_ref[...], lam_init, t_new)
        o_ref[:, h * hw:(h + 1) * hw] = o.astype(o_ref.dtype)


def _diffattn_decode_kernel(pages_per_step, lam_init, pt_ref, q_ref, kvnew_ref, blam_ref, sg_ref, *rest):
    page_refs = rest[:pages_per_step]
    o_ref, m_scr, l_scr, acc_scr, pad_scr = rest[pages_per_step:]
    j = pl.program_id(1)
    qs, head_rows = _decode_queries(q_ref)

    @pl.when(j == 0)
    def _():
        _init_online(m_scr, l_scr, acc_scr)

    scores = [_dot_nt(qs[h], _decode_page_heads(page_refs, h)) for h in range(H_B)]
    values = [_decode_page_heads(page_refs, H_B + h) for h in range(H_B)]
    _online_update(head_rows, scores, values, None, m_scr, l_scr, acc_scr)

    @pl.when(j == pl.num_programs(1) - 1)
    def _():
        _decode_finish(lam_init, qs, head_rows, kvnew_ref, blam_ref, sg_ref, o_ref, m_scr, l_scr, acc_scr, pad_scr)


def _diffattn_decode(page_table, cache, q, kvnew, blam, sg, lam_init, n_batch, t_new, pages_per_step):
    n_pages = page_table.shape[1]
    hw = 2 * HD_B
    page_specs = [
        pl.BlockSpec((None, PAGE_SIZE * 2 * H_B, hw), functools.partial(
            lambda b, j, pt, t: (pt[b, j * pages_per_step + t], 0, 0), t=t))
        for t in range(pages_per_step)]
    grid_spec = pltpu.PrefetchScalarGridSpec(
        num_scalar_prefetch=1,
        grid=(n_batch, n_pages // pages_per_step),
        in_specs=[pl.BlockSpec((None, t_new, D_B), lambda b, j, pt: (b, 0, 0)),
                  pl.BlockSpec((None, t_new, 2 * D_B), lambda b, j, pt: (b, 0, 0)),
                  pl.BlockSpec(blam.shape, lambda b, j, pt: (0, 0)),
                  pl.BlockSpec((1, hw), lambda b, j, pt: (0, 0))] + page_specs,
        out_specs=pl.BlockSpec((None, t_new, D_B), lambda b, j, pt: (b, 0, 0)),
        scratch_shapes=[pltpu.VMEM((2 * t_new * H_B, 1), F32), pltpu.VMEM((2 * t_new * H_B, 1), F32),
                        pltpu.VMEM((2 * t_new * H_B, hw), F32), pltpu.VMEM((PAGE_SIZE, 2 * D_B), F32)])
    out = pl.pallas_call(
        functools.partial(_diffattn_decode_kernel, pages_per_step, lam_init),
        grid_spec=grid_spec,
        out_shape=jax.ShapeDtypeStruct((n_batch, t_new, D_B), F32),
        compiler_params=_params("arbitrary", "arbitrary"),
    )(page_table, q.reshape(n_batch, t_new, D_B), kvnew.reshape(n_batch, t_new, 2 * D_B), blam, sg,
      *([cache] * pages_per_step))
    return out.reshape(n_batch * t_new, D_B)


def _post_kernel(dils, ff_chunk, has_final, x_ref, first_ref, *rest):
    n_g = max(len(dils), 1)
    o_refs = rest[:n_g]
    lse_refs = rest[n_g:n_g + len(dils)]
    rest = rest[n_g + len(dils):]
    wout_ref, g_ref, wup_ref, wdn_ref = rest[:4]
    rest = rest[4:]
    gf_ref = rest[0] if has_final else None
    o_ref = rest[1] if has_final else rest[0]
    scratch = list(rest[2 if has_final else 1:])
    if not dils:
        second = o_refs[0][...].astype(BF16)
    else:
        scr_of = lambda d: scratch.pop(0) if d > 1 else None
        os_ = [_rows_from_streams(r, d, scr_of(d)) for r, d in zip(o_refs, dils)]
        ls_ = [_rows_from_streams(r, d, scr_of(d)) for r, d in zip(lse_refs, dils)]
        merged = []
        for c in range(len(os_[0])):
            lses = [l[c] for l in ls_]
            mx = functools.reduce(jnp.maximum, lses)
            es = [jnp.exp(l - mx) for l in lses]
            num = functools.reduce(lambda a, b: a + b, [e * o[c] for e, o in zip(es, os_)])
            merged.append((num / functools.reduce(lambda a, b: a + b, es)).astype(BF16))
        second = jnp.concatenate(merged, axis=-1)
    mix = jnp.concatenate([first_ref[...].astype(BF16), second], axis=-1)
    x1 = x_ref[...] + _dot(mix, wout_ref[...])
    h = _rmsnorm(x1, g_ref[...]).astype(BF16)
    acc = x1
    up = _dot(h, wup_ref[:, 0:ff_chunk])
    for c in range(0, D_FF, ff_chunk):
        nxt = _dot(h, wup_ref[:, c + ff_chunk:c + 2 * ff_chunk]) if c + ff_chunk < D_FF else None
        act = jnp.square(jnp.maximum(up, 0.0)).astype(BF16)
        acc = acc + _dot(act, wdn_ref[c:c + ff_chunk, :])
        up = nxt
    if has_final:
        acc = _rmsnorm(acc, gf_ref[...])
    o_ref[...] = acc


def _post_mixer(x, first, o_parts, lse_parts, dils, wout, g, wup, wdn, gf, tm):
    m = x.shape[0]
    row = lambda n: pl.BlockSpec((tm, n), lambda i: (i, 0))

    def part_spec(a, dil):
        if dil == 1:
            return row(a.shape[1])
        tiles_per_seq = a.shape[1] * dil // tm
        return pl.BlockSpec((None, tm // dil, a.shape[2]), lambda i: (i // tiles_per_seq, i % tiles_per_seq, 0))

    has_final = gf is not None
    part_dils = tuple(dils) if dils else (1,)
    args = [x, first, *o_parts, *lse_parts, wout, g, wup, wdn]
    specs = [row(D_MODEL), row(first.shape[1])]
    specs += [part_spec(a, d) for a, d in zip(o_parts, part_dils)] + [part_spec(a, d) for a, d in zip(lse_parts, dils)]
    specs += [_resident(wout.shape), _resident((1, D_MODEL)), _resident(wup.shape), _resident(wdn.shape)]
    if has_final:
        args.append(gf)
        specs.append(_resident((1, D_MODEL)))
    n_scratch = 2 * sum(d > 1 for d in dils)
    return pl.pallas_call(
        functools.partial(_post_kernel, tuple(dils), FF_CHUNK, has_final),
        grid=(m // tm,),
        in_specs=specs,
        out_specs=row(D_MODEL),
        out_shape=jax.ShapeDtypeStruct((m, D_MODEL), F32),
        scratch_shapes=[pltpu.VMEM((D_DG // LANES, tm, LANES), F32)] * n_scratch,
        compiler_params=_params("arbitrary"),
    )(*args)


def _proj_cd_kernel(tiles_per_seq, tails, x_ref, g_ref, w_ref, cw_ref, hist_ref, *rest):
    n_g = len(D_WINDOWS)
    c_ref, ztail_ref = rest[0], rest[1]
    group_refs = rest[2:2 + 4 * n_g] if tiles_per_seq else rest[2:2 + 2 * n_g]
    carry_ref, stream_scr = rest[-2], rest[-1]
    tm = x_ref.shape[0]
    i = pl.program_id(0)
    h = _rmsnorm(x_ref[...], g_ref[...]).astype(BF16)
    gates = _dot(h, w_ref[:, 0:3 * D_C])
    b_gate = gates[:, 0:D_C]
    z = gates[:, D_C:2 * D_C] * gates[:, 2 * D_C:3 * D_C]
    row = lax.broadcasted_iota(jnp.int32, z.shape, 0)
    if tiles_per_seq:
        first = (i % tiles_per_seq) == 0
        prev = jnp.where(first, hist_ref[...], carry_ref[...])
        z1 = jnp.where(row == 0, prev[7:8], pltpu.roll(z, 1, 0))
        z2 = jnp.where(row == 0, prev[6:7], jnp.where(row == 1, prev[7:8], pltpu.roll(z, 2, 0)))
        carry_ref[...] = z[tm - SUBLANES:tm]
        ztail_ref[...] = z[tm - SUBLANES:tm]
    else:
        e = hist_ref[...]
        t = row & (SUBLANES - 1)
        z1 = jnp.where(t == 0, pltpu.roll(e, tm - 7, 0), pltpu.roll(z, 1, 0))
        z2 = jnp.where(t < 2, pltpu.roll(e, tm - 6, 0), pltpu.roll(z, 2, 0))
        ztail_ref[...] = z
    cw = cw_ref[...]
    cv = z2 * cw[0:1] + z1 * cw[1:2] + z * cw[2:3]
    c_ref[...] = (b_gate * cv).astype(c_ref.dtype)
    for gi in reversed(range(n_g)):
        off = 3 * D_C + 3 * gi * D_DG
        res = _dot(h, w_ref[:, off:off + 3 * D_DG])
        q = res[:, 0:D_DG] * QK_SCALE_D
        if tiles_per_seq:
            q_ref, k_ref, v_ref, kvt_ref = group_refs[4 * gi:4 * gi + 4]
            dil = D_WINDOWS[gi][1]
            _store_streams(q_ref, q, dil, stream_scr)
            _store_streams(k_ref, res[:, D_DG:2 * D_DG], dil, stream_scr)
            _store_streams(v_ref, res[:, 2 * D_DG:3 * D_DG], dil, stream_scr)
            kvt_ref[...] = res[tm - tails[gi]:tm, D_DG:3 * D_DG].T
        else:
            q_ref, kvt_ref = group_refs[2 * gi:2 * gi + 2]
            q_ref[...] = q
            kvt_ref[...] = res[:, D_DG:3 * D_DG].T


def _proj_cd_prompt(x, g, w, cw, n_batch, seq, tm):
    m = x.shape[0]
    tps = seq // tm
    row = lambda n: pl.BlockSpec((tm, n), lambda i: (i, 0))
    hist = jnp.zeros((n_batch * SUBLANES, D_C), F32)
    out_shape = [jax.ShapeDtypeStruct((m, D_C), BF16), jax.ShapeDtypeStruct((m // tm * SUBLANES, D_C), F32)]
    out_specs = [row(D_C), pl.BlockSpec((SUBLANES, D_C), lambda i: (i, 0))]
    tails = []
    for win, dil in D_WINDOWS:
        keep = min(win, seq)
        tail = min(keep, tm)
        first_kept = tps - keep // tail
        tails.append(tail)
        out_shape += [jax.ShapeDtypeStruct((n_batch, seq // dil, dil * D_DG), BF16)] * 3
        out_shape += [jax.ShapeDtypeStruct((n_batch, 2 * D_DG, keep), F32)]
        out_specs += [pl.BlockSpec((None, tm // dil, dil * D_DG), lambda i: (i // tps, i % tps, 0))] * 3
        out_specs += [pl.BlockSpec((None, 2 * D_DG, tail), functools.partial(
            lambda i, fk: (i // tps, 0, jnp.maximum(i % tps - fk, 0)), fk=first_kept))]
    return pl.pallas_call(
        functools.partial(_proj_cd_kernel, tps, tuple(tails)),
        grid=(m // tm,),
        in_specs=[row(D_MODEL), _resident((1, D_MODEL)), _resident(w.shape),
                  _resident(cw.shape), pl.BlockSpec((SUBLANES, D_C), lambda i: (i // tps, 0))],
        out_specs=out_specs,
        out_shape=out_shape,
        scratch_shapes=[pltpu.VMEM((SUBLANES, D_C), F32), pltpu.VMEM((D_DG // LANES, tm, LANES), F32)],
        compiler_params=_params("arbitrary"),
    )(x, g, w, cw, hist)


def _proj_cd_sample(x, g, w, cw, hist_rows):
    m = x.shape[0]
    full = lambda r, n: pl.BlockSpec((r, n), lambda i: (0, 0))
    out_shape = [jax.ShapeDtypeStruct((m, D_C), F32), jax.ShapeDtypeStruct((m, D_C), F32)]
    out_specs = [full(m, D_C), full(m, D_C)]
    for _ in D_WINDOWS:
        out_shape += [jax.ShapeDtypeStruct((m, D_DG), F32), jax.ShapeDtypeStruct((2 * D_DG, m), F32)]
        out_specs += [full(m, D_DG), full(2 * D_DG, m)]
    return pl.pallas_call(
        functools.partial(_proj_cd_kernel, 0, ()),
        grid=(1,),
        in_specs=[full(m, D_MODEL), _resident((1, D_MODEL)), _resident(w.shape),
                  _resident(cw.shape), full(m, D_C)],
        out_specs=out_specs,
        out_shape=out_shape,
        scratch_shapes=[pltpu.VMEM((SUBLANES, D_C), F32), pltpu.VMEM((D_DG // LANES, SUBLANES, LANES), F32)],
        compiler_params=_params("arbitrary"),
    )(x, g, w, cw, hist_rows)


def _dilated_prompt_kernel(n_streams, q_ref, kc_ref, kp_ref, vc_ref, vp_ref, o_ref, lse_ref):
    bl = N_BACK
    n_qblk = q_ref.shape[0] // bl
    first_blk = pl.program_id(2) * n_qblk
    shape = (bl, 2 * bl)
    row = lax.broadcasted_iota(jnp.int32, shape, 0)
    col = lax.broadcasted_iota(jnp.int32, shape, 1)
    dist = row - col + bl
    band = (dist >= 0) & (dist <= N_BACK)
    hw = 2 * HD_D
    lane = lax.broadcasted_iota(jnp.int32, (bl, hw), 1)
    for r in range(n_streams):
        for qb in range(n_qblk):
            rows = slice(qb * bl, (qb + 1) * bl)
            prev_rows = slice((qb - 1) * bl, qb * bl)
            valid = (band & ((first_blk * bl + col - bl) >= 0)) if qb == 0 else band
            pairs = [slice(r * D_DG + p * hw, r * D_DG + (p + 1) * hw) for p in range(H_DG // 2)]
            scores = []
            for sl in pairs:
                qq = _split_heads_stack(q_ref[rows, sl], HD_D)
                k_prev = kp_ref[:, sl] if qb == 0 else kc_ref[prev_rows, sl]
                kcat = jnp.concatenate([k_prev, kc_ref[rows, sl]], axis=0)
                scores.append([_dot_nt(qq[e * bl:(e + 1) * bl], kcat) for e in range(2)])
            probs = []
            for pair_scores in scores:
                stats = []
                for s in pair_scores:
                    s = jnp.where(valid, s, -jnp.inf)
                    m = jnp.max(s, axis=-1, keepdims=True)
                    pr = jnp.exp2(s - m)
                    l = jnp.sum(pr, axis=-1, keepdims=True)
                    stats.append((pr.astype(BF16), l, (m + jnp.log2(l)) * LN_2))
                probs.append(stats)
            for sl, stats in zip(pairs, probs):
                v_prev = vp_ref[:, sl] if qb == 0 else vc_ref[prev_rows, sl]
                vcat = jnp.concatenate([v_prev, vc_ref[rows, sl]], axis=0)
                outs = [_dot(pr, vcat) / l for pr, l, _ in stats]
                o_ref[rows, sl] = jnp.where(lane < HD_D, outs[0], outs[1]).astype(o_ref.dtype)
                lse_ref[rows, sl] = jnp.where(lane < HD_D, stats[0][2], stats[1][2])


def _dilated_prompt(q, k, v, dil, blocks_per_step):
    n_batch, length, _ = q.shape
    bl = N_BACK
    n_qblk = min(blocks_per_step, length // bl)
    n_streams = min(blocks_per_step // n_qblk, dil)
    rows, width = n_qblk * bl, n_streams * D_DG
    cur = pl.BlockSpec((None, rows, width), lambda b, r, j: (b, j, r))
    prev = pl.BlockSpec((None, bl, width), lambda b, r, j: (b, jnp.maximum(j * n_qblk - 1, 0), r))
    return pl.pallas_call(
        functools.partial(_dilated_prompt_kernel, n_streams),
        grid=(n_batch, dil // n_streams, length // rows),
        in_specs=[cur, cur, prev, cur, prev],
        out_specs=[cur, cur],
        out_shape=[jax.ShapeDtypeStruct(q.shape, BF16), jax.ShapeDtypeStruct(q.shape, F32)],
        compiler_params=_params("arbitrary", "arbitrary", "arbitrary"),
    )(q, k, k, v, v)


def _dilated_sample_kernel(dil, q_ref, cache_ref, kvnew_ref, newc_ref, o_ref, lse_ref):
    n_seq = q_ref.shape[0]
    for s in range(n_seq):
        _dilated_sample_one(dil, pl.program_id(0) * n_seq + s, q_ref.at[s], cache_ref.at[s], kvnew_ref,
                            newc_ref.at[s], o_ref.at[s], lse_ref.at[s])


def _dilated_sample_one(dil, seq, q_ref, cache_ref, kvnew_ref, newc_ref, o_ref, lse_ref):
    buf_len = cache_ref.shape[1]
    t_new = q_ref.shape[0]
    hw = 2 * HD_D
    n_blk = buf_len // LANES

    seq_in_block = seq % (LANES // t_new)
    new_tile = pltpu.roll(kvnew_ref[...], (LANES - seq_in_block * t_new) % LANES, 1)

    lane = lax.broadcasted_iota(jnp.int32, (cache_ref.shape[0], LANES), 1)
    cur = pltpu.roll(cache_ref[:, 0:LANES], LANES - t_new, 1)
    for c in range(n_blk):
        following = cache_ref[:, (c + 1) * LANES:(c + 2) * LANES] if c + 1 < n_blk else new_tile
        nxt = pltpu.roll(following, LANES - t_new, 1)
        newc_ref[:, c * LANES:(c + 1) * LANES] = jnp.where(lane < LANES - t_new, cur, nxt)
        cur = nxt

    def iotas(n_keys):
        shape = (2 * t_new, n_keys)
        row = lax.broadcasted_iota(jnp.int32, shape, 0)
        return row & (t_new - 1), lax.broadcasted_iota(jnp.int32, shape, 1)

    qi, col = iotas(buf_len)
    d = buf_len + qi - col
    valid_buf = ((d & (dil - 1)) == 0) & (d <= N_BACK * dil)
    qi, col = iotas(LANES)
    d = qi - col
    valid_new = (col < t_new) & (d >= 0) & ((d & (dil - 1)) == 0)

    q = q_ref[...].astype(BF16)
    lane = lax.broadcasted_iota(jnp.int32, (t_new, hw), 1)
    for p in range(H_DG // 2):
        k_rows, v_rows = slice(p * hw, (p + 1) * hw), slice(D_DG + p * hw, D_DG + (p + 1) * hw)
        qq = _split_heads_stack(q[:, k_rows], HD_D)
        s_buf = jnp.where(valid_buf, _dot(qq, cache_ref[k_rows, :].astype(BF16)), NEG_BIG)
        s_new = jnp.where(valid_new, _dot(qq, new_tile[k_rows].astype(BF16)), NEG_BIG)
        m = jnp.maximum(jnp.max(s_buf, axis=-1, keepdims=True), jnp.max(s_new, axis=-1, keepdims=True))
        p_buf = jnp.where(valid_buf, jnp.exp2(s_buf - m), 0.0)
        p_new = jnp.where(valid_new, jnp.exp2(s_new - m), 0.0)
        l = jnp.sum(p_buf, axis=-1, keepdims=True) + jnp.sum(p_new, axis=-1, keepdims=True)
        acc = (_dot_nt(p_buf.astype(BF16), cache_ref[v_rows, :].astype(BF16))
               + _dot_nt(p_new.astype(BF16), new_tile[v_rows].astype(BF16)))
        o = acc / l
        lse = (m + jnp.log2(l)) * LN_2
        o_ref[:, k_rows] = jnp.where(lane < HD_D, o[0:t_new], o[t_new:2 * t_new])
        lse_ref[:, k_rows] = jnp.where(lane < HD_D, lse[0:t_new], lse[t_new:2 * t_new])


def _dilated_sample(q, cache_t, kvnew_t, dil, n_batch, t_new, seqs_per_step):
    buf_len = cache_t.shape[2]
    steps_per_lane_block = LANES // t_new // seqs_per_step
    small = lambda n: pl.BlockSpec((seqs_per_step, t_new, n), lambda b: (b, 0, 0))
    whole = pl.BlockSpec((seqs_per_step, 2 * D_DG, buf_len), lambda b: (b, 0, 0))
    newc, o, lse = pl.pallas_call(
        functools.partial(_dilated_sample_kernel, dil),
        grid=(n_batch // seqs_per_step,),
        in_specs=[small(D_DG), whole, pl.BlockSpec((2 * D_DG, LANES), lambda b: (0, b // steps_per_lane_block))],
        out_specs=[whole, small(D_DG), small(D_DG)],
        out_shape=[jax.ShapeDtypeStruct(cache_t.shape, F32),
                   jax.ShapeDtypeStruct((n_batch, t_new, D_DG), F32),
                   jax.ShapeDtypeStruct((n_batch, t_new, D_DG), F32)],
        compiler_params=_params("arbitrary"),
    )(q.reshape(n_batch, t_new, D_DG), cache_t, kvnew_t)
    return newc, o.reshape(n_batch * t_new, D_DG), lse.reshape(n_batch * t_new, D_DG)


def kernel(x_prompt, x_sample, state_a_pool, cache_b_kv, state_c_conv, cache_d0_kv, cache_d1_kv, cache_d2_kv, page_table, norm_mix_g, norm_mlp_g, norm_out_g, w_in_ab, w_out_ab, a_mix, a_scale, b_lam, b_subln_g, w_in_cd, w_out_cd, c_conv_w, w_up, w_down):
    bp, seq, _ = x_prompt.shape
    bs, t_new, _ = x_sample.shape
    n_pages = page_table.shape[1]
    past_len = n_pages * cache_b_kv.shape[2]
    mp, ms = bp * seq, bs * t_new
    tm_p = ROW_TILE
    xp = x_prompt.reshape(mp, D_MODEL)
    xs = x_sample.reshape(ms, D_MODEL)
    g2 = lambda v: v.reshape(1, -1)
    d_caches = (cache_d0_kv, cache_d1_kv, cache_d2_kv)

    lam_init = 0.8 - 0.6 * math.exp(-0.3 * 0)
    w_in = w_in_ab[0].astype(BF16)
    w_out = w_out_ab[0].astype(BF16)
    wu, wd = w_up[0].astype(BF16), w_down[0].astype(BF16)
    amix = a_mix[0].astype(BF16)
    gm, gl = g2(norm_mix_g[0]), g2(norm_mlp_g[0])
    ascale, sg = g2(a_scale[0]), g2(b_subln_g[0])

    u_p, q_p, kv_p, k_p, vt_p = _proj_ab(xp, gm, w_in, tm_p, True, ATTN_KEY_BLOCK)
    hist_p = jnp.zeros((bp, POOL_HIST + 1, D_A), F32)
    a_p = _pool_mix(u_p, hist_p, amix, ascale, 0, bp, seq)
    ob_p = _diffattn_prompt(q_p, k_p, vt_p, b_lam[0], sg, lam_init, bp, seq, ATTN_QUERY_CHUNK)

    u_s, q_s, kv_s, _, _ = _proj_ab(xs, gm, w_in, ms, False, ATTN_KEY_BLOCK)
    hist_s = jnp.pad(state_a_pool[0], ((0, 0), (1, 0), (0, 0)))
    a_s = _pool_mix(u_s, hist_s, amix, ascale, past_len, bs, t_new)
    pool_pages = cache_b_kv[0].reshape(cache_b_kv.shape[1], PAGE_SIZE * 2 * H_B, 2 * HD_B)
    xp = _post_mixer(xp, a_p, (ob_p,), (), (), w_out, gl, wu, wd, None, tm_p)
    ob_s = _diffattn_decode(page_table, pool_pages, q_s, kv_s, b_lam[0], sg, lam_init, bs, t_new,
                            min(PAGES_PER_STEP, n_pages))
    xs = _post_mixer(xs, a_s, (ob_s,), (), (), w_out, gl, wu, wd, None, ms)

    u_p3 = u_p.reshape(bp, seq, D_A)
    new_pool_p = u_p3[:, seq - POOL_HIST:][None]
    new_pool_s = jnp.concatenate([state_a_pool[0], u_s.reshape(bs, t_new, D_A)], axis=1)[:, -POOL_HIST:][None]
    new_bkv_p = kv_p.reshape(1, bp, seq, 2, H_B, 2 * HD_B)
    new_bkv_s = kv_s.reshape(1, bs, t_new, 2, H_B, 2 * HD_B)

    w_in = w_in_cd[0].astype(BF16)
    w_out = w_out_cd[0].astype(BF16)
    wu, wd = w_up[1].astype(BF16), w_down[1].astype(BF16)
    gm, gl, gf = g2(norm_mix_g[1]), g2(norm_mlp_g[1]), g2(norm_out_g)
    cw = c_conv_w[0]

    def from_position_minor(a):
        a = a.reshape(a.shape[0], 2, H_DG, HD_D, a.shape[2])
        return jnp.transpose(a, (0, 4, 1, 2, 3))[None]

    outs = _proj_cd_prompt(xp, gm, w_in, cw, bp, seq, tm_p)
    c_p, ztail_p = outs[0], outs[1]
    o_parts, lse_parts, new_d_p = [], [], []
    for gi, (win, dil) in enumerate(D_WINDOWS):
        q_g, k_g, v_g, kvt_g = outs[2 + 4 * gi:6 + 4 * gi]
        o_g, lse_g = _dilated_prompt(q_g, k_g, v_g, dil, DILATED_BLOCKS_PER_STEP)
        if dil == 1:
            o_g, lse_g = o_g.reshape(mp, D_DG), lse_g.reshape(mp, D_DG)
        o_parts.append(o_g)
        lse_parts.append(lse_g)
        new_d_p.append(from_position_minor(kvt_g))
    dils = tuple(dil for _, dil in D_WINDOWS)
    y_p = _post_mixer(xp, c_p, o_parts, lse_parts, dils, w_out, gl, wu, wd, gf, tm_p)
    tiles_per_seq = seq // tm_p
    new_conv_p = ztail_p.reshape(bp, tiles_per_seq, SUBLANES, D_C)[:, -1, SUBLANES - (CONV_W - 1):][None]

    hist_rows = jnp.pad(state_c_conv[0], ((0, 0), (t_new - (CONV_W - 1), 0), (0, 0))).reshape(ms, D_C)
    outs = _proj_cd_sample(xs, gm, w_in, cw, hist_rows)
    c_s, z_s = outs[0], outs[1]
    o_parts, lse_parts, new_d_s = [], [], []
    for gi, (win, dil) in enumerate(D_WINDOWS):
        q_g, kvt_g = outs[2 + 2 * gi:4 + 2 * gi]
        cache = d_caches[gi][0]
        buf_len = cache.shape[1]
        cache_t = jnp.transpose(cache, (0, 2, 3, 4, 1)).reshape(bs, 2 * D_DG, buf_len)
        seqs_per_step = max(1, min(SUBLANES, D_WINDOWS[-1][0] // buf_len))
        newc, o_g, lse_g = _dilated_sample(q_g, cache_t, kvt_g, dil, bs, t_new, seqs_per_step)
        o_parts.append(o_g)
        lse_parts.append(lse_g)
        new_d_s.append(from_position_minor(newc))
    y_s = _post_mixer(xs, c_s, o_parts, lse_parts, (1,) * len(D_WINDOWS), w_out, gl, wu, wd, gf, ms)
    new_conv_s = z_s.reshape(bs, t_new, D_C)[:, t_new - (CONV_W - 1):][None]

    return (y_p.reshape(bp, seq, D_MODEL), y_s.reshape(bs, t_new, D_MODEL),
            new_pool_p, new_pool_s, new_bkv_p, new_bkv_s, new_conv_p, new_conv_s,
            new_d_p[0], new_d_s[0], new_d_p[1], new_d_s[1], new_d_p[2], new_d_s[2])
```

```python
import functools
import math

import jax
import jax.numpy as jnp
from jax import lax
from jax.experimental import pallas as pl
from jax.experimental.pallas import tpu as pltpu

F32 = jnp.float32
BF16 = jnp.bfloat16

D_MODEL = 1024
RMS_EPS = 1e-6
D_A = 512
POOL_WINDOWS = (2, 4, 8, 16)
POOL_GROUP = 128
POOL_HIST = 15
H_B = 4
HD_B = 64
D_B = 512
D_C = 512
CONV_W = 3
D_WINDOWS = ((128, 1), (512, 4), (2048, 16))
N_BACK = 128
H_DG = 8
HD_D = 64
D_DG = 512
D_FF = 4096
PAGE_SIZE = 128

LANES = 128
SUBLANES = 8
VMEM_LIMIT_BYTES = 56 * 1024 * 1024
NEG_BIG = -1e30

ROW_TILE = 512
FF_CHUNK = 1024
ATTN_QUERY_CHUNK = 512
ATTN_KEY_BLOCK = 512
PAGES_PER_STEP = 32
DECODE_KEY_SPLITS = 16
DILATED_BLOCKS_PER_STEP = 8

LOG2_E = math.log2(math.e)
LN_2 = math.log(2.0)
QK_SCALE_B = HD_B ** -0.5 * LOG2_E
QK_SCALE_D = HD_D ** -0.5 * LOG2_E


def _params(*sem):
    return pltpu.CompilerParams(dimension_semantics=sem, vmem_limit_bytes=VMEM_LIMIT_BYTES)


def _resident(shape):
    nd = len(shape)
    return pl.BlockSpec(shape, lambda *_: (0,) * nd, pipeline_mode=pl.Buffered(1))


def _rmsnorm(x, g):
    return x * lax.rsqrt(jnp.mean(x * x, axis=-1, keepdims=True) + RMS_EPS) * g


def _dot(a, b):
    return jnp.dot(a, b, preferred_element_type=F32)


def _dot_nt(a, b):
    return lax.dot_general(a, b, (((1,), (1,)), ((), ())), preferred_element_type=F32)


def _split_heads_stack(q, half):
    lane = lax.broadcasted_iota(jnp.int32, q.shape, 1)
    zero = jnp.zeros_like(q)
    return jnp.concatenate([jnp.where(lane < half, q, zero), jnp.where(lane >= half, q, zero)], axis=0)


def _lane_blocks(n):
    return [slice(c * LANES, (c + 1) * LANES) for c in range(n // LANES)]


def _store_streams(ref, val, dil, scr):
    if dil == 1:
        ref[...] = val.astype(ref.dtype)
        return
    width = val.shape[1]
    for c, sl in enumerate(_lane_blocks(width)):
        scr[c] = val[:, sl]
    for r in range(dil):
        for c, sl in enumerate(_lane_blocks(width)):
            ref[:, r * width + sl.start:r * width + sl.stop] = (
                scr[c, pl.ds(r, ref.shape[0], stride=dil), :].astype(ref.dtype))


def _rows_from_streams(ref, dil, scr):
    width = ref.shape[1] // dil
    if dil == 1:
        return [ref[:, sl].astype(F32) for sl in _lane_blocks(width)]
    for r in range(dil):
        for c, sl in enumerate(_lane_blocks(width)):
            scr[c, pl.ds(r, ref.shape[0], stride=dil), :] = ref[:, r * width + sl.start:r * width + sl.stop].astype(F32)
    return [scr[c] for c in range(width // LANES)]


def _diff_lambda(blam, lam_init):
    a = jnp.sum(blam[0:1] * blam[1:2], axis=-1, keepdims=True)
    b = jnp.sum(blam[2:3] * blam[3:4], axis=-1, keepdims=True)
    return jnp.exp(a) - jnp.exp(b) + lam_init


def _proj_ab_kernel(prompt, x_ref, g_ref, w_ref, u_ref, q_ref, kv_ref, k_ref, v_ref):
    h = _rmsnorm(x_ref[...], g_ref[...]).astype(BF16)
    kv = _dot(h, w_ref[:, D_A + D_B:D_A + 3 * D_B])
    u_ref[...] = _dot(h, w_ref[:, 0:D_A])
    q_ref[...] = (_dot(h, w_ref[:, D_A:D_A + D_B]) * QK_SCALE_B).astype(q_ref.dtype)
    k_ref[...] = kv[:, :D_B].astype(BF16)
    if not prompt:
        kv_ref[...] = kv
        v_ref[...] = kv[:, D_B:].astype(BF16)
        return
    n_blk = kv.shape[1] // LANES
    for c, sl in enumerate(_lane_blocks(kv.shape[1])):
        kv_ref[pl.ds(c, kv.shape[0], stride=n_blk), :] = kv[:, sl]
    vt = kv[:, D_B:].T.astype(BF16)
    tk = v_ref.shape[2]
    for j in range(v_ref.shape[0]):
        v_ref[j] = vt[:, j * tk:(j + 1) * tk]


def _proj_ab(x, g, w, tm, prompt, tk):
    m = x.shape[0]
    n_in = w.shape[1]
    row = lambda n: pl.BlockSpec((tm, n), lambda i: (i, 0))
    n_blk = 2 * D_B // LANES
    if prompt:
        kv_shape, kv_spec = (m * n_blk, LANES), pl.BlockSpec((tm * n_blk, LANES), lambda i: (i, 0))
        v_shape, v_spec = (m // tk, D_B, tk), pl.BlockSpec((tm // tk, D_B, tk), lambda i: (i, 0, 0))
    else:
        kv_shape, kv_spec = (m, 2 * D_B), row(2 * D_B)
        v_shape, v_spec = (m, D_B), row(D_B)
    return pl.pallas_call(
        functools.partial(_proj_ab_kernel, prompt),
        grid=(m // tm,),
        in_specs=[row(D_MODEL), _resident((1, D_MODEL)), _resident((D_MODEL, n_in))],
        out_specs=[row(D_A), row(D_B), kv_spec, row(D_B), v_spec],
        out_shape=[jax.ShapeDtypeStruct((m, D_A), F32), jax.ShapeDtypeStruct((m, D_B), BF16 if prompt else F32),
                   jax.ShapeDtypeStruct(kv_shape, F32), jax.ShapeDtypeStruct((m, D_B), BF16),
                   jax.ShapeDtypeStruct(v_shape, BF16)],
        compiler_params=_params("arbitrary"),
    )(x, g, w)


def _pool_kernel(pos0, tchunk, u_ref, hist_ref, amix_ref, ascale_ref, o_ref, ext_ref):
    t_len = u_ref.shape[0]
    hpad = hist_ref.shape[0]
    ext_ref[0:hpad, :] = hist_ref[...]
    ext_ref[hpad:hpad + t_len, :] = u_ref[...]
    for t0 in range(0, t_len, tchunk):
        pos = pos0 + t0 + lax.broadcasted_iota(jnp.int32, (tchunk, 1), 0)
        for gi, w in enumerate(POOL_WINDOWS):
            sl = slice(gi * POOL_GROUP, (gi + 1) * POOL_GROUP)
            u = u_ref[t0:t0 + tchunk, sl]
            acc = u
            for j in range(1, w):
                acc = acc + ext_ref[hpad + t0 - j:hpad + t0 - j + tchunk, sl]
            cnt = jnp.minimum(pos + 1, w).astype(F32)
            p = (acc / cnt - u).astype(BF16)
            y = _dot(p, amix_ref[gi]) * ascale_ref[:, sl]
            o_ref[t0:t0 + tchunk, sl] = y.astype(o_ref.dtype)


def _pool_mix(u, hist16, amix, ascale, pos0, n_seq, t_len):
    hpad = hist16.shape[1]
    tchunk = min(t_len, 256)
    out = pl.pallas_call(
        functools.partial(_pool_kernel, pos0, tchunk),
        grid=(n_seq,),
        in_specs=[pl.BlockSpec((None, t_len, D_A), lambda s: (s, 0, 0)),
                  pl.BlockSpec((None, hpad, D_A), lambda s: (s, 0, 0)),
                  _resident(amix.shape), _resident((1, D_A))],
        out_specs=pl.BlockSpec((None, t_len, D_A), lambda s: (s, 0, 0)),
        out_shape=jax.ShapeDtypeStruct((n_seq, t_len, D_A), BF16 if t_len % 16 == 0 else F32),
        scratch_shapes=[pltpu.VMEM((hpad + t_len, D_A), F32)],
        compiler_params=_params("arbitrary"),
    )(u.reshape(n_seq, t_len, D_A), hist16, amix, ascale)
    return out.reshape(n_seq * t_len, D_A)


def _diff_finalize(acc, l, lam, sg, lam_init, t):
    o = acc[0:t] / l[0:t] - lam * (acc[t:2 * t] / l[t:2 * t])
    o = o * lax.rsqrt(jnp.mean(o * o, axis=-1, keepdims=True) + RMS_EPS) * sg
    return o * (1.0 - lam_init)


def _diffattn_prompt_kernel(lam_init, tc, q_ref, k_ref, vt_ref, blam_ref, sgt_ref, o_ref, m_scr, l_scr, acc_scr):
    seq = q_ref.shape[0]
    tk = vt_ref.shape[2]
    hw = 2 * HD_B
    heads = [slice(h * hw, (h + 1) * hw) for h in range(H_B)]
    lam = _diff_lambda(blam_ref[...], lam_init)
    sgt = sgt_ref[...]
    shape = (tk, 2 * tc)
    key = lax.broadcasted_iota(jnp.int32, shape, 0)
    col = lax.broadcasted_iota(jnp.int32, shape, 1)
    qcol = jnp.where(col >= tc, col - tc, col)

    def scores(j, h, qq):
        off = pl.multiple_of(j * tk, tk)
        return _dot_nt(k_ref[pl.ds(off, tk), heads[h]], qq)

    def chunk(c, carry):
        q0 = pl.multiple_of(c * tc, tc)
        jd = q0 // tk
        qqs = [_split_heads_stack(q_ref[pl.ds(q0, tc), sl], HD_B) for sl in heads]

        causal = (jd * tk + key) <= (q0 + qcol)
        ss = [scores(jd, h, qqs[h]) for h in range(H_B)]
        ps = []
        for h in range(H_B):
            s = jnp.where(causal, ss[h], -jnp.inf)
            m = jnp.max(s, axis=0, keepdims=True)
            p = jnp.exp2(s - m)
            m_scr[h] = m
            l_scr[h] = jnp.sum(p, axis=0, keepdims=True)
            ps.append(p.astype(BF16))
        for h, sl in enumerate(heads):
            acc_scr[h] = _dot(vt_ref[jd, sl, :], ps[h])

        def body(j, inner):
            ss = [scores(j, h, qqs[h]) for h in range(H_B)]
            ps = []
            for h in range(H_B):
                m_prev = m_scr[h]
                m_new = jnp.maximum(m_prev, jnp.max(ss[h], axis=0, keepdims=True))
                alpha = jnp.exp2(m_prev - m_new)
                p = jnp.exp2(ss[h] - m_new)
                l_scr[h] = alpha * l_scr[h] + jnp.sum(p, axis=0, keepdims=True)
                m_scr[h] = m_new
                ps.append((alpha, p.astype(BF16)))
            for h, sl in enumerate(heads):
                alpha, p = ps[h]
                acc_scr[h] = alpha * acc_scr[h] + _dot(vt_ref[j, sl, :], p)
            return inner

        lax.fori_loop(0, jd, body, 0)
        for h, sl in enumerate(heads):
            o = acc_scr[h] / l_scr[h]
            o = o[:, 0:tc] - lam * o[:, tc:2 * tc]
            o = o * lax.rsqrt(jnp.mean(o * o, axis=0, keepdims=True) + RMS_EPS) * sgt * (1.0 - lam_init)
            o_ref[pl.ds(q0, tc), sl] = o.T.astype(o_ref.dtype)
        return carry

    lax.fori_loop(0, seq // tc, chunk, 0)


def _diffattn_prompt(q, k, vt, blam, sg, lam_init, n_batch, seq, tc):
    hw = 2 * HD_B
    tk = vt.shape[2]
    whole = pl.BlockSpec((seq, D_B), lambda b: (b, 0))
    return pl.pallas_call(
        functools.partial(_diffattn_prompt_kernel, lam_init, tc),
        grid=(n_batch,),
        in_specs=[whole, whole, pl.BlockSpec((seq // tk, D_B, tk), lambda b: (b, 0, 0)),
                  _resident(blam.shape), _resident((hw, 1))],
        out_specs=whole,
        out_shape=jax.ShapeDtypeStruct((n_batch * seq, D_B), BF16),
        scratch_shapes=[pltpu.VMEM((H_B, 1, 2 * tc), F32), pltpu.VMEM((H_B, 1, 2 * tc), F32),
                        pltpu.VMEM((H_B, hw, 2 * tc), F32)],
        compiler_params=_params("arbitrary"),
    )(q, k, vt, blam, sg.reshape(hw, 1))


def _online_softmax(rows, scores, valid, m_scr, l_scr):
    steps = []
    for r, s in zip(rows, scores):
        m_prev = m_scr[r]
        m_new = jnp.maximum(m_prev, jnp.max(s, axis=-1, keepdims=True))
        alpha = jnp.exp2(m_prev - m_new)
        p = jnp.exp2(s - m_new)
        if valid is not None:
            p = jnp.where(valid, p, 0.0)
        l_scr[r] = alpha * l_scr[r] + jnp.sum(p, axis=-1, keepdims=True)
        m_scr[r] = m_new
        steps.append((alpha, p.astype(BF16)))
    return steps


def _online_accumulate(rows, steps, values, acc_scr):
    for r, v, (alpha, p) in zip(rows, values, steps):
        acc_scr[r] = alpha * acc_scr[r] + _dot(p, v)


def _online_update(rows, scores, values, valid, m_scr, l_scr, acc_scr):
    _online_accumulate(rows, _online_softmax(rows, scores, valid, m_scr, l_scr), values, acc_scr)


def _init_online(m_scr, l_scr, acc_scr):
    m_scr[...] = jnp.full(m_scr.shape, NEG_BIG, F32)
    l_scr[...] = jnp.zeros(l_scr.shape, F32)
    acc_scr[...] = jnp.zeros(acc_scr.shape, F32)


def _decode_queries(q_ref):
    t_new = q_ref.shape[0]
    hw = 2 * HD_B
    q = q_ref[...].astype(BF16)
    qs = [_split_heads_stack(q[:, h * hw:(h + 1) * hw], HD_B) for h in range(H_B)]
    return qs, [slice(2 * t_new * h, 2 * t_new * (h + 1)) for h in range(H_B)]


def _decode_page_heads(page_refs, first_row):
    return jnp.concatenate([r[pl.ds(first_row, PAGE_SIZE, stride=2 * H_B), :] for r in page_refs],
                           axis=0).astype(BF16)


def _decode_finish(lam_init, qs, head_rows, kvnew_ref, blam_ref, sg_ref, o_ref, m_scr, l_scr, acc_scr, pad_scr):
    t_new = kvnew_ref.shape[0]
    hw = 2 * HD_B
    pad_scr[...] = jnp.zeros(pad_scr.shape, F32)
    pad_scr[0:t_new, :] = kvnew_ref[...]
    shape = (2 * t_new, pad_scr.shape[0])
    row = lax.broadcasted_iota(jnp.int32, shape, 0)
    col = lax.broadcasted_iota(jnp.int32, shape, 1)
    valid = col <= jnp.where(row >= t_new, row - t_new, row)
    scores = [jnp.where(valid, _dot_nt(qs[h], pad_scr[:, h * hw:(h + 1) * hw].astype(BF16)), NEG_BIG)
              for h in range(H_B)]
    values = [pad_scr[:, D_B + h * hw:D_B + (h + 1) * hw].astype(BF16) for h in range(H_B)]
    _online_update(head_rows, scores, values, valid, m_scr, l_scr, acc_scr)
    lam = _diff_lambda(blam_ref[...], lam_init)
    for h, rows in enumerate(head_rows):
        o = _diff_finalize(acc_scr[rows], l_scr[rows], lam, sg_ref[...], lam_init, t_new)
        o_ref[:, h * hw:(h + 1) * hw] = o.astype(o_ref.dtype)


def _diffattn_decode_kernel(pages_per_step, lam_init, pt_ref, q_ref, kvnew_ref, blam_ref, sg_ref, *rest):
    page_refs = rest[:pages_per_step]
    o_ref, m_scr, l_scr, acc_scr, pad_scr = rest[pages_per_step:]
    j = pl.program_id(1)
    qs, head_rows = _decode_queries(q_ref)

    @pl.when(j == 0)
    def _():
        _init_online(m_scr, l_scr, acc_scr)

    group = max(1, pages_per_step // DECODE_KEY_SPLITS)
    groups = [page_refs[g:g + group] for g in range(0, pages_per_step, group)]
    scores = [[_dot_nt(qs[h], _decode_page_heads(refs, h)) for h in range(H_B)] for refs in groups]
    partials = []
    for group_scores in scores:
        stats = []
        for s in group_scores:
            m = jnp.max(s, axis=-1, keepdims=True)
            p = jnp.exp2(s - m)
            stats.append((m, jnp.sum(p, axis=-1, keepdims=True), p.astype(BF16)))
        partials.append(stats)
    pvs = [[_dot(stats[h][2], _decode_page_heads(refs, H_B + h)) for h in range(H_B)]
           for refs, stats in zip(groups, partials)]
    for h, rows in enumerate(head_rows):
        m_prev = m_scr[rows]
        m_new = functools.reduce(jnp.maximum, [stats[h][0] for stats in partials], m_prev)
        alpha = jnp.exp2(m_prev - m_new)
        l_new, acc_new = alpha * l_scr[rows], alpha * acc_scr[rows]
        for stats, pv in zip(partials, pvs):
            w = jnp.exp2(stats[h][0] - m_new)
            l_new = l_new + w * stats[h][1]
            acc_new = acc_new + w * pv[h]
        m_scr[rows], l_scr[rows], acc_scr[rows] = m_new, l_new, acc_new

    @pl.when(j == pl.num_programs(1) - 1)
    def _():
        _decode_finish(lam_init, qs, head_rows, kvnew_ref, blam_ref, sg_ref, o_ref, m_scr, l_scr, acc_scr, pad_scr)


def _diffattn_decode(page_table, cache, q, kvnew, blam, sg, lam_init, n_batch, t_new, pages_per_step):
    n_pages = page_table.shape[1]
    hw = 2 * HD_B
    page_specs = [
        pl.BlockSpec((None, PAGE_SIZE * 2 * H_B, hw), functools.partial(
            lambda b, j, pt, t: (pt[b, j * pages_per_step + t], 0, 0), t=t))
        for t in range(pages_per_step)]
    grid_spec = pltpu.PrefetchScalarGridSpec(
        num_scalar_prefetch=1,
        grid=(n_batch, n_pages // pages_per_step),
        in_specs=[pl.BlockSpec((None, t_new, D_B), lambda b, j, pt: (b, 0, 0)),
                  pl.BlockSpec((None, t_new, 2 * D_B), lambda b, j, pt: (b, 0, 0)),
                  pl.BlockSpec(blam.shape, lambda b, j, pt: (0, 0)),
                  pl.BlockSpec((1, hw), lambda b, j, pt: (0, 0))] + page_specs,
        out_specs=pl.BlockSpec((None, t_new, D_B), lambda b, j, pt: (b, 0, 0)),
        scratch_shapes=[pltpu.VMEM((2 * t_new * H_B, 1), F32), pltpu.VMEM((2 * t_new * H_B, 1), F32),
                        pltpu.VMEM((2 * t_new * H_B, hw), F32), pltpu.VMEM((PAGE_SIZE, 2 * D_B), F32)])
    out = pl.pallas_call(
        functools.partial(_diffattn_decode_kernel, pages_per_step, lam_init),
        grid_spec=grid_spec,
        out_shape=jax.ShapeDtypeStruct((n_batch, t_new, D_B), F32),
        compiler_params=_params("arbitrary", "arbitrary"),
    )(page_table, q.reshape(n_batch, t_new, D_B), kvnew.reshape(n_batch, t_new, 2 * D_B), blam, sg,
      *([cache] * pages_per_step))
    return out.reshape(n_batch * t_new, D_B)


def _post_kernel(dils, ff_chunk, has_final, x_ref, first_ref, *rest):
    n_g = max(len(dils), 1)
    o_refs = rest[:n_g]
    lse_refs = rest[n_g:n_g + len(dils)]
    rest = rest[n_g + len(dils):]
    wout_ref, g_ref, wup_ref, wdn_ref = rest[:4]
    rest = rest[4:]
    gf_ref = rest[0] if has_final else None
    o_ref = rest[1] if has_final else rest[0]
    scratch = list(rest[2 if has_final else 1:])
    if not dils:
        second = o_refs[0][...].astype(BF16)
    else:
        scr_of = lambda d: scratch.pop(0) if d > 1 else None
        os_ = [_rows_from_streams(r, d, scr_of(d)) for r, d in zip(o_refs, dils)]
        ls_ = [_rows_from_streams(r, d, scr_of(d)) for r, d in zip(lse_refs, dils)]
        merged = []
        for c in range(len(os_[0])):
            lses = [l[c] for l in ls_]
            mx = functools.reduce(jnp.maximum, lses)
            es = [jnp.exp(l - mx) for l in lses]
            num = functools.reduce(lambda a, b: a + b, [e * o[c] for e, o in zip(es, os_)])
            merged.append((num / functools.reduce(lambda a, b: a + b, es)).astype(BF16))
        second = jnp.concatenate(merged, axis=-1)
    mix = jnp.concatenate([first_ref[...].astype(BF16), second], axis=-1)
    x1 = x_ref[...] + _dot(mix, wout_ref[...])
    h = _rmsnorm(x1, g_ref[...]).astype(BF16)
    acc = x1
    up = _dot(h, wup_ref[:, 0:ff_chunk])
    for c in range(0, D_FF, ff_chunk):
        nxt = _dot(h, wup_ref[:, c + ff_chunk:c + 2 * ff_chunk]) if c + ff_chunk < D_FF else None
        act = jnp.square(jnp.maximum(up, 0.0)).astype(BF16)
        acc = acc + _dot(act, wdn_ref[c:c + ff_chunk, :])
        up = nxt
    if has_final:
        acc = _rmsnorm(acc, gf_ref[...])
    o_ref[...] = acc


def _post_mixer(x, first, o_parts, lse_parts, dils, wout, g, wup, wdn, gf, tm):
    m = x.shape[0]
    row = lambda n: pl.BlockSpec((tm, n), lambda i: (i, 0))

    def part_spec(a, dil):
        if dil == 1:
            return row(a.shape[1])
        tiles_per_seq = a.shape[1] * dil // tm
        return pl.BlockSpec((None, tm // dil, a.shape[2]), lambda i: (i // tiles_per_seq, i % tiles_per_seq, 0))

    has_final = gf is not None
    part_dils = tuple(dils) if dils else (1,)
    args = [x, first, *o_parts, *lse_parts, wout, g, wup, wdn]
    specs = [row(D_MODEL), row(first.shape[1])]
    specs += [part_spec(a, d) for a, d in zip(o_parts, part_dils)] + [part_spec(a, d) for a, d in zip(lse_parts, dils)]
    specs += [_resident(wout.shape), _resident((1, D_MODEL)), _resident(wup.shape), _resident(wdn.shape)]
    if has_final:
        args.append(gf)
        specs.append(_resident((1, D_MODEL)))
    n_scratch = 2 * sum(d > 1 for d in dils)
    return pl.pallas_call(
        functools.partial(_post_kernel, tuple(dils), FF_CHUNK, has_final),
        grid=(m // tm,),
        in_specs=specs,
        out_specs=row(D_MODEL),
        out_shape=jax.ShapeDtypeStruct((m, D_MODEL), F32),
        scratch_shapes=[pltpu.VMEM((D_DG // LANES, tm, LANES), F32)] * n_scratch,
        compiler_params=_params("arbitrary"),
    )(*args)


def _proj_cd_kernel(tiles_per_seq, tails, x_ref, g_ref, w_ref, cw_ref, hist_ref, *rest):
    n_g = len(D_WINDOWS)
    c_ref, ztail_ref = rest[0], rest[1]
    group_refs = rest[2:2 + 4 * n_g] if tiles_per_seq else rest[2:2 + 2 * n_g]
    carry_ref, stream_scr = rest[-2], rest[-1]
    tm = x_ref.shape[0]
    i = pl.program_id(0)
    h = _rmsnorm(x_ref[...], g_ref[...]).astype(BF16)
    gates = _dot(h, w_ref[:, 0:3 * D_C])
    b_gate = gates[:, 0:D_C]
    z = gates[:, D_C:2 * D_C] * gates[:, 2 * D_C:3 * D_C]
    row = lax.broadcasted_iota(jnp.int32, z.shape, 0)
    if tiles_per_seq:
        first = (i % tiles_per_seq) == 0
        prev = jnp.where(first, hist_ref[...], carry_ref[...])
        z1 = jnp.where(row == 0, prev[7:8], pltpu.roll(z, 1, 0))
        z2 = jnp.where(row == 0, prev[6:7], jnp.where(row == 1, prev[7:8], pltpu.roll(z, 2, 0)))
        carry_ref[...] = z[tm - SUBLANES:tm]
        ztail_ref[...] = z[tm - SUBLANES:tm]
    else:
        e = hist_ref[...]
        t = row & (SUBLANES - 1)
        z1 = jnp.where(t == 0, pltpu.roll(e, tm - 7, 0), pltpu.roll(z, 1, 0))
        z2 = jnp.where(t < 2, pltpu.roll(e, tm - 6, 0), pltpu.roll(z, 2, 0))
        ztail_ref[...] = z
    cw = cw_ref[...]
    cv = z2 * cw[0:1] + z1 * cw[1:2] + z * cw[2:3]
    c_ref[...] = (b_gate * cv).astype(c_ref.dtype)
    for gi in reversed(range(n_g)):
        off = 3 * D_C + 3 * gi * D_DG
        res = _dot(h, w_ref[:, off:off + 3 * D_DG])
        q = res[:, 0:D_DG] * QK_SCALE_D
        if tiles_per_seq:
            q_ref, k_ref, v_ref, kvt_ref = group_refs[4 * gi:4 * gi + 4]
            dil = D_WINDOWS[gi][1]
            _store_streams(q_ref, q, dil, stream_scr)
            _store_streams(k_ref, res[:, D_DG:2 * D_DG], dil, stream_scr)
            _store_streams(v_ref, res[:, 2 * D_DG:3 * D_DG], dil, stream_scr)
            kvt_ref[...] = res[tm - tails[gi]:tm, D_DG:3 * D_DG].T
        else:
            q_ref, kvt_ref = group_refs[2 * gi:2 * gi + 2]
            q_ref[...] = q
            kvt_ref[...] = res[:, D_DG:3 * D_DG].T


def _proj_cd_prompt(x, g, w, cw, n_batch, seq, tm):
    m = x.shape[0]
    tps = seq // tm
    row = lambda n: pl.BlockSpec((tm, n), lambda i: (i, 0))
    hist = jnp.zeros((n_batch * SUBLANES, D_C), F32)
    out_shape = [jax.ShapeDtypeStruct((m, D_C), BF16), jax.ShapeDtypeStruct((m // tm * SUBLANES, D_C), F32)]
    out_specs = [row(D_C), pl.BlockSpec((SUBLANES, D_C), lambda i: (i, 0))]
    tails = []
    for win, dil in D_WINDOWS:
        keep = min(win, seq)
        tail = min(keep, tm)
        first_kept = tps - keep // tail
        tails.append(tail)
        out_shape += [jax.ShapeDtypeStruct((n_batch, seq // dil, dil * D_DG), BF16)] * 3
        out_shape += [jax.ShapeDtypeStruct((n_batch, 2 * D_DG, keep), F32)]
        out_specs += [pl.BlockSpec((None, tm // dil, dil * D_DG), lambda i: (i // tps, i % tps, 0))] * 3
        out_specs += [pl.BlockSpec((None, 2 * D_DG, tail), functools.partial(
            lambda i, fk: (i // tps, 0, jnp.maximum(i % tps - fk, 0)), fk=first_kept))]
    return pl.pallas_call(
        functools.partial(_proj_cd_kernel, tps, tuple(tails)),
        grid=(m // tm,),
        in_specs=[row(D_MODEL), _resident((1, D_MODEL)), _resident(w.shape),
                  _resident(cw.shape), pl.BlockSpec((SUBLANES, D_C), lambda i: (i // tps, 0))],
        out_specs=out_specs,
        out_shape=out_shape,
        scratch_shapes=[pltpu.VMEM((SUBLANES, D_C), F32), pltpu.VMEM((D_DG // LANES, tm, LANES), F32)],
        compiler_params=_params("arbitrary"),
    )(x, g, w, cw, hist)


def _proj_cd_sample(x, g, w, cw, hist_rows):
    m = x.shape[0]
    full = lambda r, n: pl.BlockSpec((r, n), lambda i: (0, 0))
    out_shape = [jax.ShapeDtypeStruct((m, D_C), F32), jax.ShapeDtypeStruct((m, D_C), F32)]
    out_specs = [full(m, D_C), full(m, D_C)]
    for _ in D_WINDOWS:
        out_shape += [jax.ShapeDtypeStruct((m, D_DG), F32), jax.ShapeDtypeStruct((2 * D_DG, m), F32)]
        out_specs += [full(m, D_DG), full(2 * D_DG, m)]
    return pl.pallas_call(
        functools.partial(_proj_cd_kernel, 0, ()),
        grid=(1,),
        in_specs=[full(m, D_MODEL), _resident((1, D_MODEL)), _resident(w.shape),
                  _resident(cw.shape), full(m, D_C)],
        out_specs=out_specs,
        out_shape=out_shape,
        scratch_shapes=[pltpu.VMEM((SUBLANES, D_C), F32), pltpu.VMEM((D_DG // LANES, SUBLANES, LANES), F32)],
        compiler_params=_params("arbitrary"),
    )(x, g, w, cw, hist_rows)


def _dilated_prompt_kernel(n_streams, q_ref, kc_ref, kp_ref, vc_ref, vp_ref, o_ref, lse_ref):
    bl = N_BACK
    n_qblk = q_ref.shape[0] // bl
    first_blk = pl.program_id(2) * n_qblk
    shape = (bl, 2 * bl)
    row = lax.broadcasted_iota(jnp.int32, shape, 0)
    col = lax.broadcasted_iota(jnp.int32, shape, 1)
    dist = row - col + bl
    band = (dist >= 0) & (dist <= N_BACK)
    hw = 2 * HD_D
    lane = lax.broadcasted_iota(jnp.int32, (bl, hw), 1)
    for r in range(n_streams):
        for qb in range(n_qblk):
            rows = slice(qb * bl, (qb + 1) * bl)
            prev_rows = slice((qb - 1) * bl, qb * bl)
            valid = (band & ((first_blk * bl + col - bl) >= 0)) if qb == 0 else band
            pairs = [slice(r * D_DG + p * hw, r * D_DG + (p + 1) * hw) for p in range(H_DG // 2)]
            scores = []
            for sl in pairs:
                qq = _split_heads_stack(q_ref[rows, sl], HD_D)
                k_prev = kp_ref[:, sl] if qb == 0 else kc_ref[prev_rows, sl]
                kcat = jnp.concatenate([k_prev, kc_ref[rows, sl]], axis=0)
                scores.append([_dot_nt(qq[e * bl:(e + 1) * bl], kcat) for e in range(2)])
            probs = []
            for pair_scores in scores:
                stats = []
                for s in pair_scores:
                    s = jnp.where(valid, s, -jnp.inf)
                    m = jnp.max(s, axis=-1, keepdims=True)
                    pr = jnp.exp2(s - m)
                    l = jnp.sum(pr, axis=-1, keepdims=True)
                    stats.append((pr.astype(BF16), l, (m + jnp.log2(l)) * LN_2))
                probs.append(stats)
            for sl, stats in zip(pairs, probs):
                v_prev = vp_ref[:, sl] if qb == 0 else vc_ref[prev_rows, sl]
                vcat = jnp.concatenate([v_prev, vc_ref[rows, sl]], axis=0)
                outs = [_dot(pr, vcat) / l for pr, l, _ in stats]
                o_ref[rows, sl] = jnp.where(lane < HD_D, outs[0], outs[1]).astype(o_ref.dtype)
                lse_ref[rows, sl] = jnp.where(lane < HD_D, stats[0][2], stats[1][2])


def _dilated_prompt(q, k, v, dil, blocks_per_step):
    n_batch, length, _ = q.shape
    bl = N_BACK
    n_qblk = min(blocks_per_step, length // bl)
    n_streams = min(blocks_per_step // n_qblk, dil)
    rows, width = n_qblk * bl, n_streams * D_DG
    cur = pl.BlockSpec((None, rows, width), lambda b, r, j: (b, j, r))
    prev = pl.BlockSpec((None, bl, width), lambda b, r, j: (b, jnp.maximum(j * n_qblk - 1, 0), r))
    return pl.pallas_call(
        functools.partial(_dilated_prompt_kernel, n_streams),
        grid=(n_batch, dil // n_streams, length // rows),
        in_specs=[cur, cur, prev, cur, prev],
        out_specs=[cur, cur],
        out_shape=[jax.ShapeDtypeStruct(q.shape, BF16), jax.ShapeDtypeStruct(q.shape, F32)],
        compiler_params=_params("arbitrary", "arbitrary", "arbitrary"),
    )(q, k, k, v, v)


def _dilated_sample_kernel(dil, q_ref, cache_ref, kvnew_ref, newc_ref, o_ref, lse_ref):
    n_seq = q_ref.shape[0]
    for s in range(n_seq):
        _dilated_sample_one(dil, pl.program_id(0) * n_seq + s, q_ref.at[s], cache_ref.at[s], kvnew_ref,
                            newc_ref.at[s], o_ref.at[s], lse_ref.at[s])


def _dilated_sample_one(dil, seq, q_ref, cache_ref, kvnew_ref, newc_ref, o_ref, lse_ref):
    buf_len = cache_ref.shape[1]
    t_new = q_ref.shape[0]
    hw = 2 * HD_D
    n_blk = buf_len // LANES

    seq_in_block = seq % (LANES // t_new)
    new_tile = pltpu.roll(kvnew_ref[...], (LANES - seq_in_block * t_new) % LANES, 1)

    lane = lax.broadcasted_iota(jnp.int32, (cache_ref.shape[0], LANES), 1)
    cur = pltpu.roll(cache_ref[:, 0:LANES], LANES - t_new, 1)
    for c in range(n_blk):
        following = cache_ref[:, (c + 1) * LANES:(c + 2) * LANES] if c + 1 < n_blk else new_tile
        nxt = pltpu.roll(following, LANES - t_new, 1)
        newc_ref[:, c * LANES:(c + 1) * LANES] = jnp.where(lane < LANES - t_new, cur, nxt)
        cur = nxt

    def iotas(n_keys):
        shape = (2 * t_new, n_keys)
        row = lax.broadcasted_iota(jnp.int32, shape, 0)
        return row & (t_new - 1), lax.broadcasted_iota(jnp.int32, shape, 1)

    qi, col = iotas(buf_len)
    d = buf_len + qi - col
    valid_buf = ((d & (dil - 1)) == 0) & (d <= N_BACK * dil)
    qi, col = iotas(LANES)
    d = qi - col
    valid_new = (col < t_new) & (d >= 0) & ((d & (dil - 1)) == 0)

    q = q_ref[...].astype(BF16)
    lane = lax.broadcasted_iota(jnp.int32, (t_new, hw), 1)
    for p in range(H_DG // 2):
        k_rows, v_rows = slice(p * hw, (p + 1) * hw), slice(D_DG + p * hw, D_DG + (p + 1) * hw)
        qq = _split_heads_stack(q[:, k_rows], HD_D)
        s_buf = jnp.where(valid_buf, _dot(qq, cache_ref[k_rows, :].astype(BF16)), NEG_BIG)
        s_new = jnp.where(valid_new, _dot(qq, new_tile[k_rows].astype(BF16)), NEG_BIG)
        m = jnp.maximum(jnp.max(s_buf, axis=-1, keepdims=True), jnp.max(s_new, axis=-1, keepdims=True))
        p_buf = jnp.where(valid_buf, jnp.exp2(s_buf - m), 0.0)
        p_new = jnp.where(valid_new, jnp.exp2(s_new - m), 0.0)
        l = jnp.sum(p_buf, axis=-1, keepdims=True) + jnp.sum(p_new, axis=-1, keepdims=True)
        acc = (_dot_nt(p_buf.astype(BF16), cache_ref[v_rows, :].astype(BF16))
               + _dot_nt(p_new.astype(BF16), new_tile[v_rows].astype(BF16)))
        o = acc / l
        lse = (m + jnp.log2(l)) * LN_2
        o_ref[:, k_rows] = jnp.where(lane < HD_D, o[0:t_new], o[t_new:2 * t_new])
        lse_ref[:, k_rows] = jnp.where(lane < HD_D, lse[0:t_new], lse[t_new:2 * t_new])


def _dilated_sample(q, cache_t, kvnew_t, dil, n_batch, t_new, seqs_per_step):
    buf_len = cache_t.shape[2]
    steps_per_lane_block = LANES // t_new // seqs_per_step
    small = lambda n: pl.BlockSpec((seqs_per_step, t_new, n), lambda b: (b, 0, 0))
    whole = pl.BlockSpec((seqs_per_step, 2 * D_DG, buf_len), lambda b: (b, 0, 0))
    newc, o, lse = pl.pallas_call(
        functools.partial(_dilated_sample_kernel, dil),
        grid=(n_batch // seqs_per_step,),
        in_specs=[small(D_DG), whole, pl.BlockSpec((2 * D_DG, LANES), lambda b: (0, b // steps_per_lane_block))],
        out_specs=[whole, small(D_DG), small(D_DG)],
        out_shape=[jax.ShapeDtypeStruct(cache_t.shape, F32),
                   jax.ShapeDtypeStruct((n_batch, t_new, D_DG), F32),
                   jax.ShapeDtypeStruct((n_batch, t_new, D_DG), F32)],
        compiler_params=_params("arbitrary"),
    )(q.reshape(n_batch, t_new, D_DG), cache_t, kvnew_t)
    return newc, o.reshape(n_batch * t_new, D_DG), lse.reshape(n_batch * t_new, D_DG)


def kernel(x_prompt, x_sample, state_a_pool, cache_b_kv, state_c_conv, cache_d0_kv, cache_d1_kv, cache_d2_kv, page_table, norm_mix_g, norm_mlp_g, norm_out_g, w_in_ab, w_out_ab, a_mix, a_scale, b_lam, b_subln_g, w_in_cd, w_out_cd, c_conv_w, w_up, w_down):
    bp, seq, _ = x_prompt.shape
    bs, t_new, _ = x_sample.shape
    n_pages = page_table.shape[1]
    past_len = n_pages * cache_b_kv.shape[2]
    mp, ms = bp * seq, bs * t_new
    tm_p = ROW_TILE
    xp = x_prompt.reshape(mp, D_MODEL)
    xs = x_sample.reshape(ms, D_MODEL)
    g2 = lambda v: v.reshape(1, -1)
    d_caches = (cache_d0_kv, cache_d1_kv, cache_d2_kv)

    lam_init = 0.8 - 0.6 * math.exp(-0.3 * 0)
    w_in = w_in_ab[0].astype(BF16)
    w_out = w_out_ab[0].astype(BF16)
    wu, wd = w_up[0].astype(BF16), w_down[0].astype(BF16)
    amix = a_mix[0].astype(BF16)
    gm, gl = g2(norm_mix_g[0]), g2(norm_mlp_g[0])
    ascale, sg = g2(a_scale[0]), g2(b_subln_g[0])

    u_p, q_p, kv_p, k_p, vt_p = _proj_ab(xp, gm, w_in, tm_p, True, ATTN_KEY_BLOCK)
    hist_p = jnp.zeros((bp, POOL_HIST + 1, D_A), F32)
    a_p = _pool_mix(u_p, hist_p, amix, ascale, 0, bp, seq)
    ob_p = _diffattn_prompt(q_p, k_p, vt_p, b_lam[0], sg, lam_init, bp, seq, ATTN_QUERY_CHUNK)

    u_s, q_s, kv_s, _, _ = _proj_ab(xs, gm, w_in, ms, False, ATTN_KEY_BLOCK)
    hist_s = jnp.pad(state_a_pool[0], ((0, 0), (1, 0), (0, 0)))
    a_s = _pool_mix(u_s, hist_s, amix, ascale, past_len, bs, t_new)
    pool_pages = cache_b_kv[0].reshape(cache_b_kv.shape[1], PAGE_SIZE * 2 * H_B, 2 * HD_B)
    xp = _post_mixer(xp, a_p, (ob_p,), (), (), w_out, gl, wu, wd, None, tm_p)
    ob_s = _diffattn_decode(page_table, pool_pages, q_s, kv_s, b_lam[0], sg, lam_init, bs, t_new,
                            min(PAGES_PER_STEP, n_pages))
    xs = _post_mixer(xs, a_s, (ob_s,), (), (), w_out, gl, wu, wd, None, ms)

    u_p3 = u_p.reshape(bp, seq, D_A)
    new_pool_p = u_p3[:, seq - POOL_HIST:][None]
    new_pool_s = jnp.concatenate([state_a_pool[0], u_s.reshape(bs, t_new, D_A)], axis=1)[:, -POOL_HIST:][None]
    new_bkv_p = kv_p.reshape(1, bp, seq, 2, H_B, 2 * HD_B)
    new_bkv_s = kv_s.reshape(1, bs, t_new, 2, H_B, 2 * HD_B)

    w_in = w_in_cd[0].astype(BF16)
    w_out = w_out_cd[0].astype(BF16)
    wu, wd = w_up[1].astype(BF16), w_down[1].astype(BF16)
    gm, gl, gf = g2(norm_mix_g[1]), g2(norm_mlp_g[1]), g2(norm_out_g)
    cw = c_conv_w[0]

    def from_position_minor(a):
        a = a.reshape(a.shape[0], 2, H_DG, HD_D, a.shape[2])
        return jnp.transpose(a, (0, 4, 1, 2, 3))[None]

    outs = _proj_cd_prompt(xp, gm, w_in, cw, bp, seq, tm_p)
    c_p, ztail_p = outs[0], outs[1]
    o_parts, lse_parts, new_d_p = [], [], []
    for gi, (win, dil) in enumerate(D_WINDOWS):
        q_g, k_g, v_g, kvt_g = outs[2 + 4 * gi:6 + 4 * gi]
        o_g, lse_g = _dilated_prompt(q_g, k_g, v_g, dil, DILATED_BLOCKS_PER_STEP)
        if dil == 1:
            o_g, lse_g = o_g.reshape(mp, D_DG), lse_g.reshape(mp, D_DG)
        o_parts.append(o_g)
        lse_parts.append(lse_g)
        new_d_p.append(from_position_minor(kvt_g))
    dils = tuple(dil for _, dil in D_WINDOWS)
    y_p = _post_mixer(xp, c_p, o_parts, lse_parts, dils, w_out, gl, wu, wd, gf, tm_p)
    tiles_per_seq = seq // tm_p
    new_conv_p = ztail_p.reshape(bp, tiles_per_seq, SUBLANES, D_C)[:, -1, SUBLANES - (CONV_W - 1):][None]

    hist_rows = jnp.pad(state_c_conv[0], ((0, 0), (t_new - (CONV_W - 1), 0), (0, 0))).reshape(ms, D_C)
    outs = _proj_cd_sample(xs, gm, w_in, cw, hist_rows)
    c_s, z_s = outs[0], outs[1]
    o_parts, lse_parts, new_d_s = [], [], []
    for gi, (win, dil) in enumerate(D_WINDOWS):
        q_g, kvt_g = outs[2 + 2 * gi:4 + 2 * gi]
        cache = d_caches[gi][0]
        buf_len = cache.shape[1]
        cache_t = jnp.transpose(cache, (0, 2, 3, 4, 1)).reshape(bs, 2 * D_DG, buf_len)
        seqs_per_step = max(1, min(SUBLANES, D_WINDOWS[-1][0] // (2 * buf_len)))
        newc, o_g, lse_g = _dilated_sample(q_g, cache_t, kvt_g, dil, bs, t_new, seqs_per_step)
        o_parts.append(o_g)
        lse_parts.append(lse_g)
        new_d_s.append(from_position_minor(newc))
    y_s = _post_mixer(xs, c_s, o_parts, lse_parts, (1,) * len(D_WINDOWS), w_out, gl, wu, wd, gf, ms)
    new_conv_s = z_s.reshape(bs, t_new, D_C)[:, t_new - (CONV_W - 1):][None]

    return (y_p.reshape(bp, seq, D_MODEL), y_s.reshape(bs, t_new, D_MODEL),
            new_pool_p, new_pool_s, new_bkv_p, new_bkv_s, new_conv_p, new_conv_s,
            new_d_p[0], new_d_s[0], new_d_p[1], new_d_s[1], new_d_p[2], new_d_s[2])
```
